```python
import jax, jax.numpy as jnp
from jax import lax
import numpy as np

D_MODEL = 1024
BATCH = 32
SEQ = 256
DEPTH = 4
DEC_BATCH = 2
DEC_SEQ = 1024
PAST_LEN = 256

GRID_W = 64
BLOCK = 128
ROPE_BASE = 10000.0
EPS = 1e-6
N_Q_HEADS = 8
N_KV_HEADS = 2
HEAD_DIM = 64
ATT_Q = N_Q_HEADS * HEAD_DIM
ATT_KV = N_KV_HEADS * HEAD_DIM
N_RET_HEADS = 4
RET_DK = 128
RET_DV = 128
RET_W = N_RET_HEADS * RET_DK
N_MLP_GROUPS = 4
MLP_W = 512
MLP_GC = MLP_W // N_MLP_GROUPS
N_BRANCH = 3
IN_COLS = ATT_Q + 2 * ATT_KV + 4 * RET_W + 2 * MLP_W + N_BRANCH * D_MODEL
N_MOD = 6
N_EXPERTS = 16
N_EXPERT_GROUPS = 4
EXPERTS_PER_GROUP = N_EXPERTS // N_EXPERT_GROUPS
TOP_K = 2
EXPERT_FF = 512
MOE_BLOCK = 128

kernel_name = 'hybrid_diffusion_prefix_trunk_step'


def rms_norm(x, g):
    xf = x.astype(jnp.float32)
    y = xf * lax.rsqrt(jnp.mean(xf * xf, axis=-1, keepdims=True) + EPS)
    return (y * g.astype(jnp.float32)).astype(x.dtype)


def axial_rope(rows, dim):
    r = jnp.repeat(jnp.arange(rows, dtype=jnp.float32), GRID_W)
    col = jnp.tile(jnp.arange(GRID_W, dtype=jnp.float32), rows)
    half = dim // 2
    inv = ROPE_BASE ** (-jnp.arange(0, half, 2, dtype=jnp.float32) / half)
    ar = r[:, None] * inv
    ac = col[:, None] * inv
    ang = jnp.concatenate([ar, ar, ac, ac], axis=-1)
    return jnp.cos(ang), jnp.sin(ang)


def apply_rope(x, cos, sin):
    x1, x2, x3, x4 = jnp.split(x, 4, axis=-1)
    rot = jnp.concatenate([-x2, x1, -x4, x3], axis=-1)
    return (x * cos[:, None, :] + rot * sin[:, None, :]).astype(x.dtype)


def modulation(cond, w, b):
    m = jnp.einsum('bd,de->be', jax.nn.silu(cond), w) + b
    return [t[:, None, :] for t in jnp.split(m, N_MOD, axis=-1)]


def block_attention(q, k, v):
    B, Sq, Hq, dh = q.shape
    Hkv = k.shape[2]
    G = Hq // Hkv
    qb = q.reshape(B, Sq // BLOCK, BLOCK, Hkv, G, dh).transpose(1, 0, 2, 3, 4, 5)
    scale = dh ** -0.5

    def one_block(qi):
        s = jnp.einsum('bqhgd,bkhd->bhgqk', qi, k, preferred_element_type=jnp.float32) * scale
        p = jax.nn.softmax(s, axis=-1).astype(v.dtype)
        return jnp.einsum('bhgqk,bkhd->bqhgd', p, v)

    o = lax.map(one_block, qb)
    return o.transpose(1, 0, 2, 3, 4, 5).reshape(B, Sq, Hq * dh)


def retention_chunked(q, k, v, log_gamma, s0):
    B, S, H, dk = q.shape
    dv = v.shape[-1]
    N = S // BLOCK
    qc = q.astype(jnp.float32).reshape(B, N, BLOCK, H, dk)
    kc = k.astype(jnp.float32).reshape(B, N, BLOCK, H, dk)
    vc = v.astype(jnp.float32).reshape(B, N, BLOCK, H, dv)
    i = jnp.arange(BLOCK, dtype=jnp.float32)
    diff = i[:, None] - i[None, :]
    dmat = jnp.exp(jnp.where(diff[None] >= 0, diff[None] * log_gamma[:, None, None], -jnp.inf))
    inner = jnp.einsum('bnihd,bnjhd->bnhij', qc, kc) * dmat[None, None]
    o_inner = jnp.einsum('bnhij,bnjhe->bnihe', inner, vc)
    k_dec = kc * jnp.exp((BLOCK - 1 - i)[:, None] * log_gamma)[None, None, :, :, None]
    kv = jnp.einsum('bnjhd,bnjhe->nbhde', k_dec, vc)
    chunk_decay = jnp.exp(BLOCK * log_gamma)[None, :, None, None]

    def step(s, kv_n):
        return chunk_decay * s + kv_n, s

    s_end, s_start = lax.scan(step, s0.astype(jnp.float32), kv)
    q_dec = qc * jnp.exp((i + 1)[:, None] * log_gamma)[None, None, :, :, None]
    o_cross = jnp.einsum('bnihd,nbhde->bnihe', q_dec, s_start)
    return (o_inner + o_cross).reshape(B, S, H, dv), s_end


def bidir_retention(q, k, v, lg_f, lg_b, s0_f, s0_b):
    o_f, s_f = retention_chunked(q, k, v, lg_f, s0_f)
    o_b, s_b = retention_chunked(q[:, ::-1], k[:, ::-1], v[:, ::-1], lg_b, s0_b)
    return o_f + o_b[:, ::-1], s_f, s_b


def head_group_norm(o, g):
    mu = jnp.mean(o, axis=-1, keepdims=True)
    var = jnp.mean(jnp.square(o - mu), axis=-1, keepdims=True)
    y = (o - mu) * lax.rsqrt(var + EPS)
    return y.reshape(o.shape[0], o.shape[1], -1) * g.astype(jnp.float32)


def chunk_mlp(u, v, w_s, b_s, g_norm):
    B, S, _ = u.shape
    vn = rms_norm(v, g_norm).reshape(B, S // BLOCK, BLOCK, N_MLP_GROUPS, MLP_GC)
    f = jnp.einsum('gij,bnjgc->bnigc', w_s, vn) + b_s.T[None, None, :, :, None]
    return u * f.reshape(B, S, MLP_W).astype(u.dtype)


def route(h, w_router, b_router):
    T = h.shape[0]
    scores = jax.nn.sigmoid(jnp.einsum('td,de->te', h, w_router, preferred_element_type=jnp.float32))
    sel = (scores + b_router.astype(jnp.float32)).reshape(T, N_EXPERT_GROUPS, EXPERTS_PER_GROUP)
    group_score = jnp.sum(lax.top_k(sel, TOP_K)[0], axis=-1)
    g_idx = jnp.argmax(group_score, axis=-1)
    in_group = jnp.take_along_axis(sel, g_idx[:, None, None], axis=1)[:, 0]
    _, loc = lax.top_k(in_group, TOP_K)
    eid = g_idx[:, None] * EXPERTS_PER_GROUP + loc
    w = jnp.take_along_axis(scores, eid, axis=-1)
    return eid, w / jnp.sum(w, axis=-1, keepdims=True)


def moe_ffn(h, eid, w, w_gate, w_up, w_down):
    T, D = h.shape
    A = T * TOP_K
    e_flat = eid.reshape(A)
    tok = jnp.arange(A, dtype=jnp.int32) // TOP_K
    order = jnp.argsort(e_flat)
    e_sorted = e_flat[order]
    counts = jnp.bincount(e_flat, length=N_EXPERTS)
    padded = (counts + MOE_BLOCK - 1) // MOE_BLOCK * MOE_BLOCK
    start = jnp.cumsum(counts) - counts
    pend = jnp.cumsum(padded)
    pstart = pend - padded
    dest = pstart[e_sorted] + jnp.arange(A) - start[e_sorted]
    n_blocks = -(-A // MOE_BLOCK) + N_EXPERTS
    P = n_blocks * MOE_BLOCK
    buf_tok = jnp.zeros((P,), jnp.int32).at[dest].set(tok[order])
    xb = h[buf_tok].reshape(n_blocks, MOE_BLOCK, D)
    block_e = jnp.minimum(jnp.searchsorted(pend, jnp.arange(n_blocks) * MOE_BLOCK, side='right'), N_EXPERTS - 1)

    def expert_block(args):
        xi, e = args
        a = jnp.einsum('cd,df->cf', xi, w_gate[e])
        b = jnp.einsum('cd,df->cf', xi, w_up[e])
        return jnp.einsum('cf,fd->cd', jax.nn.silu(a) * b, w_down[e])

    yb = lax.map(expert_block, (xb, block_e)).reshape(P, D)
    contrib = yb[dest] * w.reshape(A)[order][:, None].astype(yb.dtype)
    return jnp.zeros((T, D), h.dtype).at[tok[order]].add(contrib.astype(h.dtype))


def trunk_layer(x, mods, rope_a, rope_r, ctx, lw):
    B, S, _ = x.shape
    shift1, scale1, gate1, shift2, scale2, gate2 = mods
    h = rms_norm(x, lw['norm1']) * (1 + scale1) + shift1
    z = jnp.einsum('bsd,de->bse', h, lw['w_in'])
    sizes = (ATT_Q, ATT_KV, ATT_KV, RET_W, RET_W, RET_W, RET_W, MLP_W, MLP_W, D_MODEL, D_MODEL, D_MODEL)
    cuts = np.cumsum(sizes)[:-1].tolist()
    qa, ka, va, qr, kr, vr, gr, mu, mv, g_att, g_ret, g_mlp = jnp.split(z, cuts, axis=-1)

    qa = rms_norm(qa.reshape(B, S, N_Q_HEADS, HEAD_DIM), lw['q_norm'])
    ka = rms_norm(ka.reshape(B, S, N_KV_HEADS, HEAD_DIM), lw['k_norm'])
    va = va.reshape(B, S, N_KV_HEADS, HEAD_DIM)
    if rope_a is not None:
        qa = apply_rope(qa, *rope_a)
        ka = apply_rope(ka, *rope_a)
    if ctx is None:
        k_all, v_all = ka, va
        s0_f = jnp.zeros((B, N_RET_HEADS, RET_DK, RET_DV), jnp.float32)
        s0_b = s0_f
    else:
        k_ctx, v_ctx, s0_f, s0_b = ctx
        k_all = jnp.concatenate([ka, k_ctx.astype(ka.dtype)], axis=1)
        v_all = jnp.concatenate([va, v_ctx.astype(va.dtype)], axis=1)
    att = block_attention(qa, k_all, v_all)

    qr = qr.reshape(B, S, N_RET_HEADS, RET_DK)
    kr = kr.reshape(B, S, N_RET_HEADS, RET_DK) * (RET_DK ** -0.5)
    vr = vr.reshape(B, S, N_RET_HEADS, RET_DV)
    if rope_r is not None:
        qr = apply_rope(qr, *rope_r)
        kr = apply_rope(kr, *rope_r)
    lg_f = -jnp.exp(lw['ret_decay_fwd'].astype(jnp.float32))
    lg_b = -jnp.exp(lw['ret_decay_bwd'].astype(jnp.float32))
    o, s_f, s_b = bidir_retention(qr, kr, vr, lg_f, lg_b, s0_f, s0_b)
    ret = (head_group_norm(o, lw['ret_gn']) * jax.nn.silu(gr.astype(jnp.float32))).astype(x.dtype)

    cm = chunk_mlp(jax.nn.gelu(mu), jax.nn.gelu(mv), lw['w_spatial'], lw['b_spatial'], lw['mlp_norm'])

    mix = (jax.nn.sigmoid(g_att) * jnp.einsum('bsc,cd->bsd', att, lw['w_att_o'])
           + jax.nn.sigmoid(g_ret) * jnp.einsum('bsc,cd->bsd', ret, lw['w_ret_o'])
           + jax.nn.sigmoid(g_mlp) * jnp.einsum('bsc,cd->bsd', cm, lw['w_mlp_o']))
    x = x + gate1 * jnp.einsum('bsd,de->bse', mix, lw['w_out'])

    h2 = (rms_norm(x, lw['norm2']) * (1 + scale2) + shift2).reshape(B * S, D_MODEL)
    eid, wt = route(h2, lw['w_router'], lw['b_router'])
    y = moe_ffn(h2, eid, wt, lw['w_gate'], lw['w_up'], lw['w_down']).reshape(B, S, D_MODEL)
    x = x + gate2 * y
    return x, ka, va, s_f.astype(x.dtype), s_b.astype(x.dtype)


def setup_inputs(seed: int = 0) -> dict:
    key = jax.random.key(seed)
    ks = jax.random.split(key, 32)
    f32 = jnp.float32

    def nrm(k, shape, scale):
        return jax.random.normal(k, shape, f32) * scale

    decay_base = jnp.log(-jnp.log1p(-(2.0 ** (-5.0 - jnp.arange(N_RET_HEADS, dtype=f32)))))
    return {
        'x_prompt': nrm(ks[0], (BATCH, SEQ, D_MODEL), 1.0),
        'x_sample': nrm(ks[1], (DEC_BATCH, DEC_SEQ, D_MODEL), 1.0),
        'cache_k': nrm(ks[2], (DEC_BATCH, DEPTH, PAST_LEN, N_KV_HEADS, HEAD_DIM), 1.0),
        'cache_v': nrm(ks[3], (DEC_BATCH, DEPTH, PAST_LEN, N_KV_HEADS, HEAD_DIM), 1.0),
        'state_ret_fwd': nrm(ks[4], (DEC_BATCH, DEPTH, N_RET_HEADS, RET_DK, RET_DV), 0.5),
        'state_ret_bwd': nrm(ks[5], (DEC_BATCH, DEPTH, N_RET_HEADS, RET_DK, RET_DV), 0.5),
        'c': nrm(ks[6], (DEC_BATCH, D_MODEL), 1.0),
        'c_ctx': nrm(ks[7], (D_MODEL,), 1.0),
        'w_mod': nrm(ks[8], (DEPTH, D_MODEL, N_MOD * D_MODEL), 0.5 * D_MODEL ** -0.5),
        'b_mod': nrm(ks[9], (DEPTH, N_MOD * D_MODEL), 0.02),
        'norm1': 1.0 + nrm(ks[10], (DEPTH, D_MODEL), 0.02),
        'norm2': 1.0 + nrm(ks[11], (DEPTH, D_MODEL), 0.02),
        'w_in': nrm(ks[12], (DEPTH, D_MODEL, IN_COLS), D_MODEL ** -0.5),
        'q_norm': 1.0 + nrm(ks[13], (DEPTH, HEAD_DIM), 0.02),
        'k_norm': 1.0 + nrm(ks[14], (DEPTH, HEAD_DIM), 0.02),
        'ret_decay_fwd': decay_base[None, :] + nrm(ks[15], (DEPTH, N_RET_HEADS), 0.01),
        'ret_decay_bwd': decay_base[None, :] + nrm(ks[16], (DEPTH, N_RET_HEADS), 0.01),
        'ret_gn': 1.0 + nrm(ks[17], (DEPTH, RET_W), 0.02),
        'mlp_norm': 1.0 + nrm(ks[18], (DEPTH, MLP_W), 0.02),
        'w_spatial': nrm(ks[19], (DEPTH, N_MLP_GROUPS, BLOCK, BLOCK), BLOCK ** -0.5),
        'b_spatial': 1.0 + nrm(ks[20], (DEPTH, N_MLP_GROUPS, BLOCK), 0.02),
        'w_att_o': nrm(ks[21], (DEPTH, ATT_Q, D_MODEL), ATT_Q ** -0.5),
        'w_ret_o': nrm(ks[22], (DEPTH, RET_W, D_MODEL), RET_W ** -0.5),
        'w_mlp_o': nrm(ks[23], (DEPTH, MLP_W, D_MODEL), MLP_W ** -0.5),
        'w_out': nrm(ks[24], (DEPTH, D_MODEL, D_MODEL), D_MODEL ** -0.5),
        'w_router': nrm(ks[25], (D_MODEL, N_EXPERTS), D_MODEL ** -0.5),
        'b_router': nrm(ks[26], (N_EXPERTS,), 0.01),
        'w_gate': nrm(ks[27], (DEPTH, N_EXPERTS, D_MODEL, EXPERT_FF), D_MODEL ** -0.5),
        'w_up': nrm(ks[28], (DEPTH, N_EXPERTS, D_MODEL, EXPERT_FF), D_MODEL ** -0.5),
        'w_down': nrm(ks[29], (DEPTH, N_EXPERTS, EXPERT_FF, D_MODEL), EXPERT_FF ** -0.5),
    }


def reference(x_prompt, x_sample, cache_k, cache_v, state_ret_fwd, state_ret_bwd, c, c_ctx,
              w_mod, b_mod, norm1, norm2, w_in, q_norm, k_norm, ret_decay_fwd, ret_decay_bwd,
              ret_gn, mlp_norm, w_spatial, b_spatial, w_att_o, w_ret_o, w_mlp_o, w_out,
              w_router, b_router, w_gate, w_up, w_down):
    lws = [dict(norm1=norm1[l], norm2=norm2[l], w_in=w_in[l], q_norm=q_norm[l], k_norm=k_norm[l],
                ret_decay_fwd=ret_decay_fwd[l], ret_decay_bwd=ret_decay_bwd[l], ret_gn=ret_gn[l],
                mlp_norm=mlp_norm[l], w_spatial=w_spatial[l], b_spatial=b_spatial[l],
                w_att_o=w_att_o[l], w_ret_o=w_ret_o[l], w_mlp_o=w_mlp_o[l], w_out=w_out[l],
                w_router=w_router, b_router=b_router,
                w_gate=w_gate[l], w_up=w_up[l], w_down=w_down[l]) for l in range(DEPTH)]

    x = x_prompt
    ks, vs, sfs, sbs = [], [], [], []
    for l in range(DEPTH):
        mods = modulation(c_ctx[None, :], w_mod[l], b_mod[l])
        x, k_l, v_l, sf_l, sb_l = trunk_layer(x, mods, None, None, None, lws[l])
        ks.append(k_l)
        vs.append(v_l)
        sfs.append(sf_l)
        sbs.append(sb_l)
    y_prompt = x
    new_k = jnp.stack(ks, axis=1)
    new_v = jnp.stack(vs, axis=1)
    new_sf = jnp.stack(sfs, axis=1)
    new_sb = jnp.stack(sbs, axis=1)

    rows = x_sample.shape[1] // GRID_W
    rope_a = axial_rope(rows, HEAD_DIM)
    rope_r = axial_rope(rows, RET_DK)
    x = x_sample
    for l in range(DEPTH):
        mods = modulation(c, w_mod[l], b_mod[l])
        ctx = (cache_k[:, l], cache_v[:, l], state_ret_fwd[:, l], state_ret_bwd[:, l])
        x, _, _, _, _ = trunk_layer(x, mods, rope_a, rope_r, ctx, lws[l])
    y_sample = x
    return (y_prompt, y_sample, new_k, new_v, new_sf, new_sb)
```

```python
import functools

import numpy as np
import jax
import jax.numpy as jnp
from jax import lax
from jax.experimental import pallas as pl
from jax.experimental.pallas import tpu as pltpu

F32 = jnp.float32
BF16 = jnp.bfloat16

D_MODEL = 1024
BATCH = 32
SEQ = 256
DEPTH = 4
DEC_BATCH = 2
DEC_SEQ = 1024
PAST_LEN = 256
GRID_W = 64
BLOCK = 128
ROPE_BASE = 10000.0
EPS = 1e-6
N_Q_HEADS = 8
N_KV_HEADS = 2
HEAD_DIM = 64
ATT_Q = N_Q_HEADS * HEAD_DIM
ATT_KV = N_KV_HEADS * HEAD_DIM
N_RET_HEADS = 4
RET_DK = 128
RET_DV = 128
RET_W = N_RET_HEADS * RET_DK
N_MLP_GROUPS = 4
MLP_W = 512
N_MOD = 6
N_EXPERTS = 16
N_EXPERT_GROUPS = 4
EXPERTS_PER_GROUP = N_EXPERTS // N_EXPERT_GROUPS
TOP_K = 2
EXPERT_FF = 512
MOE_BLOCK = 128

T_CTX = BATCH * SEQ
T_LAT = DEC_BATCH * DEC_SEQ
T_ALL = T_CTX + T_LAT
N_ASSIGN = T_ALL * TOP_K
N_MOE_BLOCKS = -(-N_ASSIGN // MOE_BLOCK) + N_EXPERTS
N_SLOTS = N_MOE_BLOCKS * MOE_BLOCK

ROW_TILE = 512
V7X_VMEM_LIMIT_BYTES = 56 * 1024 * 1024

Z_SPLITS = (("qa", ATT_Q), ("kv", 2 * ATT_KV), ("qr", RET_W), ("kr", RET_W), ("vr", RET_W), ("gr", RET_W),
            ("mu", MLP_W), ("mv", MLP_W), ("ga", D_MODEL), ("gb", D_MODEL), ("gc", D_MODEL))
DOT_CHUNK = 512


def _params(sem, vmem=None):
    return pltpu.CompilerParams(dimension_semantics=sem, vmem_limit_bytes=vmem)


def _rms(x, g):
    return x * lax.rsqrt(jnp.mean(x * x, axis=-1, keepdims=True) + EPS) * g


def _sigmoid(x):
    return jax.nn.sigmoid(x)


def _mod_row(i):
    first_lat = T_CTX // ROW_TILE
    per_batch = DEC_SEQ // ROW_TILE
    return jnp.where(i < first_lat, 0, 1 + (i - first_lat) // per_batch)


def _mod_kernel(cond_ref, w_ref, b_ref, o_ref):
    c = cond_ref[...]
    s = c * _sigmoid(c)
    o_ref[...] = jnp.dot(s, w_ref[...], preferred_element_type=F32,
                         precision=lax.Precision.HIGHEST) + b_ref[...]


def _modulation(cond, w_mod, b_mod):
    tn = D_MODEL
    out = pl.pallas_call(
        _mod_kernel,
        grid=(DEPTH, N_MOD),
        in_specs=[pl.BlockSpec((8, D_MODEL), lambda l, j: (0, 0)),
                  pl.BlockSpec((None, D_MODEL, tn), lambda l, j: (l, 0, j)),
                  pl.BlockSpec((None, 1, tn), lambda l, j: (l, 0, j))],
        out_specs=pl.BlockSpec((None, 8, tn), lambda l, j: (l, 0, j)),
        out_shape=jax.ShapeDtypeStruct((DEPTH, 8, N_MOD * D_MODEL), F32),
        compiler_params=_params(("arbitrary", "arbitrary")),
        name="modulation",
    )(cond, w_mod, b_mod.reshape(DEPTH, 1, N_MOD * D_MODEL))
    return out.reshape(DEPTH, 8, N_MOD, D_MODEL)


def _in_kernel(*refs, combine):
    it = iter(refs)
    x_ref = next(it)
    if combine:
        y0_ref, y1_ref, wc_ref, pmod_ref = next(it), next(it), next(it), next(it)
    mod_ref, g_ref, w_ref = next(it), next(it), next(it)
    if combine:
        xo_ref = next(it)
    outs = [next(it) for _ in Z_SPLITS]

    x = x_ref[...]
    if combine:
        wc = wc_ref[...]
        x = x + pmod_ref[5:6, :] * (wc[:, 0:1] * y0_ref[...] + wc[:, 1:2] * y1_ref[...])
        xo_ref[...] = x
    h = (_rms(x, g_ref[...]) * (1.0 + mod_ref[1:2, :]) + mod_ref[0:1, :]).astype(BF16)
    col = 0
    for (_, width), o_ref in zip(Z_SPLITS, outs):
        for c in range(0, width, DOT_CHUNK):
            cw = min(DOT_CHUNK, width - c)
            o_ref[:, c:c + cw] = jnp.dot(h, w_ref[:, col + c:col + c + cw],
                                         preferred_element_type=F32).astype(BF16)
        col += width


def _input_projection(l, x, mods, norm1, w_in_bf, moe_out=None, wcol=None):
    combine = moe_out is not None
    tm = ROW_TILE
    nt = T_ALL // tm
    row = lambda i: (i, 0)
    mod_spec = lambda ll: pl.BlockSpec((None, None, N_MOD, D_MODEL), lambda i: (ll, _mod_row(i), 0, 0))
    in_specs = [pl.BlockSpec((tm, D_MODEL), row)]
    args = [x]
    if combine:
        in_specs += [pl.BlockSpec((tm, D_MODEL), row),
                     pl.BlockSpec((tm, D_MODEL), lambda i: (nt + i, 0)),
                     pl.BlockSpec((tm, TOP_K), row),
                     mod_spec(l - 1)]
        args += [moe_out, moe_out, wcol, mods]
    in_specs += [mod_spec(l),
                 pl.BlockSpec((None, 1, D_MODEL), lambda i: (l, 0, 0)),
                 pl.BlockSpec((None, D_MODEL, w_in_bf.shape[-1]), lambda i: (l, 0, 0),
                              pipeline_mode=pl.Buffered(1))]
    args += [mods, norm1.reshape(DEPTH, 1, D_MODEL), w_in_bf]
    out_specs, out_shape = [], []
    if combine:
        out_specs.append(pl.BlockSpec((tm, D_MODEL), row))
        out_shape.append(jax.ShapeDtypeStruct((T_ALL, D_MODEL), F32))
    for _, width in Z_SPLITS:
        out_specs.append(pl.BlockSpec((tm, width), row))
        out_shape.append(jax.ShapeDtypeStruct((T_ALL, width), BF16))
    res = pl.pallas_call(
        functools.partial(_in_kernel, combine=combine),
        grid=(nt,),
        in_specs=in_specs, out_specs=out_specs, out_shape=out_shape,
        compiler_params=_params(("arbitrary",), V7X_VMEM_LIMIT_BYTES),
        name="input_projection",
    )(*args)
    if combine:
        x, res = res[0], res[1:]
    return x, {name: r for (name, _), r in zip(Z_SPLITS, res)}


def _rope_tables(rows, dim, reps):
    r = jnp.repeat(jnp.arange(rows, dtype=F32), GRID_W)
    col = jnp.tile(jnp.arange(GRID_W, dtype=F32), rows)
    half = dim // 2
    inv = ROPE_BASE ** (-jnp.arange(0, half, 2, dtype=F32) / half)
    ar = r[:, None] * inv
    ac = col[:, None] * inv
    ang = jnp.concatenate([ar, ar, ac, ac], axis=-1)
    cos, sin = jnp.cos(ang), jnp.sin(ang)
    first = (jnp.arange(dim) % (dim // 2)) < (dim // 4)
    sin_up = jnp.where(first, -sin, 0.0)
    sin_dn = jnp.where(first, 0.0, sin)
    t = lambda a: jnp.tile(a, (1, reps))
    return t(cos), t(sin_up), t(sin_dn)


def _rope(x, cos, sin_up, sin_dn, quarter):
    w = x.shape[-1]
    return x * cos + pltpu.roll(x, w - quarter, 1) * sin_up + pltpu.roll(x, quarter, 1) * sin_dn


def _attn_kernel(*refs, sk_new, use_rope, has_cache, emit_kv):
    it = iter(refs)
    q_ref, kv_ref, gq_ref, gk_ref, bdq_ref, bdk_ref = (next(it) for _ in range(6))
    if use_rope:
        cq_ref, suq_ref, sdq_ref, ck_ref, suk_ref, sdk_ref = (next(it) for _ in range(6))
    if has_cache:
        kc_ref, vc_ref = next(it), next(it)
    o_ref = next(it)
    if emit_kv:
        nk_ref, nv_ref = next(it), next(it)
    kd_scr, vd_scr = next(it), next(it)

    lo = lax.broadcasted_iota(jnp.int32, (1, 2 * HEAD_DIM), 1) < HEAD_DIM

    def dup_halves(a):
        r = pltpu.roll(a, HEAD_DIM, 1)
        return jnp.where(lo, a, r), jnp.where(lo, r, a)

    @pl.when(pl.program_id(1) == 0)
    def _prep():
        kv = kv_ref[...].astype(F32)
        k, v = kv[:, :ATT_KV], kv[:, ATT_KV:]
        ms = jnp.dot((k * k).astype(BF16), bdk_ref[...], preferred_element_type=F32)
        kn = k * lax.rsqrt(ms + EPS) * gk_ref[...]
        if emit_kv:
            nk_ref[...] = kn
            nv_ref[...] = v
        if use_rope:
            kn = _rope(kn, ck_ref[...], suk_ref[...], sdk_ref[...], HEAD_DIM // 4)
        k0, k1 = dup_halves(kn)
        v0, v1 = dup_halves(v)
        kd_scr[0, 0:sk_new, :] = k0.astype(BF16)
        kd_scr[1, 0:sk_new, :] = k1.astype(BF16)
        vd_scr[0, 0:sk_new, :] = v0.astype(BF16)
        vd_scr[1, 0:sk_new, :] = v1.astype(BF16)
        if has_cache:
            c0, c1 = dup_halves(kc_ref[...])
            d0, d1 = dup_halves(vc_ref[...])
            kd_scr[0, sk_new:, :] = c0.astype(BF16)
            kd_scr[1, sk_new:, :] = c1.astype(BF16)
            vd_scr[0, sk_new:, :] = d0.astype(BF16)
            vd_scr[1, sk_new:, :] = d1.astype(BF16)

    q = q_ref[...].astype(F32)
    ms = jnp.dot((q * q).astype(BF16), bdq_ref[...], preferred_element_type=F32)
    qn = q * lax.rsqrt(ms + EPS) * gq_ref[...]
    if use_rope:
        qn = _rope(qn, cq_ref[...], suq_ref[...], sdq_ref[...], HEAD_DIM // 4)
    qn = qn * (HEAD_DIM ** -0.5)
    heads_per_kv = N_Q_HEADS // N_KV_HEADS
    for j in range(N_Q_HEADS // 2):
        grp = (2 * j) // heads_per_kv
        kd, vd = kd_scr[grp], vd_scr[grp]
        qp = qn[:, 2 * HEAD_DIM * j:2 * HEAD_DIM * (j + 1)]
        halves = []
        for qm in (jnp.where(lo, qp, 0.0), jnp.where(lo, 0.0, qp)):
            s = lax.dot_general(qm.astype(BF16), kd, (((1,), (1,)), ((), ())), preferred_element_type=F32)
            e = jnp.exp(s - jnp.max(s, axis=-1, keepdims=True))
            den = jnp.sum(e, axis=-1, keepdims=True)
            halves.append(jnp.dot(e.astype(BF16), vd, preferred_element_type=F32) / den)
        o_ref[:, 2 * HEAD_DIM * j:2 * HEAD_DIM * (j + 1)] = jnp.where(lo, halves[0], halves[1]).astype(BF16)


def _block_diag_mean(width, group):
    idx = np.arange(width) // group
    return jnp.asarray((idx[:, None] == idx[None, :]).astype(np.float32) / group, dtype=BF16)


def _attention(l, z, q_norm, k_norm, *, latent, rope=None, cache_k=None, cache_v=None):
    if latent:
        nb, s, tq, row0 = DEC_BATCH, DEC_SEQ, 256, T_CTX
    else:
        nb, s, tq, row0 = BATCH, SEQ, SEQ, 0
    nq = s // tq
    sk = s + (PAST_LEN if latent else 0)
    qrow = lambda b, qi: (row0 // tq + b * nq + qi, 0)
    krow = lambda b, qi: (row0 // s + b, 0)
    const = lambda b, qi: (0, 0)
    lrow = lambda b, qi: (l, 0, 0)
    in_specs = [pl.BlockSpec((tq, ATT_Q), qrow),
                pl.BlockSpec((s, 2 * ATT_KV), krow),
                pl.BlockSpec((None, 1, ATT_Q), lrow),
                pl.BlockSpec((None, 1, ATT_KV), lrow),
                pl.BlockSpec((ATT_Q, ATT_Q), const),
                pl.BlockSpec((ATT_KV, ATT_KV), const)]
    args = [z["qa"], z["kv"],
            jnp.tile(q_norm, (1, N_Q_HEADS)).reshape(DEPTH, 1, ATT_Q),
            jnp.tile(k_norm, (1, N_KV_HEADS)).reshape(DEPTH, 1, ATT_KV),
            _block_diag_mean(ATT_Q, HEAD_DIM), _block_diag_mean(ATT_KV, HEAD_DIM)]
    if latent:
        cq, suq, sdq, ck, suk, sdk = rope
        in_specs += [pl.BlockSpec((tq, ATT_Q), lambda b, qi: (qi, 0))] * 3
        in_specs += [pl.BlockSpec((s, ATT_KV), const)] * 3
        in_specs += [pl.BlockSpec((None, None, PAST_LEN, ATT_KV), lambda b, qi: (b, l, 0, 0))] * 2
        args += [cq, suq, sdq, ck, suk, sdk,
                 cache_k.reshape(DEC_BATCH, DEPTH, PAST_LEN, ATT_KV),
                 cache_v.reshape(DEC_BATCH, DEPTH, PAST_LEN, ATT_KV)]
    rows = nb * s
    out_specs = [pl.BlockSpec((tq, ATT_Q), lambda b, qi: (b * nq + qi, 0))]
    out_shape = [jax.ShapeDtypeStruct((rows, ATT_Q), BF16)]
    if not latent:
        out_specs += [pl.BlockSpec((s, ATT_KV), lambda b, qi: (b, 0))] * 2
        out_shape += [jax.ShapeDtypeStruct((rows, ATT_KV), F32)] * 2
    return pl.pallas_call(
        functools.partial(_attn_kernel, sk_new=s, use_rope=latent, has_cache=latent, emit_kv=not latent),
        grid=(nb, nq),
        in_specs=in_specs, out_specs=out_specs, out_shape=out_shape,
        scratch_shapes=[pltpu.VMEM((N_KV_HEADS, sk, 2 * HEAD_DIM), BF16),
                        pltpu.VMEM((N_KV_HEADS, sk, 2 * HEAD_DIM), BF16)],
        compiler_params=_params(("arbitrary", "arbitrary"), V7X_VMEM_LIMIT_BYTES),
        name="attention_latent" if latent else "attention_context",
    )(*args)


def _ret_kernel(*refs, s, use_rope, has_state):
    it = iter(refs)
    q_ref, k_ref, v_ref, g_ref, dec_ref, gn_ref = (next(it) for _ in range(6))
    if use_rope:
        c_ref, su_ref, sd_ref = next(it), next(it), next(it)
    if has_state:
        s0f_ref, s0b_ref = next(it), next(it)
    o_ref, sf_ref, sb_ref, oacc = next(it), next(it), next(it), next(it)

    n_chunks = s // BLOCK
    lgf = -jnp.exp(dec_ref[0])
    lgb = -jnp.exp(dec_ref[1])
    lgf1, lgb1 = lgf[:, 0:1], lgb[:, 0:1]
    diff = (lax.broadcasted_iota(jnp.int32, (BLOCK, BLOCK), 0)
            - lax.broadcasted_iota(jnp.int32, (BLOCK, BLOCK), 1)).astype(F32)
    dsum = (jnp.where(diff >= 0, jnp.exp(diff * lgf), 0.0)
            + jnp.where(diff <= 0, jnp.exp(-diff * lgb), 0.0))
    ic = lax.broadcasted_iota(jnp.int32, (BLOCK, 1), 0).astype(F32)
    qdf, kdf, cdf = jnp.exp((ic + 1.0) * lgf1), jnp.exp((BLOCK - 1.0 - ic) * lgf1), jnp.exp(BLOCK * lgf1)
    qdb, kdb, cdb = jnp.exp((BLOCK - ic) * lgb1), jnp.exp(ic * lgb1), jnp.exp(BLOCK * lgb1)

    q = q_ref[...].astype(F32)
    k = k_ref[...].astype(F32) * (RET_DK ** -0.5)
    if use_rope:
        q = _rope(q, c_ref[...], su_ref[...], sd_ref[...], RET_DK // 4)
        k = _rope(k, c_ref[...], su_ref[...], sd_ref[...], RET_DK // 4)
    v = v_ref[...]

    def rows(a, n):
        return a[n * BLOCK:(n + 1) * BLOCK]

    def state_step(state, kn, kdec, vn, cdec):
        kd_t = jnp.transpose(kn * kdec).astype(BF16)
        return cdec * state + jnp.dot(kd_t, vn, preferred_element_type=F32)

    state = s0f_ref[...] if has_state else jnp.zeros((RET_DK, RET_DV), F32)
    for n in range(n_chunks):
        qn, kn, vn = rows(q, n), rows(k, n), rows(v, n)
        inner = lax.dot_general(qn.astype(BF16), kn.astype(BF16), (((1,), (1,)), ((), ())),
                                preferred_element_type=F32) * dsum
        o = (jnp.dot(inner.astype(BF16), vn, preferred_element_type=F32)
             + jnp.dot((qn * qdf).astype(BF16), state.astype(BF16), preferred_element_type=F32))
        state = state_step(state, kn, kdf, vn, cdf)
        oacc[n * BLOCK:(n + 1) * BLOCK, :] = o
    sf_ref[...] = state

    state = s0b_ref[...] if has_state else jnp.zeros((RET_DK, RET_DV), F32)
    for n in reversed(range(n_chunks)):
        qn, kn, vn = rows(q, n), rows(k, n), rows(v, n)
        o = rows(oacc, n) + jnp.dot((qn * qdb).astype(BF16), state.astype(BF16), preferred_element_type=F32)
        state = state_step(state, kn, kdb, vn, cdb)
        mu = jnp.mean(o, axis=-1, keepdims=True)
        var = jnp.mean(jnp.square(o - mu), axis=-1, keepdims=True)
        y = (o - mu) * lax.rsqrt(var + EPS) * gn_ref[...]
        gate = g_ref[n * BLOCK:(n + 1) * BLOCK, :].astype(F32)
        o_ref[n * BLOCK:(n + 1) * BLOCK, :] = (y * (gate * _sigmoid(gate))).astype(BF16)
    sb_ref[...] = state


def _retention(l, z, dec, ret_gn, *, latent, rope=None, s0f=None, s0b=None):
    if latent:
        nb, s, row0 = DEC_BATCH, DEC_SEQ, T_CTX
    else:
        nb, s, row0 = BATCH, SEQ, 0
    hrow = lambda b, h: (row0 // s + b, h)
    in_specs = [pl.BlockSpec((s, RET_DK), hrow)] * 4
    in_specs += [pl.BlockSpec((None, 2, None, 1, RET_DK), lambda b, h: (l, 0, h, 0, 0)),
                 pl.BlockSpec((None, None, 1, RET_DV), lambda b, h: (l, h, 0, 0))]
    args = [z["qr"], z["kr"], z["vr"], z["gr"], dec, ret_gn.reshape(DEPTH, N_RET_HEADS, 1, RET_DV)]
    if latent:
        in_specs += [pl.BlockSpec((s, RET_DK), lambda b, h: (0, 0))] * 3
        in_specs += [pl.BlockSpec((None, None, None, RET_DK, RET_DV), lambda b, h: (b, l, h, 0, 0))] * 2
        args += [*rope, s0f, s0b]
    st_spec = pl.BlockSpec((None, None, RET_DK, RET_DV), lambda b, h: (b, h, 0, 0))
    st_shape = jax.ShapeDtypeStruct((nb, N_RET_HEADS, RET_DK, RET_DV), F32)
    return pl.pallas_call(
        functools.partial(_ret_kernel, s=s, use_rope=latent, has_state=latent),
        grid=(nb, N_RET_HEADS),
        in_specs=in_specs,
        out_specs=[pl.BlockSpec((s, RET_DV), lambda b, h: (b, h)), st_spec, st_spec],
        out_shape=[jax.ShapeDtypeStruct((nb * s, RET_W), BF16), st_shape, st_shape],
        scratch_shapes=[pltpu.VMEM((s, RET_DV), F32)],
        compiler_params=_params(("arbitrary", "arbitrary")),
        name="retention_latent" if latent else "retention_context",
    )(*args)


def _cmlp_kernel(mu_ref, mv_ref, gn_ref, ws_ref, bs_ref, o_ref):
    u = jax.nn.gelu(mu_ref[...].astype(F32), approximate=True)
    v = jax.nn.gelu(mv_ref[...].astype(F32), approximate=True)
    vn = _rms(v, gn_ref[...]).astype(BF16)
    gc = MLP_W // N_MLP_GROUPS
    for n in range(ROW_TILE // BLOCK):
        r = slice(n * BLOCK, (n + 1) * BLOCK)
        for g in range(N_MLP_GROUPS):
            c = slice(g * gc, (g + 1) * gc)
            f = jnp.dot(ws_ref[g], vn[r, c], preferred_element_type=F32) + bs_ref[g]
            o_ref[r, c] = (u[r, c] * f).astype(BF16)


def _chunk_mlp(l, z, mlp_norm, w_spatial_bf, b_spatial):
    tm = ROW_TILE
    row = lambda i: (i, 0)
    return pl.pallas_call(
        _cmlp_kernel,
        grid=(T_ALL // tm,),
        in_specs=[pl.BlockSpec((tm, MLP_W), row), pl.BlockSpec((tm, MLP_W), row),
                  pl.BlockSpec((None, 1, MLP_W), lambda i: (l, 0, 0)),
                  pl.BlockSpec((None, N_MLP_GROUPS, BLOCK, BLOCK), lambda i: (l, 0, 0, 0)),
                  pl.BlockSpec((None, N_MLP_GROUPS, BLOCK, 1), lambda i: (l, 0, 0, 0))],
        out_specs=pl.BlockSpec((tm, MLP_W), row),
        out_shape=jax.ShapeDtypeStruct((T_ALL, MLP_W), BF16),
        compiler_params=_params(("arbitrary",)),
        name="chunk_mlp",
    )(z["mu"], z["mv"], mlp_norm.reshape(DEPTH, 1, MLP_W), w_spatial_bf,
      b_spatial.reshape(DEPTH, N_MLP_GROUPS, BLOCK, 1))


def _first_max(vals):
    best, idx = vals[0], jnp.zeros(vals[0].shape, jnp.int32)
    for j in range(1, len(vals)):
        upd = vals[j] > best
        best = jnp.where(upd, vals[j], best)
        idx = jnp.where(upd, j, idx)
    return best, idx


def _pick(idx, vals):
    out = vals[-1]
    for j in range(len(vals) - 2, -1, -1):
        out = jnp.where(idx == j, vals[j], out)
    return out


def _merge_kernel(x_ref, att_ref, ret_ref, cm_ref, ga_ref, gb_ref, gc_ref, mod_ref, g2_ref,
                  wa_ref, wr_ref, wm_ref, wo_ref, wrt_ref, br_ref,
                  x1_ref, h2_ref, eid_ref, wts_ref):
    sig = lambda r: _sigmoid(r[...].astype(F32))
    dot = lambda a, b: jnp.dot(a, b, preferred_element_type=F32)
    mix = (sig(ga_ref) * dot(att_ref[...], wa_ref[...])
           + sig(gb_ref) * dot(ret_ref[...], wr_ref[...])
           + sig(gc_ref) * dot(cm_ref[...], wm_ref[...]))
    x1 = x_ref[...] + mod_ref[2:3, :] * dot(mix.astype(BF16), wo_ref[...])
    x1_ref[...] = x1
    h2 = _rms(x1, g2_ref[...]) * (1.0 + mod_ref[4:5, :]) + mod_ref[3:4, :]
    h2_ref[...] = h2

    logits = lax.dot_general(wrt_ref[...], h2, (((1,), (1,)), ((), ())), preferred_element_type=F32,
                             precision=lax.Precision.HIGHEST)
    score = _sigmoid(logits)
    sel = score + br_ref[...]
    sel_rows = [sel[e:e + 1, :] for e in range(N_EXPERTS)]
    score_rows = [score[e:e + 1, :] for e in range(N_EXPERTS)]
    group_scores = []
    for g in range(N_EXPERT_GROUPS):
        v = sel_rows[g * EXPERTS_PER_GROUP:(g + 1) * EXPERTS_PER_GROUP]
        pair_sums = [v[a] + v[b] for a in range(EXPERTS_PER_GROUP) for b in range(a + 1, EXPERTS_PER_GROUP)]
        group_scores.append(functools.reduce(jnp.maximum, pair_sums))
    _, gidx = _first_max(group_scores)
    in_sel = [_pick(gidx, [sel_rows[g * EXPERTS_PER_GROUP + j] for g in range(N_EXPERT_GROUPS)])
              for j in range(EXPERTS_PER_GROUP)]
    in_score = [_pick(gidx, [score_rows[g * EXPERTS_PER_GROUP + j] for g in range(N_EXPERT_GROUPS)])
                for j in range(EXPERTS_PER_GROUP)]
    _, loc0 = _first_max(in_sel)
    _, loc1 = _first_max([jnp.where(loc0 == j, -jnp.inf, in_sel[j]) for j in range(EXPERTS_PER_GROUP)])
    w0, w1 = _pick(loc0, in_score), _pick(loc1, in_score)
    den = w0 + w1
    eid_ref[0:1, :] = gidx * EXPERTS_PER_GROUP + loc0
    eid_ref[1:2, :] = gidx * EXPERTS_PER_GROUP + loc1
    wts_ref[0:1, :] = w0 / den
    wts_ref[1:2, :] = w1 / den


def _merge(l, x, att, ret, cm, z, mods, norm2, wa, wr, wm, wo, w_router_t, b_router):
    tm = ROW_TILE
    row = lambda i: (i, 0)
    lw = lambda k, n: pl.BlockSpec((None, k, n), lambda i: (l, 0, 0))
    in_specs = [pl.BlockSpec((tm, D_MODEL), row),
                pl.BlockSpec((tm, ATT_Q), row), pl.BlockSpec((tm, RET_W), row), pl.BlockSpec((tm, MLP_W), row),
                pl.BlockSpec((tm, D_MODEL), row), pl.BlockSpec((tm, D_MODEL), row), pl.BlockSpec((tm, D_MODEL), row),
                pl.BlockSpec((None, None, N_MOD, D_MODEL), lambda i: (l, _mod_row(i), 0, 0)),
                pl.BlockSpec((None, 1, D_MODEL), lambda i: (l, 0, 0)),
                lw(ATT_Q, D_MODEL), lw(RET_W, D_MODEL), lw(MLP_W, D_MODEL), lw(D_MODEL, D_MODEL),
                pl.BlockSpec((N_EXPERTS, D_MODEL), lambda i: (0, 0)),
                pl.BlockSpec((N_EXPERTS, 1), lambda i: (0, 0))]
    lane = lambda i: (0, i)
    return pl.pallas_call(
        _merge_kernel,
        grid=(T_ALL // tm,),
        in_specs=in_specs,
        out_specs=[pl.BlockSpec((tm, D_MODEL), row), pl.BlockSpec((tm, D_MODEL), row),
                   pl.BlockSpec((TOP_K, tm), lane), pl.BlockSpec((TOP_K, tm), lane)],
        out_shape=[jax.ShapeDtypeStruct((T_ALL, D_MODEL), F32), jax.ShapeDtypeStruct((T_ALL, D_MODEL), F32),
                   jax.ShapeDtypeStruct((TOP_K, T_ALL), jnp.int32), jax.ShapeDtypeStruct((TOP_K, T_ALL), F32)],
        compiler_params=_params(("arbitrary",), V7X_VMEM_LIMIT_BYTES),
        name="merge_router",
    )(x, att, ret, cm, z["ga"], z["gb"], z["gc"], mods, norm2.reshape(DEPTH, 1, D_MODEL),
      wa, wr, wm, wo, w_router_t, b_router.reshape(N_EXPERTS, 1))


def _dispatch_plan(eid):
    e_flat = eid.T.reshape(N_ASSIGN)
    onehot = (e_flat[:, None] == jnp.arange(N_EXPERTS, dtype=jnp.int32)[None, :]).astype(jnp.int32)
    cum = jnp.cumsum(onehot, axis=0)
    counts = cum[-1]
    rank = jnp.sum(cum * onehot, axis=1) - 1
    padded = (counts + MOE_BLOCK - 1) // MOE_BLOCK * MOE_BLOCK
    pend = jnp.cumsum(padded)
    pstart = pend - padded
    dest = pstart[e_flat] + rank
    a = jnp.arange(N_ASSIGN, dtype=jnp.int32)
    tok, slot_k = a // TOP_K, a % TOP_K
    src_tok = jnp.zeros((N_SLOTS,), jnp.int32).at[dest].set(tok)
    dst_row = jnp.zeros((N_SLOTS,), jnp.int32).at[dest].set(slot_k * T_ALL + tok)
    block_start = jnp.arange(N_MOE_BLOCKS, dtype=jnp.int32) * MOE_BLOCK
    block_e = jnp.minimum(jnp.searchsorted(pend, block_start, side="right"), N_EXPERTS - 1).astype(jnp.int32)
    n_valid = jnp.clip(pstart[block_e] + counts[block_e] - block_start, 0, MOE_BLOCK).astype(jnp.int32)
    return (src_tok.reshape(N_MOE_BLOCKS, 1, MOE_BLOCK), dst_row.reshape(N_MOE_BLOCKS, 1, MOE_BLOCK),
            block_e, n_valid)


def _moe_kernel(be_ref, nv_ref, src_ref, dst_ref, h_hbm, wg_ref, wu_ref, wd_ref, out_hbm,
                xbuf, ybuf, wg_s, wu_s, wd_s, sem_in, sem_out):
    i = pl.program_id(0)
    n_valid = nv_ref[i]

    def row_in(r):
        return pltpu.make_async_copy(h_hbm.at[pl.ds(src_ref[0, r], 1)], xbuf.at[pl.ds(r, 1)], sem_in)

    def row_out(r):
        return pltpu.make_async_copy(ybuf.at[pl.ds(r, 1)], out_hbm.at[pl.ds(dst_ref[0, r], 1)], sem_out)

    @pl.when(i == 0)
    def _init():
        xbuf[...] = jnp.zeros_like(xbuf)

    @pl.when(n_valid > 0)
    def _block():
        def start_in(r, c):
            row_in(r).start()
            return c
        lax.fori_loop(0, n_valid, start_in, 0)

        prev = be_ref[jnp.maximum(i - 1, 0)]

        @pl.when((i == 0) | (be_ref[i] != prev))
        def _cast():
            wg_s[...] = wg_ref[...].astype(BF16)
            wu_s[...] = wu_ref[...].astype(BF16)
            wd_s[...] = wd_ref[...].astype(BF16)

        def wait_in(r, c):
            row_in(r).wait()
            return c
        lax.fori_loop(0, n_valid, wait_in, 0)

        x = xbuf[...].astype(BF16)
        a = jnp.dot(x, wg_s[...], preferred_element_type=F32)
        b = jnp.dot(x, wu_s[...], preferred_element_type=F32)
        mid = ((a * _sigmoid(a)) * b).astype(BF16)
        ybuf[...] = jnp.dot(mid, wd_s[...], preferred_element_type=F32)

        def start_out(r, c):
            row_out(r).start()
            return c
        lax.fori_loop(0, n_valid, start_out, 0)

        def wait_out(r, c):
            row_out(r).wait()
            return c
        lax.fori_loop(0, n_valid, wait_out, 0)


def _moe(l, h2, plan, w_gate, w_up, w_down):
    src, dst, block_e, n_valid = plan
    idx_spec = pl.BlockSpec((None, 1, MOE_BLOCK), lambda i, be, nu: (i, 0, 0), memory_space=pltpu.SMEM)
    wspec = lambda k, n: pl.BlockSpec((None, None, k, n), lambda i, be, nu: (l, be[i], 0, 0))
    grid_spec = pltpu.PrefetchScalarGridSpec(
        num_scalar_prefetch=2,
        grid=(N_MOE_BLOCKS,),
        in_specs=[idx_spec, idx_spec,
                  pl.BlockSpec(memory_space=pl.ANY),
                  wspec(D_MODEL, EXPERT_FF), wspec(D_MODEL, EXPERT_FF), wspec(EXPERT_FF, D_MODEL)],
        out_specs=pl.BlockSpec(memory_space=pl.ANY),
        scratch_shapes=[pltpu.VMEM((MOE_BLOCK, D_MODEL), F32), pltpu.VMEM((MOE_BLOCK, D_MODEL), F32),
                        pltpu.VMEM((D_MODEL, EXPERT_FF), BF16), pltpu.VMEM((D_MODEL, EXPERT_FF), BF16),
                        pltpu.VMEM((EXPERT_FF, D_MODEL), BF16),
                        pltpu.SemaphoreType.DMA(()), pltpu.SemaphoreType.DMA(())])
    return pl.pallas_call(
        _moe_kernel,
        grid_spec=grid_spec,
        out_shape=jax.ShapeDtypeStruct((TOP_K * T_ALL, D_MODEL), F32),
        compiler_params=_params(("arbitrary",), V7X_VMEM_LIMIT_BYTES),
        name="moe_experts",
    )(block_e, n_valid, src, dst, h2, w_gate, w_up, w_down)


def _combine_kernel(x_ref, y0_ref, y1_ref, wc_ref, mod_ref, o_ref):
    wc = wc_ref[...]
    o_ref[...] = x_ref[...] + mod_ref[5:6, :] * (wc[:, 0:1] * y0_ref[...] + wc[:, 1:2] * y1_ref[...])


def _final_combine(l, x, moe_out, wcol, mods):
    tm = ROW_TILE
    nt = T_ALL // tm
    row = lambda i: (i, 0)
    return pl.pallas_call(
        _combine_kernel,
        grid=(nt,),
        in_specs=[pl.BlockSpec((tm, D_MODEL), row), pl.BlockSpec((tm, D_MODEL), row),
                  pl.BlockSpec((tm, D_MODEL), lambda i: (nt + i, 0)),
                  pl.BlockSpec((tm, TOP_K), row),
                  pl.BlockSpec((None, None, N_MOD, D_MODEL), lambda i: (l, _mod_row(i), 0, 0))],
        out_specs=pl.BlockSpec((tm, D_MODEL), row),
        out_shape=jax.ShapeDtypeStruct((T_ALL, D_MODEL), F32),
        compiler_params=_params(("arbitrary",)),
        name="final_combine",
    )(x, moe_out, moe_out, wcol, mods)


def kernel(x_prompt, x_sample, cache_k, cache_v, state_ret_fwd, state_ret_bwd, c, c_ctx, w_mod, b_mod, norm1, norm2, w_in, q_norm, k_norm, ret_decay_fwd, ret_decay_bwd, ret_gn, mlp_norm, w_spatial, b_spatial, w_att_o, w_ret_o, w_mlp_o, w_out, w_router, b_router, w_gate, w_up, w_down):
    cond = jnp.zeros((8, D_MODEL), F32).at[0].set(c_ctx).at[1:1 + DEC_BATCH].set(c)
    mods = _modulation(cond, w_mod, b_mod)

    w_in_bf = w_in.astype(BF16)
    wa, wr, wm, wo = (w.astype(BF16) for w in (w_att_o, w_ret_o, w_mlp_o, w_out))
    ws_bf = w_spatial.astype(BF16)
    w_router_t = w_router.T
    dec = jnp.broadcast_to(jnp.stack([ret_decay_fwd, ret_decay_bwd], axis=1)[..., None, None],
                           (DEPTH, 2, N_RET_HEADS, 1, RET_DK))
    rows = DEC_SEQ // GRID_W
    rope_q = _rope_tables(rows, HEAD_DIM, N_Q_HEADS)
    rope_k = _rope_tables(rows, HEAD_DIM, N_KV_HEADS)
    rope_r = _rope_tables(rows, RET_DK, 1)

    x = jnp.concatenate([x_prompt.reshape(T_CTX, D_MODEL), x_sample.reshape(T_LAT, D_MODEL)], axis=0)
    moe_out, wcol = None, None
    ks, vs, sfs, sbs = [], [], [], []
    for l in range(DEPTH):
        x, z = _input_projection(l, x, mods, norm1, w_in_bf, moe_out, wcol)

        att_c, k_l, v_l = _attention(l, z, q_norm, k_norm, latent=False)
        att_l, = _attention(l, z, q_norm, k_norm, latent=True, rope=rope_q + rope_k,
                            cache_k=cache_k, cache_v=cache_v)
        ret_c, sf_l, sb_l = _retention(l, z, dec, ret_gn, latent=False)
        ret_l, _, _ = _retention(l, z, dec, ret_gn, latent=True, rope=rope_r,
                                 s0f=state_ret_fwd, s0b=state_ret_bwd)
        cm = _chunk_mlp(l, z, mlp_norm, ws_bf, b_spatial)
        att = jnp.concatenate([att_c, att_l], axis=0)
        ret = jnp.concatenate([ret_c, ret_l], axis=0)

        x, h2, eid, wts = _merge(l, x, att, ret, cm, z, mods, norm2, wa, wr, wm, wo, w_router_t, b_router)
        moe_out = _moe(l, h2, _dispatch_plan(eid), w_gate, w_up, w_down)
        wcol = wts.T
        ks.append(k_l.reshape(BATCH, SEQ, ATT_KV))
        vs.append(v_l.reshape(BATCH, SEQ, ATT_KV))
        sfs.append(sf_l)
        sbs.append(sb_l)
    x = _final_combine(DEPTH - 1, x, moe_out, wcol, mods)

    y_prompt = x[:T_CTX].reshape(BATCH, SEQ, D_MODEL)
    y_sample = x[T_CTX:].reshape(DEC_BATCH, DEC_SEQ, D_MODEL)
    new_k = jnp.stack(ks, axis=1).reshape(BATCH, DEPTH, SEQ, N_KV_HEADS, HEAD_DIM)
    new_v = jnp.stack(vs, axis=1).reshape(BATCH, DEPTH, SEQ, N_KV_HEADS, HEAD_DIM)
    new_sf = jnp.stack(sfs, axis=1)
    new_sb = jnp.stack(sbs, axis=1)
    return (y_prompt, y_sample, new_k, new_v, new_sf, new_sb)
```

```python
import functools

import numpy as np
import jax
import jax.numpy as jnp
from jax import lax
from jax.experimental import pallas as pl
from jax.experimental.pallas import tpu as pltpu

F32 = jnp.float32
BF16 = jnp.bfloat16

D_MODEL = 1024
BATCH = 32
SEQ = 256
DEPTH = 4
DEC_BATCH = 2
DEC_SEQ = 1024
PAST_LEN = 256
GRID_W = 64
BLOCK = 128
ROPE_BASE = 10000.0
EPS = 1e-6
N_Q_HEADS = 8
N_KV_HEADS = 2
HEAD_DIM = 64
ATT_Q = N_Q_HEADS * HEAD_DIM
ATT_KV = N_KV_HEADS * HEAD_DIM
N_RET_HEADS = 4
RET_DK = 128
RET_DV = 128
RET_W = N_RET_HEADS * RET_DK
N_MLP_GROUPS = 4
MLP_W = 512
N_MOD = 6
N_EXPERTS = 16
N_EXPERT_GROUPS = 4
EXPERTS_PER_GROUP = N_EXPERTS // N_EXPERT_GROUPS
TOP_K = 2
EXPERT_FF = 512
MOE_BLOCK = 128

T_CTX = BATCH * SEQ
T_LAT = DEC_BATCH * DEC_SEQ
T_ALL = T_CTX + T_LAT
N_ASSIGN = T_ALL * TOP_K
N_MOE_BLOCKS = -(-N_ASSIGN // MOE_BLOCK) + N_EXPERTS
N_SLOTS = N_MOE_BLOCKS * MOE_BLOCK

ROW_TILE = 512
V7X_VMEM_LIMIT_BYTES = 56 * 1024 * 1024

Z_SPLITS = (("qa", ATT_Q), ("kv", 2 * ATT_KV), ("qr", RET_W), ("kr", RET_W), ("vr", RET_W), ("gr", RET_W),
            ("mu", MLP_W), ("mv", MLP_W), ("ga", D_MODEL), ("gb", D_MODEL), ("gc", D_MODEL))
DOT_CHUNK = 512


def _params(sem, vmem=None):
    return pltpu.CompilerParams(dimension_semantics=sem, vmem_limit_bytes=vmem)


def _rms(x, g):
    return x * lax.rsqrt(jnp.mean(x * x, axis=-1, keepdims=True) + EPS) * g


def _sigmoid(x):
    return jax.nn.sigmoid(x)


def _mod_row(i):
    first_lat = T_CTX // ROW_TILE
    per_batch = DEC_SEQ // ROW_TILE
    return jnp.where(i < first_lat, 0, 1 + (i - first_lat) // per_batch)


def _mod_kernel(cond_ref, w_ref, b_ref, o_ref):
    c = cond_ref[...]
    s = c * _sigmoid(c)
    o_ref[...] = jnp.dot(s, w_ref[...], preferred_element_type=F32,
                         precision=lax.Precision.HIGHEST) + b_ref[...]


def _modulation(cond, w_mod, b_mod):
    tn = D_MODEL
    out = pl.pallas_call(
        _mod_kernel,
        grid=(DEPTH, N_MOD),
        in_specs=[pl.BlockSpec((8, D_MODEL), lambda l, j: (0, 0)),
                  pl.BlockSpec((None, D_MODEL, tn), lambda l, j: (l, 0, j)),
                  pl.BlockSpec((None, 1, tn), lambda l, j: (l, 0, j))],
        out_specs=pl.BlockSpec((None, 8, tn), lambda l, j: (l, 0, j)),
        out_shape=jax.ShapeDtypeStruct((DEPTH, 8, N_MOD * D_MODEL), F32),
        compiler_params=_params(("arbitrary", "arbitrary")),
        name="modulation",
    )(cond, w_mod, b_mod.reshape(DEPTH, 1, N_MOD * D_MODEL))
    return out.reshape(DEPTH, 8, N_MOD, D_MODEL)


def _in_kernel(*refs, combine):
    it = iter(refs)
    x_ref = next(it)
    if combine:
        y0_ref, y1_ref, wc_ref, pmod_ref = next(it), next(it), next(it), next(it)
    mod_ref, g_ref, w_ref = next(it), next(it), next(it)
    if combine:
        xo_ref = next(it)
    outs = [next(it) for _ in Z_SPLITS]

    x = x_ref[...]
    if combine:
        wc = wc_ref[...]
        x = x + pmod_ref[5:6, :] * (wc[:, 0:1] * y0_ref[...] + wc[:, 1:2] * y1_ref[...])
        xo_ref[...] = x
    h = (_rms(x, g_ref[...]) * (1.0 + mod_ref[1:2, :]) + mod_ref[0:1, :]).astype(BF16)
    col = 0
    for (_, width), o_ref in zip(Z_SPLITS, outs):
        for c in range(0, width, DOT_CHUNK):
            cw = min(DOT_CHUNK, width - c)
            o_ref[:, c:c + cw] = jnp.dot(h, w_ref[:, col + c:col + c + cw],
                                         preferred_element_type=F32).astype(BF16)
        col += width


def _input_projection(l, x, mods, norm1, w_in_bf, moe_out=None, wcol=None):
    combine = moe_out is not None
    tm = ROW_TILE
    nt = T_ALL // tm
    row = lambda i: (i, 0)
    mod_spec = lambda ll: pl.BlockSpec((None, None, N_MOD, D_MODEL), lambda i: (ll, _mod_row(i), 0, 0))
    in_specs = [pl.BlockSpec((tm, D_MODEL), row)]
    args = [x]
    if combine:
        in_specs += [pl.BlockSpec((tm, D_MODEL), row),
                     pl.BlockSpec((tm, D_MODEL), lambda i: (nt + i, 0)),
                     pl.BlockSpec((tm, TOP_K), row),
                     mod_spec(l - 1)]
        args += [moe_out, moe_out, wcol, mods]
    in_specs += [mod_spec(l),
                 pl.BlockSpec((None, 1, D_MODEL), lambda i: (l, 0, 0)),
                 pl.BlockSpec((None, D_MODEL, w_in_bf.shape[-1]), lambda i: (l, 0, 0),
                              pipeline_mode=pl.Buffered(1))]
    args += [mods, norm1.reshape(DEPTH, 1, D_MODEL), w_in_bf]
    out_specs, out_shape = [], []
    if combine:
        out_specs.append(pl.BlockSpec((tm, D_MODEL), row))
        out_shape.append(jax.ShapeDtypeStruct((T_ALL, D_MODEL), F32))
    for _, width in Z_SPLITS:
        out_specs.append(pl.BlockSpec((tm, width), row))
        out_shape.append(jax.ShapeDtypeStruct((T_ALL, width), BF16))
    res = pl.pallas_call(
        functools.partial(_in_kernel, combine=combine),
        grid=(nt,),
        in_specs=in_specs, out_specs=out_specs, out_shape=out_shape,
        compiler_params=_params(("arbitrary",), V7X_VMEM_LIMIT_BYTES),
        name="input_projection",
    )(*args)
    if combine:
        x, res = res[0], res[1:]
    return x, {name: r for (name, _), r in zip(Z_SPLITS, res)}


def _rope_tables(rows, dim, reps):
    r = jnp.repeat(jnp.arange(rows, dtype=F32), GRID_W)
    col = jnp.tile(jnp.arange(GRID_W, dtype=F32), rows)
    half = dim // 2
    inv = ROPE_BASE ** (-jnp.arange(0, half, 2, dtype=F32) / half)
    ar = r[:, None] * inv
    ac = col[:, None] * inv
    ang = jnp.concatenate([ar, ar, ac, ac], axis=-1)
    cos, sin = jnp.cos(ang), jnp.sin(ang)
    first = (jnp.arange(dim) % (dim // 2)) < (dim // 4)
    sin_up = jnp.where(first, -sin, 0.0)
    sin_dn = jnp.where(first, 0.0, sin)
    t = lambda a: jnp.tile(a, (1, reps))
    return t(cos), t(sin_up), t(sin_dn)


def _rope(x, cos, sin_up, sin_dn, quarter):
    w = x.shape[-1]
    return x * cos + pltpu.roll(x, w - quarter, 1) * sin_up + pltpu.roll(x, quarter, 1) * sin_dn


def _attn_kernel(*refs, sk_new, use_rope, has_cache, emit_kv):
    it = iter(refs)
    q_ref, kv_ref, gq_ref, gk_ref, bdq_ref, bdk_ref = (next(it) for _ in range(6))
    if use_rope:
        cq_ref, suq_ref, sdq_ref, ck_ref, suk_ref, sdk_ref = (next(it) for _ in range(6))
    if has_cache:
        kc_ref, vc_ref = next(it), next(it)
    o_ref = next(it)
    if emit_kv:
        nk_ref, nv_ref = next(it), next(it)
    kd_scr, vd_scr = next(it), next(it)

    lo = lax.broadcasted_iota(jnp.int32, (1, 2 * HEAD_DIM), 1) < HEAD_DIM

    def dup_halves(a):
        r = pltpu.roll(a, HEAD_DIM, 1)
        return jnp.where(lo, a, r), jnp.where(lo, r, a)

    @pl.when(pl.program_id(1) == 0)
    def _prep():
        kv = kv_ref[...].astype(F32)
        k, v = kv[:, :ATT_KV], kv[:, ATT_KV:]
        ms = jnp.dot((k * k).astype(BF16), bdk_ref[...], preferred_element_type=F32)
        kn = k * lax.rsqrt(ms + EPS) * gk_ref[...]
        if emit_kv:
            nk_ref[...] = kn
            nv_ref[...] = v
        if use_rope:
            kn = _rope(kn, ck_ref[...], suk_ref[...], sdk_ref[...], HEAD_DIM // 4)
        k0, k1 = dup_halves(kn)
        v0, v1 = dup_halves(v)
        kd_scr[0, 0:sk_new, :] = k0.astype(BF16)
        kd_scr[1, 0:sk_new, :] = k1.astype(BF16)
        vd_scr[0, 0:sk_new, :] = v0.astype(BF16)
        vd_scr[1, 0:sk_new, :] = v1.astype(BF16)
        if has_cache:
            c0, c1 = dup_halves(kc_ref[...])
            d0, d1 = dup_halves(vc_ref[...])
            kd_scr[0, sk_new:, :] = c0.astype(BF16)
            kd_scr[1, sk_new:, :] = c1.astype(BF16)
            vd_scr[0, sk_new:, :] = d0.astype(BF16)
            vd_scr[1, sk_new:, :] = d1.astype(BF16)

    q = q_ref[...].astype(F32)
    ms = jnp.dot((q * q).astype(BF16), bdq_ref[...], preferred_element_type=F32)
    qn = q * lax.rsqrt(ms + EPS) * gq_ref[...]
    if use_rope:
        qn = _rope(qn, cq_ref[...], suq_ref[...], sdq_ref[...], HEAD_DIM // 4)
    qn = qn * (HEAD_DIM ** -0.5)
    heads_per_kv = N_Q_HEADS // N_KV_HEADS
    for j in range(N_Q_HEADS // 2):
        grp = (2 * j) // heads_per_kv
        kd, vd = kd_scr[grp], vd_scr[grp]
        qp = qn[:, 2 * HEAD_DIM * j:2 * HEAD_DIM * (j + 1)]
        halves = []
        for qm in (jnp.where(lo, qp, 0.0), jnp.where(lo, 0.0, qp)):
            s = lax.dot_general(qm.astype(BF16), kd, (((1,), (1,)), ((), ())), preferred_element_type=F32)
            e = jnp.exp(s - jnp.max(s, axis=-1, keepdims=True))
            den = jnp.sum(e, axis=-1, keepdims=True)
            halves.append(jnp.dot(e.astype(BF16), vd, preferred_element_type=F32) / den)
        o_ref[:, 2 * HEAD_DIM * j:2 * HEAD_DIM * (j + 1)] = jnp.where(lo, halves[0], halves[1]).astype(BF16)


def _block_diag_mean(width, group):
    idx = np.arange(width) // group
    return jnp.asarray((idx[:, None] == idx[None, :]).astype(np.float32) / group, dtype=BF16)


def _attention(l, z, q_norm, k_norm, *, latent, rope=None, cache_k=None, cache_v=None):
    if latent:
        nb, s, tq, row0 = DEC_BATCH, DEC_SEQ, 256, T_CTX
    else:
        nb, s, tq, row0 = BATCH, SEQ, SEQ, 0
    nq = s // tq
    sk = s + (PAST_LEN if latent else 0)
    qrow = lambda b, qi: (row0 // tq + b * nq + qi, 0)
    krow = lambda b, qi: (row0 // s + b, 0)
    const = lambda b, qi: (0, 0)
    lrow = lambda b, qi: (l, 0, 0)
    in_specs = [pl.BlockSpec((tq, ATT_Q), qrow),
                pl.BlockSpec((s, 2 * ATT_KV), krow),
                pl.BlockSpec((None, 1, ATT_Q), lrow),
                pl.BlockSpec((None, 1, ATT_KV), lrow),
                pl.BlockSpec((ATT_Q, ATT_Q), const),
                pl.BlockSpec((ATT_KV, ATT_KV), const)]
    args = [z["qa"], z["kv"],
            jnp.tile(q_norm, (1, N_Q_HEADS)).reshape(DEPTH, 1, ATT_Q),
            jnp.tile(k_norm, (1, N_KV_HEADS)).reshape(DEPTH, 1, ATT_KV),
            _block_diag_mean(ATT_Q, HEAD_DIM), _block_diag_mean(ATT_KV, HEAD_DIM)]
    if latent:
        cq, suq, sdq, ck, suk, sdk = rope
        in_specs += [pl.BlockSpec((tq, ATT_Q), lambda b, qi: (qi, 0))] * 3
        in_specs += [pl.BlockSpec((s, ATT_KV), const)] * 3
        in_specs += [pl.BlockSpec((None, None, PAST_LEN, ATT_KV), lambda b, qi: (b, l, 0, 0))] * 2
        args += [cq, suq, sdq, ck, suk, sdk,
                 cache_k.reshape(DEC_BATCH, DEPTH, PAST_LEN, ATT_KV),
                 cache_v.reshape(DEC_BATCH, DEPTH, PAST_LEN, ATT_KV)]
    rows = nb * s
    out_specs = [pl.BlockSpec((tq, ATT_Q), lambda b, qi: (b * nq + qi, 0))]
    out_shape = [jax.ShapeDtypeStruct((rows, ATT_Q), BF16)]
    if not latent:
        out_specs += [pl.BlockSpec((s, ATT_KV), lambda b, qi: (b, 0))] * 2
        out_shape += [jax.ShapeDtypeStruct((rows, ATT_KV), F32)] * 2
    return pl.pallas_call(
        functools.partial(_attn_kernel, sk_new=s, use_rope=latent, has_cache=latent, emit_kv=not latent),
        grid=(nb, nq),
        in_specs=in_specs, out_specs=out_specs, out_shape=out_shape,
        scratch_shapes=[pltpu.VMEM((N_KV_HEADS, sk, 2 * HEAD_DIM), BF16),
                        pltpu.VMEM((N_KV_HEADS, sk, 2 * HEAD_DIM), BF16)],
        compiler_params=_params(("arbitrary", "arbitrary"), V7X_VMEM_LIMIT_BYTES),
        name="attention_latent" if latent else "attention_context",
    )(*args)


def _ret_kernel(*refs, s, heads, use_rope, has_state):
    it = iter(refs)
    q_ref, k_ref, v_ref, g_ref, dec_ref, gn_ref = (next(it) for _ in range(6))
    rope = (next(it)[...], next(it)[...], next(it)[...]) if use_rope else None
    s0f_ref, s0b_ref = (next(it), next(it)) if has_state else (None, None)
    o_ref, sf_ref, sb_ref, oacc = next(it), next(it), next(it), next(it)
    for hh in range(heads):
        cols = slice(hh * RET_DK, (hh + 1) * RET_DK)
        _ret_head(q_ref.at[:, cols], k_ref.at[:, cols], v_ref.at[:, cols], g_ref.at[:, cols],
                  dec_ref.at[:, hh], gn_ref.at[hh], rope,
                  s0f_ref.at[hh] if has_state else None, s0b_ref.at[hh] if has_state else None,
                  o_ref.at[:, cols], sf_ref.at[hh], sb_ref.at[hh], oacc.at[hh], s)


def _ret_head(q_ref, k_ref, v_ref, g_ref, dec_ref, gn_ref, rope, s0f_ref, s0b_ref, o_ref, sf_ref, sb_ref, oacc, s):
    use_rope, has_state = rope is not None, s0f_ref is not None
    n_chunks = s // BLOCK
    lgf = -jnp.exp(dec_ref[0])
    lgb = -jnp.exp(dec_ref[1])
    lgf1, lgb1 = lgf[:, 0:1], lgb[:, 0:1]
    diff = (lax.broadcasted_iota(jnp.int32, (BLOCK, BLOCK), 0)
            - lax.broadcasted_iota(jnp.int32, (BLOCK, BLOCK), 1)).astype(F32)
    dsum = (jnp.where(diff >= 0, jnp.exp(diff * lgf), 0.0)
            + jnp.where(diff <= 0, jnp.exp(-diff * lgb), 0.0))
    ic = lax.broadcasted_iota(jnp.int32, (BLOCK, 1), 0).astype(F32)
    qdf, kdf, cdf = jnp.exp((ic + 1.0) * lgf1), jnp.exp((BLOCK - 1.0 - ic) * lgf1), jnp.exp(BLOCK * lgf1)
    qdb, kdb, cdb = jnp.exp((BLOCK - ic) * lgb1), jnp.exp(ic * lgb1), jnp.exp(BLOCK * lgb1)

    q = q_ref[...].astype(F32)
    k = k_ref[...].astype(F32) * (RET_DK ** -0.5)
    if use_rope:
        q = _rope(q, *rope, RET_DK // 4)
        k = _rope(k, *rope, RET_DK // 4)
    v = v_ref[...]

    def rows(a, n):
        return a[n * BLOCK:(n + 1) * BLOCK]

    def state_step(state, kn, kdec, vn, cdec):
        kd_t = jnp.transpose(kn * kdec).astype(BF16)
        return cdec * state + jnp.dot(kd_t, vn, preferred_element_type=F32)

    state = s0f_ref[...] if has_state else jnp.zeros((RET_DK, RET_DV), F32)
    for n in range(n_chunks):
        qn, kn, vn = rows(q, n), rows(k, n), rows(v, n)
        inner = lax.dot_general(qn.astype(BF16), kn.astype(BF16), (((1,), (1,)), ((), ())),
                                preferred_element_type=F32) * dsum
        o = (jnp.dot(inner.astype(BF16), vn, preferred_element_type=F32)
             + jnp.dot((qn * qdf).astype(BF16), state.astype(BF16), preferred_element_type=F32))
        state = state_step(state, kn, kdf, vn, cdf)
        oacc[n * BLOCK:(n + 1) * BLOCK, :] = o
    sf_ref[...] = state

    state = s0b_ref[...] if has_state else jnp.zeros((RET_DK, RET_DV), F32)
    for n in reversed(range(n_chunks)):
        qn, kn, vn = rows(q, n), rows(k, n), rows(v, n)
        o = rows(oacc, n) + jnp.dot((qn * qdb).astype(BF16), state.astype(BF16), preferred_element_type=F32)
        state = state_step(state, kn, kdb, vn, cdb)
        mu = jnp.mean(o, axis=-1, keepdims=True)
        var = jnp.mean(jnp.square(o - mu), axis=-1, keepdims=True)
        y = (o - mu) * lax.rsqrt(var + EPS) * gn_ref[...]
        gate = g_ref[n * BLOCK:(n + 1) * BLOCK, :].astype(F32)
        o_ref[n * BLOCK:(n + 1) * BLOCK, :] = (y * (gate * _sigmoid(gate))).astype(BF16)
    sb_ref[...] = state


def _retention(l, z, dec, ret_gn, *, latent, rope=None, s0f=None, s0b=None):
    if latent:
        nb, s, row0, heads = DEC_BATCH, DEC_SEQ, T_CTX, 1
    else:
        nb, s, row0, heads = BATCH, SEQ, 0, N_RET_HEADS
    width = heads * RET_DK
    hrow = lambda b, h: (row0 // s + b, h)
    in_specs = [pl.BlockSpec((s, width), hrow)] * 4
    in_specs += [pl.BlockSpec((None, 2, heads, 1, RET_DK), lambda b, h: (l, 0, h, 0, 0)),
                 pl.BlockSpec((None, heads, 1, RET_DV), lambda b, h: (l, h, 0, 0))]
    args = [z["qr"], z["kr"], z["vr"], z["gr"], dec, ret_gn.reshape(DEPTH, N_RET_HEADS, 1, RET_DV)]
    if latent:
        in_specs += [pl.BlockSpec((s, RET_DK), lambda b, h: (0, 0))] * 3
        in_specs += [pl.BlockSpec((None, None, heads, RET_DK, RET_DV), lambda b, h: (b, l, h, 0, 0))] * 2
        args += [*rope, s0f, s0b]
    st_spec = pl.BlockSpec((None, heads, RET_DK, RET_DV), lambda b, h: (b, h, 0, 0))
    st_shape = jax.ShapeDtypeStruct((nb, N_RET_HEADS, RET_DK, RET_DV), F32)
    return pl.pallas_call(
        functools.partial(_ret_kernel, s=s, heads=heads, use_rope=latent, has_state=latent),
        grid=(nb, N_RET_HEADS // heads),
        in_specs=in_specs,
        out_specs=[pl.BlockSpec((s, width), lambda b, h: (b, h)), st_spec, st_spec],
        out_shape=[jax.ShapeDtypeStruct((nb * s, RET_W), BF16), st_shape, st_shape],
        scratch_shapes=[pltpu.VMEM((heads, s, RET_DV), F32)],
        compiler_params=_params(("arbitrary", "arbitrary")),
        name="retention_latent" if latent else "retention_context",
    )(*args)


def _cmlp_kernel(mu_ref, mv_ref, gn_ref, ws_ref, bs_ref, o_ref):
    u = jax.nn.gelu(mu_ref[...].astype(F32), approximate=True)
    v = jax.nn.gelu(mv_ref[...].astype(F32), approximate=True)
    vn = _rms(v, gn_ref[...]).astype(BF16)
    gc = MLP_W // N_MLP_GROUPS
    for n in range(ROW_TILE // BLOCK):
        r = slice(n * BLOCK, (n + 1) * BLOCK)
        for g in range(N_MLP_GROUPS):
            c = slice(g * gc, (g + 1) * gc)
            f = jnp.dot(ws_ref[g], vn[r, c], preferred_element_type=F32) + bs_ref[g]
            o_ref[r, c] = (u[r, c] * f).astype(BF16)


def _chunk_mlp(l, z, mlp_norm, w_spatial_bf, b_spatial):
    tm = ROW_TILE
    row = lambda i: (i, 0)
    return pl.pallas_call(
        _cmlp_kernel,
        grid=(T_ALL // tm,),
        in_specs=[pl.BlockSpec((tm, MLP_W), row), pl.BlockSpec((tm, MLP_W), row),
                  pl.BlockSpec((None, 1, MLP_W), lambda i: (l, 0, 0)),
                  pl.BlockSpec((None, N_MLP_GROUPS, BLOCK, BLOCK), lambda i: (l, 0, 0, 0)),
                  pl.BlockSpec((None, N_MLP_GROUPS, BLOCK, 1), lambda i: (l, 0, 0, 0))],
        out_specs=pl.BlockSpec((tm, MLP_W), row),
        out_shape=jax.ShapeDtypeStruct((T_ALL, MLP_W), BF16),
        compiler_params=_params(("arbitrary",)),
        name="chunk_mlp",
    )(z["mu"], z["mv"], mlp_norm.reshape(DEPTH, 1, MLP_W), w_spatial_bf,
      b_spatial.reshape(DEPTH, N_MLP_GROUPS, BLOCK, 1))


def _first_max(vals):
    best, idx = vals[0], jnp.zeros(vals[0].shape, jnp.int32)
    for j in range(1, len(vals)):
        upd = vals[j] > best
        best = jnp.where(upd, vals[j], best)
        idx = jnp.where(upd, j, idx)
    return best, idx


def _pick(idx, vals):
    out = vals[-1]
    for j in range(len(vals) - 2, -1, -1):
        out = jnp.where(idx == j, vals[j], out)
    return out


def _merge_kernel(x_ref, att_ref, ret_ref, cm_ref, ga_ref, gb_ref, gc_ref, mod_ref, g2_ref,
                  wa_ref, wr_ref, wm_ref, wo_ref, wrt_ref, br_ref, tri_ref,
                  x1_ref, h2_ref, eid_ref, wts_ref, rank_ref, cnt_ref, cnt_scr):
    @pl.when(pl.program_id(0) == 0)
    def _zero_counts():
        cnt_scr[...] = jnp.zeros_like(cnt_scr)

    sig = lambda r: _sigmoid(r[...].astype(F32))
    dot = lambda a, b: jnp.dot(a, b, preferred_element_type=F32)
    mix = (sig(ga_ref) * dot(att_ref[...], wa_ref[...])
           + sig(gb_ref) * dot(ret_ref[...], wr_ref[...])
           + sig(gc_ref) * dot(cm_ref[...], wm_ref[...]))
    x1 = x_ref[...] + mod_ref[2:3, :] * dot(mix.astype(BF16), wo_ref[...])
    x1_ref[...] = x1
    h2 = _rms(x1, g2_ref[...]) * (1.0 + mod_ref[4:5, :]) + mod_ref[3:4, :]
    h2_ref[...] = h2

    logits = lax.dot_general(wrt_ref[...], h2, (((1,), (1,)), ((), ())), preferred_element_type=F32,
                             precision=lax.Precision.HIGHEST)
    score = _sigmoid(logits)
    sel = score + br_ref[...]
    sel_rows = [sel[e:e + 1, :] for e in range(N_EXPERTS)]
    score_rows = [score[e:e + 1, :] for e in range(N_EXPERTS)]
    group_scores = []
    for g in range(N_EXPERT_GROUPS):
        v = sel_rows[g * EXPERTS_PER_GROUP:(g + 1) * EXPERTS_PER_GROUP]
        pair_sums = [v[a] + v[b] for a in range(EXPERTS_PER_GROUP) for b in range(a + 1, EXPERTS_PER_GROUP)]
        group_scores.append(functools.reduce(jnp.maximum, pair_sums))
    _, gidx = _first_max(group_scores)
    in_sel = [_pick(gidx, [sel_rows[g * EXPERTS_PER_GROUP + j] for g in range(N_EXPERT_GROUPS)])
              for j in range(EXPERTS_PER_GROUP)]
    in_score = [_pick(gidx, [score_rows[g * EXPERTS_PER_GROUP + j] for g in range(N_EXPERT_GROUPS)])
                for j in range(EXPERTS_PER_GROUP)]
    _, loc0 = _first_max(in_sel)
    _, loc1 = _first_max([jnp.where(loc0 == j, -jnp.inf, in_sel[j]) for j in range(EXPERTS_PER_GROUP)])
    w0, w1 = _pick(loc0, in_score), _pick(loc1, in_score)
    den = w0 + w1
    e0 = gidx * EXPERTS_PER_GROUP + loc0
    e1 = gidx * EXPERTS_PER_GROUP + loc1
    eid_ref[0:1, :] = e0
    eid_ref[1:2, :] = e1
    wts_ref[0:1, :] = w0 / den
    wts_ref[1:2, :] = w1 / den

    eio = lax.broadcasted_iota(jnp.int32, score.shape, 0)
    oh0 = (eio == e0).astype(F32)
    oh1 = (eio == e1).astype(F32)
    both = oh0 + oh1
    before = jnp.dot(both.astype(BF16), tri_ref[...], preferred_element_type=F32) + cnt_scr[...]
    rank_ref[0:1, :] = jnp.sum(oh0 * before, axis=0, keepdims=True).astype(jnp.int32)
    rank_ref[1:2, :] = jnp.sum(oh1 * before, axis=0, keepdims=True).astype(jnp.int32)
    cnt_scr[...] = cnt_scr[...] + jnp.sum(both, axis=1, keepdims=True)
    cnt_ref[...] = cnt_scr[...]


def _merge(l, x, att, ret, cm, z, mods, norm2, wa, wr, wm, wo, w_router_t, b_router):
    tm = ROW_TILE
    row = lambda i: (i, 0)
    lw = lambda k, n: pl.BlockSpec((None, k, n), lambda i: (l, 0, 0))
    in_specs = [pl.BlockSpec((tm, D_MODEL), row),
                pl.BlockSpec((tm, ATT_Q), row), pl.BlockSpec((tm, RET_W), row), pl.BlockSpec((tm, MLP_W), row),
                pl.BlockSpec((tm, D_MODEL), row), pl.BlockSpec((tm, D_MODEL), row), pl.BlockSpec((tm, D_MODEL), row),
                pl.BlockSpec((None, None, N_MOD, D_MODEL), lambda i: (l, _mod_row(i), 0, 0)),
                pl.BlockSpec((None, 1, D_MODEL), lambda i: (l, 0, 0)),
                lw(ATT_Q, D_MODEL), lw(RET_W, D_MODEL), lw(MLP_W, D_MODEL), lw(D_MODEL, D_MODEL),
                pl.BlockSpec((N_EXPERTS, D_MODEL), lambda i: (0, 0)),
                pl.BlockSpec((N_EXPERTS, 1), lambda i: (0, 0)),
                pl.BlockSpec((tm, tm), lambda i: (0, 0))]
    lane = lambda i: (0, i)
    tri = jnp.asarray(np.triu(np.ones((tm, tm), np.float32), k=1), dtype=BF16)
    return pl.pallas_call(
        _merge_kernel,
        grid=(T_ALL // tm,),
        in_specs=in_specs,
        out_specs=[pl.BlockSpec((tm, D_MODEL), row), pl.BlockSpec((tm, D_MODEL), row),
                   pl.BlockSpec((TOP_K, tm), lane), pl.BlockSpec((TOP_K, tm), lane),
                   pl.BlockSpec((TOP_K, tm), lane), pl.BlockSpec((N_EXPERTS, 1), lambda i: (0, 0))],
        out_shape=[jax.ShapeDtypeStruct((T_ALL, D_MODEL), F32), jax.ShapeDtypeStruct((T_ALL, D_MODEL), F32),
                   jax.ShapeDtypeStruct((TOP_K, T_ALL), jnp.int32), jax.ShapeDtypeStruct((TOP_K, T_ALL), F32),
                   jax.ShapeDtypeStruct((TOP_K, T_ALL), jnp.int32), jax.ShapeDtypeStruct((N_EXPERTS, 1), F32)],
        scratch_shapes=[pltpu.VMEM((N_EXPERTS, 1), F32)],
        compiler_params=_params(("arbitrary",), V7X_VMEM_LIMIT_BYTES),
        name="merge_router",
    )(x, att, ret, cm, z["ga"], z["gb"], z["gc"], mods, norm2.reshape(DEPTH, 1, D_MODEL),
      wa, wr, wm, wo, w_router_t, b_router.reshape(N_EXPERTS, 1), tri)


def _dispatch_plan(eid, rank, counts):
    expert = jnp.arange(N_EXPERTS, dtype=jnp.int32)
    padded = (counts + MOE_BLOCK - 1) // MOE_BLOCK * MOE_BLOCK
    pend = jnp.cumsum(padded)
    pstart = pend - padded
    dest = rank + jnp.sum(jnp.where(eid[..., None] == expert, pstart, 0), axis=-1)
    block_start = jnp.arange(N_MOE_BLOCKS, dtype=jnp.int32) * MOE_BLOCK
    block_e = jnp.minimum(jnp.sum((pend[None, :] <= block_start[:, None]).astype(jnp.int32), axis=1),
                          N_EXPERTS - 1)
    seg_end = jnp.sum(jnp.where(block_e[:, None] == expert, pstart + counts, 0), axis=-1)
    n_valid = jnp.clip(seg_end - block_start, 0, MOE_BLOCK).astype(jnp.int32)
    return block_e, n_valid, dest.reshape(N_ASSIGN)


ROWS_PER_TRIP = 8


def _for_rows(n, fn):
    full = lax.div(n, ROWS_PER_TRIP)

    def group(g, c):
        for u in range(ROWS_PER_TRIP):
            fn(g * ROWS_PER_TRIP + u)
        return c
    lax.fori_loop(0, full, group, 0)

    def single(r, c):
        fn(r)
        return c
    lax.fori_loop(full * ROWS_PER_TRIP, n, single, 0)


def _moe_kernel(be_ref, nv_ref, dest_ref, h_hbm, wg_ref, wu_ref, wd_ref, out_hbm,
                xbuf, ybuf, wg_s, wu_s, wd_s, slot_s, sem_in, sem_out):
    i = pl.program_id(0)
    last = pl.num_programs(0) - 1

    def token_of(a):
        return jnp.where(a >= T_ALL, a - T_ALL, a)

    def start_gather(blk):
        buf = lax.rem(blk, 2)

        def one(r):
            a = slot_s[blk * MOE_BLOCK + r]
            pltpu.make_async_copy(h_hbm.at[pl.ds(token_of(a), 1)], xbuf.at[buf, pl.ds(r, 1)],
                                  sem_in.at[buf]).start()
        _for_rows(nv_ref[blk], one)

    def start_scatter(blk):
        buf = lax.rem(blk, 2)

        def one(r):
            a = slot_s[blk * MOE_BLOCK + r]
            pltpu.make_async_copy(ybuf.at[buf, pl.ds(r, 1)], out_hbm.at[pl.ds(a, 1)], sem_out.at[buf]).start()
        _for_rows(nv_ref[blk], one)

    def wait_rows(blk, gather):
        buf, n = lax.rem(blk, 2), nv_ref[blk]
        n8 = pl.multiple_of(lax.div(n, ROWS_PER_TRIP) * ROWS_PER_TRIP, ROWS_PER_TRIP)

        def copy(rows):
            if gather:
                return pltpu.make_async_copy(h_hbm.at[rows], xbuf.at[buf, rows], sem_in.at[buf])
            return pltpu.make_async_copy(ybuf.at[buf, rows], out_hbm.at[rows], sem_out.at[buf])

        @pl.when(n8 > 0)
        def _():
            copy(pl.ds(0, n8)).wait()

        def single(r, c):
            copy(pl.ds(0, 1)).wait()
            return c
        lax.fori_loop(n8, n, single, 0)

    @pl.when(i == 0)
    def _init():
        def put(g, c):
            for u in range(ROWS_PER_TRIP):
                a = g * ROWS_PER_TRIP + u
                slot_s[dest_ref[a]] = a
            return c
        lax.fori_loop(0, N_ASSIGN // ROWS_PER_TRIP, put, 0)
        xbuf[...] = jnp.zeros_like(xbuf)
        start_gather(0)

    @pl.when(i < last)
    def _prefetch():
        start_gather(i + 1)

    @pl.when(i >= 2)
    def _drain():
        wait_rows(i - 2, gather=False)

    @pl.when(nv_ref[i] > 0)
    def _block():
        buf = lax.rem(i, 2)
        prev = be_ref[jnp.maximum(i - 1, 0)]

        @pl.when((i == 0) | (be_ref[i] != prev))
        def _cast():
            wg_s[...] = wg_ref[...].astype(BF16)
            wu_s[...] = wu_ref[...].astype(BF16)
            wd_s[...] = wd_ref[...].astype(BF16)

        wait_rows(i, gather=True)
        x = xbuf[buf].astype(BF16)
        a = jnp.dot(x, wg_s[...], preferred_element_type=F32)
        b = jnp.dot(x, wu_s[...], preferred_element_type=F32)
        mid = ((a * _sigmoid(a)) * b).astype(BF16)
        ybuf[buf] = jnp.dot(mid, wd_s[...], preferred_element_type=F32)
        start_scatter(i)

    @pl.when(i == last)
    def _finish():
        wait_rows(last - 1, gather=False)
        wait_rows(last, gather=False)


def _moe(l, h2, plan, w_gate, w_up, w_down):
    block_e, n_valid, dest = plan
    wspec = lambda k, n: pl.BlockSpec((None, None, k, n), lambda i, be, nv, de: (l, be[i], 0, 0))
    grid_spec = pltpu.PrefetchScalarGridSpec(
        num_scalar_prefetch=3,
        grid=(N_MOE_BLOCKS,),
        in_specs=[pl.BlockSpec(memory_space=pl.ANY),
                  wspec(D_MODEL, EXPERT_FF), wspec(D_MODEL, EXPERT_FF), wspec(EXPERT_FF, D_MODEL)],
        out_specs=pl.BlockSpec(memory_space=pl.ANY),
        scratch_shapes=[pltpu.VMEM((2, MOE_BLOCK, D_MODEL), F32), pltpu.VMEM((2, MOE_BLOCK, D_MODEL), F32),
                        pltpu.VMEM((D_MODEL, EXPERT_FF), BF16), pltpu.VMEM((D_MODEL, EXPERT_FF), BF16),
                        pltpu.VMEM((EXPERT_FF, D_MODEL), BF16),
                        pltpu.SMEM((N_SLOTS,), jnp.int32),
                        pltpu.SemaphoreType.DMA((2,)), pltpu.SemaphoreType.DMA((2,))])
    return pl.pallas_call(
        _moe_kernel,
        grid_spec=grid_spec,
        out_shape=jax.ShapeDtypeStruct((TOP_K * T_ALL, D_MODEL), F32),
        compiler_params=_params(("arbitrary",), V7X_VMEM_LIMIT_BYTES),
        name="moe_experts",
    )(block_e, n_valid, dest, h2, w_gate, w_up, w_down)


def _combine_kernel(x_ref, y0_ref, y1_ref, wc_ref, mod_ref, o_ref):
    wc = wc_ref[...]
    o_ref[...] = x_ref[...] + mod_ref[5:6, :] * (wc[:, 0:1] * y0_ref[...] + wc[:, 1:2] * y1_ref[...])


def _final_combine(l, x, moe_out, wcol, mods):
    tm = ROW_TILE
    nt = T_ALL // tm
    row = lambda i: (i, 0)
    return pl.pallas_call(
        _combine_kernel,
        grid=(nt,),
        in_specs=[pl.BlockSpec((tm, D_MODEL), row), pl.BlockSpec((tm, D_MODEL), row),
                  pl.BlockSpec((tm, D_MODEL), lambda i: (nt + i, 0)),
                  pl.BlockSpec((tm, TOP_K), row),
                  pl.BlockSpec((None, None, N_MOD, D_MODEL), lambda i: (l, _mod_row(i), 0, 0))],
        out_specs=pl.BlockSpec((tm, D_MODEL), row),
        out_shape=jax.ShapeDtypeStruct((T_ALL, D_MODEL), F32),
        compiler_params=_params(("arbitrary",)),
        name="final_combine",
    )(x, moe_out, moe_out, wcol, mods)


def kernel(x_prompt, x_sample, cache_k, cache_v, state_ret_fwd, state_ret_bwd, c, c_ctx, w_mod, b_mod, norm1, norm2, w_in, q_norm, k_norm, ret_decay_fwd, ret_decay_bwd, ret_gn, mlp_norm, w_spatial, b_spatial, w_att_o, w_ret_o, w_mlp_o, w_out, w_router, b_router, w_gate, w_up, w_down):
    cond = jnp.zeros((8, D_MODEL), F32).at[0].set(c_ctx).at[1:1 + DEC_BATCH].set(c)
    mods = _modulation(cond, w_mod, b_mod)

    w_in_bf = w_in.astype(BF16)
    wa, wr, wm, wo = (w.astype(BF16) for w in (w_att_o, w_ret_o, w_mlp_o, w_out))
    ws_bf = w_spatial.astype(BF16)
    w_router_t = w_router.T
    dec = jnp.broadcast_to(jnp.stack([ret_decay_fwd, ret_decay_bwd], axis=1)[..., None, None],
                           (DEPTH, 2, N_RET_HEADS, 1, RET_DK))
    rows = DEC_SEQ // GRID_W
    rope_q = _rope_tables(rows, HEAD_DIM, N_Q_HEADS)
    rope_k = _rope_tables(rows, HEAD_DIM, N_KV_HEADS)
    rope_r = _rope_tables(rows, RET_DK, 1)

    x = jnp.concatenate([x_prompt.reshape(T_CTX, D_MODEL), x_sample.reshape(T_LAT, D_MODEL)], axis=0)
    moe_out, wcol = None, None
    ks, vs, sfs, sbs = [], [], [], []
    for l in range(DEPTH):
        x, z = _input_projection(l, x, mods, norm1, w_in_bf, moe_out, wcol)

        att_c, k_l, v_l = _attention(l, z, q_norm, k_norm, latent=False)
        att_l, = _attention(l, z, q_norm, k_norm, latent=True, rope=rope_q + rope_k,
                            cache_k=cache_k, cache_v=cache_v)
        ret_c, sf_l, sb_l = _retention(l, z, dec, ret_gn, latent=False)
        ret_l, _, _ = _retention(l, z, dec, ret_gn, latent=True, rope=rope_r,
                                 s0f=state_ret_fwd, s0b=state_ret_bwd)
        cm = _chunk_mlp(l, z, mlp_norm, ws_bf, b_spatial)
        att = jnp.concatenate([att_c, att_l], axis=0)
        ret = jnp.concatenate([ret_c, ret_l], axis=0)

        x, h2, eid, wts, rank, cnt = _merge(l, x, att, ret, cm, z, mods, norm2, wa, wr, wm, wo,
                                            w_router_t, b_router)
        plan = _dispatch_plan(eid, rank, cnt[:, 0].astype(jnp.int32))
        moe_out = _moe(l, h2, plan, w_gate, w_up, w_down)
        wcol = wts.T
        ks.append(k_l.reshape(BATCH, SEQ, ATT_KV))
        vs.append(v_l.reshape(BATCH, SEQ, ATT_KV))
        sfs.append(sf_l)
        sbs.append(sb_l)
    x = _final_combine(DEPTH - 1, x, moe_out, wcol, mods)

    y_prompt = x[:T_CTX].reshape(BATCH, SEQ, D_MODEL)
    y_sample = x[T_CTX:].reshape(DEC_BATCH, DEC_SEQ, D_MODEL)
    new_k = jnp.stack(ks, axis=1).reshape(BATCH, DEPTH, SEQ, N_KV_HEADS, HEAD_DIM)
    new_v = jnp.stack(vs, axis=1).reshape(BATCH, DEPTH, SEQ, N_KV_HEADS, HEAD_DIM)
    new_sf = jnp.stack(sfs, axis=1)
    new_sb = jnp.stack(sbs, axis=1)
    return (y_prompt, y_sample, new_k, new_v, new_sf, new_sb)
```

```python
import functools

import numpy as np
import jax
import jax.numpy as jnp
from jax import lax
from jax.experimental import pallas as pl
from jax.experimental.pallas import tpu as pltpu

F32 = jnp.float32
BF16 = jnp.bfloat16

D_MODEL = 1024
BATCH = 32
SEQ = 256
DEPTH = 4
DEC_BATCH = 2
DEC_SEQ = 1024
PAST_LEN = 256
GRID_W = 64
BLOCK = 128
ROPE_BASE = 10000.0
EPS = 1e-6
N_Q_HEADS = 8
N_KV_HEADS = 2
HEAD_DIM = 64
ATT_Q = N_Q_HEADS * HEAD_DIM
ATT_KV = N_KV_HEADS * HEAD_DIM
N_RET_HEADS = 4
RET_DK = 128
RET_DV = 128
RET_W = N_RET_HEADS * RET_DK
N_MLP_GROUPS = 4
MLP_W = 512
N_MOD = 6
N_EXPERTS = 16
N_EXPERT_GROUPS = 4
EXPERTS_PER_GROUP = N_EXPERTS // N_EXPERT_GROUPS
TOP_K = 2
EXPERT_FF = 512
MOE_BLOCK = 128

T_CTX = BATCH * SEQ
T_LAT = DEC_BATCH * DEC_SEQ
T_ALL = T_CTX + T_LAT
N_ASSIGN = T_ALL * TOP_K
N_MOE_BLOCKS = -(-N_ASSIGN // MOE_BLOCK) + N_EXPERTS
N_SLOTS = N_MOE_BLOCKS * MOE_BLOCK

ROW_TILE = 512
V7X_VMEM_LIMIT_BYTES = 56 * 1024 * 1024

Z_SPLITS = (("qa", ATT_Q), ("kv", 2 * ATT_KV), ("qr", RET_W), ("kr", RET_W), ("vr", RET_W), ("gr", RET_W),
            ("mu", MLP_W), ("mv", MLP_W), ("ga", D_MODEL), ("gb", D_MODEL), ("gc", D_MODEL))
DOT_CHUNK = 512


def _params(sem, vmem=None):
    return pltpu.CompilerParams(dimension_semantics=sem, vmem_limit_bytes=vmem)


def _rms(x, g):
    return x * lax.rsqrt(jnp.mean(x * x, axis=-1, keepdims=True) + EPS) * g


def _sigmoid(x):
    return jax.nn.sigmoid(x)


TILE_ROWS = D_MODEL // 128


def _store_row_tiles(ref, val):
    rows = val.shape[0]
    for c in range(TILE_ROWS):
        ref[pl.ds(c, rows, stride=TILE_ROWS), :] = val[:, c * 128:(c + 1) * 128]


def _load_row_tiles(ref):
    rows = ref.shape[0] // TILE_ROWS
    return jnp.concatenate([ref[pl.ds(c, rows, stride=TILE_ROWS), :] for c in range(TILE_ROWS)], axis=1)


def _mod_row(i):
    first_lat = T_CTX // ROW_TILE
    per_batch = DEC_SEQ // ROW_TILE
    return jnp.where(i < first_lat, 0, 1 + (i - first_lat) // per_batch)


def _mod_kernel(cond_ref, w_ref, b_ref, o_ref):
    c = cond_ref[...]
    s = c * _sigmoid(c)
    o_ref[...] = jnp.dot(s, w_ref[...], preferred_element_type=F32,
                         precision=lax.Precision.HIGHEST) + b_ref[...]


def _modulation(cond, w_mod, b_mod):
    tn = D_MODEL
    out = pl.pallas_call(
        _mod_kernel,
        grid=(DEPTH, N_MOD),
        in_specs=[pl.BlockSpec((8, D_MODEL), lambda l, j: (0, 0)),
                  pl.BlockSpec((None, D_MODEL, tn), lambda l, j: (l, 0, j)),
                  pl.BlockSpec((None, 1, tn), lambda l, j: (l, 0, j))],
        out_specs=pl.BlockSpec((None, 8, tn), lambda l, j: (l, 0, j)),
        out_shape=jax.ShapeDtypeStruct((DEPTH, 8, N_MOD * D_MODEL), F32),
        compiler_params=_params(("arbitrary", "arbitrary")),
        name="modulation",
    )(cond, w_mod, b_mod.reshape(DEPTH, 1, N_MOD * D_MODEL))
    return out.reshape(DEPTH, 8, N_MOD, D_MODEL)


def _in_kernel(*refs, combine):
    it = iter(refs)
    x_ref = next(it)
    if combine:
        y0_ref, y1_ref, wc_ref, pmod_ref = next(it), next(it), next(it), next(it)
    mod_ref, g_ref, w_ref = next(it), next(it), next(it)
    if combine:
        xo_ref = next(it)
    outs = [next(it) for _ in Z_SPLITS]

    x = x_ref[...]
    if combine:
        wc = wc_ref[...]
        x = x + pmod_ref[5:6, :] * (wc[:, 0:1] * _load_row_tiles(y0_ref) + wc[:, 1:2] * _load_row_tiles(y1_ref))
        xo_ref[...] = x
    h = (_rms(x, g_ref[...]) * (1.0 + mod_ref[1:2, :]) + mod_ref[0:1, :]).astype(BF16)
    col = 0
    for (_, width), o_ref in zip(Z_SPLITS, outs):
        for c in range(0, width, DOT_CHUNK):
            cw = min(DOT_CHUNK, width - c)
            o_ref[:, c:c + cw] = jnp.dot(h, w_ref[:, col + c:col + c + cw],
                                         preferred_element_type=F32).astype(BF16)
        col += width


def _input_projection(l, x, mods, norm1, w_in_bf, moe_out=None, wcol=None):
    combine = moe_out is not None
    tm = ROW_TILE
    nt = T_ALL // tm
    row = lambda i: (i, 0)
    mod_spec = lambda ll: pl.BlockSpec((None, None, N_MOD, D_MODEL), lambda i: (ll, _mod_row(i), 0, 0))
    in_specs = [pl.BlockSpec((tm, D_MODEL), row)]
    args = [x]
    if combine:
        in_specs += [pl.BlockSpec((tm * TILE_ROWS, 128), row),
                     pl.BlockSpec((tm * TILE_ROWS, 128), lambda i: (nt + i, 0)),
                     pl.BlockSpec((tm, TOP_K), row),
                     mod_spec(l - 1)]
        args += [moe_out, moe_out, wcol, mods]
    in_specs += [mod_spec(l),
                 pl.BlockSpec((None, 1, D_MODEL), lambda i: (l, 0, 0)),
                 pl.BlockSpec((None, D_MODEL, w_in_bf.shape[-1]), lambda i: (l, 0, 0),
                              pipeline_mode=pl.Buffered(1))]
    args += [mods, norm1.reshape(DEPTH, 1, D_MODEL), w_in_bf]
    out_specs, out_shape = [], []
    if combine:
        out_specs.append(pl.BlockSpec((tm, D_MODEL), row))
        out_shape.append(jax.ShapeDtypeStruct((T_ALL, D_MODEL), F32))
    for _, width in Z_SPLITS:
        out_specs.append(pl.BlockSpec((tm, width), row))
        out_shape.append(jax.ShapeDtypeStruct((T_ALL, width), BF16))
    res = pl.pallas_call(
        functools.partial(_in_kernel, combine=combine),
        grid=(nt,),
        in_specs=in_specs, out_specs=out_specs, out_shape=out_shape,
        compiler_params=_params(("arbitrary",), V7X_VMEM_LIMIT_BYTES),
        name="input_projection",
    )(*args)
    if combine:
        x, res = res[0], res[1:]
    return x, {name: r for (name, _), r in zip(Z_SPLITS, res)}


def _rope_tables(rows, dim, reps):
    r = jnp.repeat(jnp.arange(rows, dtype=F32), GRID_W)
    col = jnp.tile(jnp.arange(GRID_W, dtype=F32), rows)
    half = dim // 2
    inv = ROPE_BASE ** (-jnp.arange(0, half, 2, dtype=F32) / half)
    ar = r[:, None] * inv
    ac = col[:, None] * inv
    ang = jnp.concatenate([ar, ar, ac, ac], axis=-1)
    cos, sin = jnp.cos(ang), jnp.sin(ang)
    first = (jnp.arange(dim) % (dim // 2)) < (dim // 4)
    sin_up = jnp.where(first, -sin, 0.0)
    sin_dn = jnp.where(first, 0.0, sin)
    t = lambda a: jnp.tile(a, (1, reps))
    return t(cos), t(sin_up), t(sin_dn)


def _rope(x, cos, sin_up, sin_dn, quarter):
    w = x.shape[-1]
    return x * cos + pltpu.roll(x, w - quarter, 1) * sin_up + pltpu.roll(x, quarter, 1) * sin_dn


def _attn_kernel(*refs, sk_new, use_rope, has_cache, emit_kv):
    it = iter(refs)
    q_ref, kv_ref, gq_ref, gk_ref, bdq_ref, bdk_ref = (next(it) for _ in range(6))
    if use_rope:
        cq_ref, suq_ref, sdq_ref, ck_ref, suk_ref, sdk_ref = (next(it) for _ in range(6))
    if has_cache:
        kc_ref, vc_ref = next(it), next(it)
    o_ref = next(it)
    if emit_kv:
        nk_ref, nv_ref = next(it), next(it)
    kd_scr, vd_scr = next(it), next(it)

    lo = lax.broadcasted_iota(jnp.int32, (1, 2 * HEAD_DIM), 1) < HEAD_DIM

    def dup_halves(a):
        r = pltpu.roll(a, HEAD_DIM, 1)
        return jnp.where(lo, a, r), jnp.where(lo, r, a)

    @pl.when(pl.program_id(1) == 0)
    def _prep():
        kv = kv_ref[...].astype(F32)
        k, v = kv[:, :ATT_KV], kv[:, ATT_KV:]
        ms = jnp.dot((k * k).astype(BF16), bdk_ref[...], preferred_element_type=F32)
        kn = k * lax.rsqrt(ms + EPS) * gk_ref[...]
        if emit_kv:
            nk_ref[...] = kn
            nv_ref[...] = v
        if use_rope:
            kn = _rope(kn, ck_ref[...], suk_ref[...], sdk_ref[...], HEAD_DIM // 4)
        k0, k1 = dup_halves(kn)
        v0, v1 = dup_halves(v)
        kd_scr[0, 0:sk_new, :] = k0.astype(BF16)
        kd_scr[1, 0:sk_new, :] = k1.astype(BF16)
        vd_scr[0, 0:sk_new, :] = v0.astype(BF16)
        vd_scr[1, 0:sk_new, :] = v1.astype(BF16)
        if has_cache:
            c0, c1 = dup_halves(kc_ref[...])
            d0, d1 = dup_halves(vc_ref[...])
            kd_scr[0, sk_new:, :] = c0.astype(BF16)
            kd_scr[1, sk_new:, :] = c1.astype(BF16)
            vd_scr[0, sk_new:, :] = d0.astype(BF16)
            vd_scr[1, sk_new:, :] = d1.astype(BF16)

    q = q_ref[...].astype(F32)
    ms = jnp.dot((q * q).astype(BF16), bdq_ref[...], preferred_element_type=F32)
    qn = q * lax.rsqrt(ms + EPS) * gq_ref[...]
    if use_rope:
        qn = _rope(qn, cq_ref[...], suq_ref[...], sdq_ref[...], HEAD_DIM // 4)
    qn = qn * (HEAD_DIM ** -0.5)
    heads_per_kv = N_Q_HEADS // N_KV_HEADS
    for j in range(N_Q_HEADS // 2):
        grp = (2 * j) // heads_per_kv
        kd, vd = kd_scr[grp], vd_scr[grp]
        qp = qn[:, 2 * HEAD_DIM * j:2 * HEAD_DIM * (j + 1)]
        halves = []
        for qm in (jnp.where(lo, qp, 0.0), jnp.where(lo, 0.0, qp)):
            s = lax.dot_general(qm.astype(BF16), kd, (((1,), (1,)), ((), ())), preferred_element_type=F32)
            e = jnp.exp(s - jnp.max(s, axis=-1, keepdims=True))
            den = jnp.sum(e, axis=-1, keepdims=True)
            halves.append(jnp.dot(e.astype(BF16), vd, preferred_element_type=F32) / den)
        o_ref[:, 2 * HEAD_DIM * j:2 * HEAD_DIM * (j + 1)] = jnp.where(lo, halves[0], halves[1]).astype(BF16)


def _block_diag_mean(width, group):
    idx = np.arange(width) // group
    return jnp.asarray((idx[:, None] == idx[None, :]).astype(np.float32) / group, dtype=BF16)


def _attention(l, z, q_norm, k_norm, *, latent, rope=None, cache_k=None, cache_v=None):
    if latent:
        nb, s, tq, row0 = DEC_BATCH, DEC_SEQ, 256, T_CTX
    else:
        nb, s, tq, row0 = BATCH, SEQ, SEQ, 0
    nq = s // tq
    sk = s + (PAST_LEN if latent else 0)
    qrow = lambda b, qi: (row0 // tq + b * nq + qi, 0)
    krow = lambda b, qi: (row0 // s + b, 0)
    const = lambda b, qi: (0, 0)
    lrow = lambda b, qi: (l, 0, 0)
    in_specs = [pl.BlockSpec((tq, ATT_Q), qrow),
                pl.BlockSpec((s, 2 * ATT_KV), krow),
                pl.BlockSpec((None, 1, ATT_Q), lrow),
                pl.BlockSpec((None, 1, ATT_KV), lrow),
                pl.BlockSpec((ATT_Q, ATT_Q), const),
                pl.BlockSpec((ATT_KV, ATT_KV), const)]
    args = [z["qa"], z["kv"],
            jnp.tile(q_norm, (1, N_Q_HEADS)).reshape(DEPTH, 1, ATT_Q),
            jnp.tile(k_norm, (1, N_KV_HEADS)).reshape(DEPTH, 1, ATT_KV),
            _block_diag_mean(ATT_Q, HEAD_DIM), _block_diag_mean(ATT_KV, HEAD_DIM)]
    if latent:
        cq, suq, sdq, ck, suk, sdk = rope
        in_specs += [pl.BlockSpec((tq, ATT_Q), lambda b, qi: (qi, 0))] * 3
        in_specs += [pl.BlockSpec((s, ATT_KV), const)] * 3
        in_specs += [pl.BlockSpec((None, None, PAST_LEN, ATT_KV), lambda b, qi: (b, l, 0, 0))] * 2
        args += [cq, suq, sdq, ck, suk, sdk,
                 cache_k.reshape(DEC_BATCH, DEPTH, PAST_LEN, ATT_KV),
                 cache_v.reshape(DEC_BATCH, DEPTH, PAST_LEN, ATT_KV)]
    rows = nb * s
    out_specs = [pl.BlockSpec((tq, ATT_Q), lambda b, qi: (b * nq + qi, 0))]
    out_shape = [jax.ShapeDtypeStruct((rows, ATT_Q), BF16)]
    if not latent:
        out_specs += [pl.BlockSpec((s, ATT_KV), lambda b, qi: (b, 0))] * 2
        out_shape += [jax.ShapeDtypeStruct((rows, ATT_KV), F32)] * 2
    return pl.pallas_call(
        functools.partial(_attn_kernel, sk_new=s, use_rope=latent, has_cache=latent, emit_kv=not latent),
        grid=(nb, nq),
        in_specs=in_specs, out_specs=out_specs, out_shape=out_shape,
        scratch_shapes=[pltpu.VMEM((N_KV_HEADS, sk, 2 * HEAD_DIM), BF16),
                        pltpu.VMEM((N_KV_HEADS, sk, 2 * HEAD_DIM), BF16)],
        compiler_params=_params(("arbitrary", "arbitrary"), V7X_VMEM_LIMIT_BYTES),
        name="attention_latent" if latent else "attention_context",
    )(*args)


def _ret_kernel(*refs, s, heads, use_rope, has_state):
    it = iter(refs)
    q_ref, k_ref, v_ref, g_ref, dec_ref, gn_ref = (next(it) for _ in range(6))
    rope = (next(it)[...], next(it)[...], next(it)[...]) if use_rope else None
    s0f_ref, s0b_ref = (next(it), next(it)) if has_state else (None, None)
    o_ref, sf_ref, sb_ref, oacc = next(it), next(it), next(it), next(it)
    for hh in range(heads):
        cols = slice(hh * RET_DK, (hh + 1) * RET_DK)
        _ret_head(q_ref.at[:, cols], k_ref.at[:, cols], v_ref.at[:, cols], g_ref.at[:, cols],
                  dec_ref.at[:, hh], gn_ref.at[hh], rope,
                  s0f_ref.at[hh] if has_state else None, s0b_ref.at[hh] if has_state else None,
                  o_ref.at[:, cols], sf_ref.at[hh], sb_ref.at[hh], oacc.at[hh], s)


def _ret_head(q_ref, k_ref, v_ref, g_ref, dec_ref, gn_ref, rope, s0f_ref, s0b_ref, o_ref, sf_ref, sb_ref, oacc, s):
    use_rope, has_state = rope is not None, s0f_ref is not None
    n_chunks = s // BLOCK
    lgf = -jnp.exp(dec_ref[0])
    lgb = -jnp.exp(dec_ref[1])
    lgf1, lgb1 = lgf[:, 0:1], lgb[:, 0:1]
    diff = (lax.broadcasted_iota(jnp.int32, (BLOCK, BLOCK), 0)
            - lax.broadcasted_iota(jnp.int32, (BLOCK, BLOCK), 1)).astype(F32)
    dsum = (jnp.where(diff >= 0, jnp.exp(diff * lgf), 0.0)
            + jnp.where(diff <= 0, jnp.exp(-diff * lgb), 0.0))
    ic = lax.broadcasted_iota(jnp.int32, (BLOCK, 1), 0).astype(F32)
    qdf, kdf, cdf = jnp.exp((ic + 1.0) * lgf1), jnp.exp((BLOCK - 1.0 - ic) * lgf1), jnp.exp(BLOCK * lgf1)
    qdb, kdb, cdb = jnp.exp((BLOCK - ic) * lgb1), jnp.exp(ic * lgb1), jnp.exp(BLOCK * lgb1)

    q = q_ref[...].astype(F32)
    k = k_ref[...].astype(F32) * (RET_DK ** -0.5)
    if use_rope:
        q = _rope(q, *rope, RET_DK // 4)
        k = _rope(k, *rope, RET_DK // 4)
    v = v_ref[...]

    def rows(a, n):
        return a[n * BLOCK:(n + 1) * BLOCK]

    def state_step(state, kn, kdec, vn, cdec):
        kd_t = jnp.transpose(kn * kdec).astype(BF16)
        return cdec * state + jnp.dot(kd_t, vn, preferred_element_type=F32)

    state = s0f_ref[...] if has_state else jnp.zeros((RET_DK, RET_DV), F32)
    for n in range(n_chunks):
        qn, kn, vn = rows(q, n), rows(k, n), rows(v, n)
        inner = lax.dot_general(qn.astype(BF16), kn.astype(BF16), (((1,), (1,)), ((), ())),
                                preferred_element_type=F32) * dsum
        o = (jnp.dot(inner.astype(BF16), vn, preferred_element_type=F32)
             + jnp.dot((qn * qdf).astype(BF16), state.astype(BF16), preferred_element_type=F32))
        state = state_step(state, kn, kdf, vn, cdf)
        oacc[n * BLOCK:(n + 1) * BLOCK, :] = o
    sf_ref[...] = state

    state = s0b_ref[...] if has_state else jnp.zeros((RET_DK, RET_DV), F32)
    for n in reversed(range(n_chunks)):
        qn, kn, vn = rows(q, n), rows(k, n), rows(v, n)
        o = rows(oacc, n) + jnp.dot((qn * qdb).astype(BF16), state.astype(BF16), preferred_element_type=F32)
        state = state_step(state, kn, kdb, vn, cdb)
        mu = jnp.mean(o, axis=-1, keepdims=True)
        var = jnp.mean(jnp.square(o - mu), axis=-1, keepdims=True)
        y = (o - mu) * lax.rsqrt(var + EPS) * gn_ref[...]
        gate = g_ref[n * BLOCK:(n + 1) * BLOCK, :].astype(F32)
        o_ref[n * BLOCK:(n + 1) * BLOCK, :] = (y * (gate * _sigmoid(gate))).astype(BF16)
    sb_ref[...] = state


def _retention(l, z, dec, ret_gn, *, latent, rope=None, s0f=None, s0b=None):
    if latent:
        nb, s, row0, heads = DEC_BATCH, DEC_SEQ, T_CTX, 1
    else:
        nb, s, row0, heads = BATCH, SEQ, 0, N_RET_HEADS
    width = heads * RET_DK
    hrow = lambda b, h: (row0 // s + b, h)
    in_specs = [pl.BlockSpec((s, width), hrow)] * 4
    in_specs += [pl.BlockSpec((None, 2, heads, 1, RET_DK), lambda b, h: (l, 0, h, 0, 0)),
                 pl.BlockSpec((None, heads, 1, RET_DV), lambda b, h: (l, h, 0, 0))]
    args = [z["qr"], z["kr"], z["vr"], z["gr"], dec, ret_gn.reshape(DEPTH, N_RET_HEADS, 1, RET_DV)]
    if latent:
        in_specs += [pl.BlockSpec((s, RET_DK), lambda b, h: (0, 0))] * 3
        in_specs += [pl.BlockSpec((None, None, heads, RET_DK, RET_DV), lambda b, h: (b, l, h, 0, 0))] * 2
        args += [*rope, s0f, s0b]
    st_spec = pl.BlockSpec((None, heads, RET_DK, RET_DV), lambda b, h: (b, h, 0, 0))
    st_shape = jax.ShapeDtypeStruct((nb, N_RET_HEADS, RET_DK, RET_DV), F32)
    return pl.pallas_call(
        functools.partial(_ret_kernel, s=s, heads=heads, use_rope=latent, has_state=latent),
        grid=(nb, N_RET_HEADS // heads),
        in_specs=in_specs,
        out_specs=[pl.BlockSpec((s, width), lambda b, h: (b, h)), st_spec, st_spec],
        out_shape=[jax.ShapeDtypeStruct((nb * s, RET_W), BF16), st_shape, st_shape],
        scratch_shapes=[pltpu.VMEM((heads, s, RET_DV), F32)],
        compiler_params=_params(("arbitrary", "arbitrary")),
        name="retention_latent" if latent else "retention_context",
    )(*args)


def _cmlp_kernel(mu_ref, mv_ref, gn_ref, ws_ref, bs_ref, o_ref):
    u = jax.nn.gelu(mu_ref[...].astype(F32), approximate=True)
    v = jax.nn.gelu(mv_ref[...].astype(F32), approximate=True)
    vn = _rms(v, gn_ref[...]).astype(BF16)
    gc = MLP_W // N_MLP_GROUPS
    for n in range(ROW_TILE // BLOCK):
        r = slice(n * BLOCK, (n + 1) * BLOCK)
        for g in range(N_MLP_GROUPS):
            c = slice(g * gc, (g + 1) * gc)
            f = jnp.dot(ws_ref[g], vn[r, c], preferred_element_type=F32) + bs_ref[g]
            o_ref[r, c] = (u[r, c] * f).astype(BF16)


def _chunk_mlp(l, z, mlp_norm, w_spatial_bf, b_spatial):
    tm = ROW_TILE
    row = lambda i: (i, 0)
    return pl.pallas_call(
        _cmlp_kernel,
        grid=(T_ALL // tm,),
        in_specs=[pl.BlockSpec((tm, MLP_W), row), pl.BlockSpec((tm, MLP_W), row),
                  pl.BlockSpec((None, 1, MLP_W), lambda i: (l, 0, 0)),
                  pl.BlockSpec((None, N_MLP_GROUPS, BLOCK, BLOCK), lambda i: (l, 0, 0, 0)),
                  pl.BlockSpec((None, N_MLP_GROUPS, BLOCK, 1), lambda i: (l, 0, 0, 0))],
        out_specs=pl.BlockSpec((tm, MLP_W), row),
        out_shape=jax.ShapeDtypeStruct((T_ALL, MLP_W), BF16),
        compiler_params=_params(("arbitrary",)),
        name="chunk_mlp",
    )(z["mu"], z["mv"], mlp_norm.reshape(DEPTH, 1, MLP_W), w_spatial_bf,
      b_spatial.reshape(DEPTH, N_MLP_GROUPS, BLOCK, 1))


def _first_max(vals):
    best, idx = vals[0], jnp.zeros(vals[0].shape, jnp.int32)
    for j in range(1, len(vals)):
        upd = vals[j] > best
        best = jnp.where(upd, vals[j], best)
        idx = jnp.where(upd, j, idx)
    return best, idx


def _pick(idx, vals):
    out = vals[-1]
    for j in range(len(vals) - 2, -1, -1):
        out = jnp.where(idx == j, vals[j], out)
    return out


def _merge_kernel(x_ref, attc_ref, attl_ref, retc_ref, retl_ref, cm_ref, ga_ref, gb_ref, gc_ref, mod_ref, g2_ref,
                  wa_ref, wr_ref, wm_ref, wo_ref, wrt_ref, br_ref, tri_ref,
                  x1_ref, h2_ref, eid_ref, wts_ref, rank_ref, cnt_ref, cnt_scr):
    @pl.when(pl.program_id(0) == 0)
    def _zero_counts():
        cnt_scr[...] = jnp.zeros_like(cnt_scr)

    is_lat = pl.program_id(0) >= T_CTX // ROW_TILE
    att = jnp.where(is_lat, attl_ref[...], attc_ref[...])
    ret = jnp.where(is_lat, retl_ref[...], retc_ref[...])
    sig = lambda r: _sigmoid(r[...].astype(F32))
    dot = lambda a, b: jnp.dot(a, b, preferred_element_type=F32)
    mix = (sig(ga_ref) * dot(att, wa_ref[...])
           + sig(gb_ref) * dot(ret, wr_ref[...])
           + sig(gc_ref) * dot(cm_ref[...], wm_ref[...]))
    x1 = x_ref[...] + mod_ref[2:3, :] * dot(mix.astype(BF16), wo_ref[...])
    x1_ref[...] = x1
    h2 = _rms(x1, g2_ref[...]) * (1.0 + mod_ref[4:5, :]) + mod_ref[3:4, :]
    _store_row_tiles(h2_ref, h2)

    logits = lax.dot_general(wrt_ref[...], h2, (((1,), (1,)), ((), ())), preferred_element_type=F32,
                             precision=lax.Precision.HIGHEST)
    score = _sigmoid(logits)
    sel = score + br_ref[...]
    sel_rows = [sel[e:e + 1, :] for e in range(N_EXPERTS)]
    score_rows = [score[e:e + 1, :] for e in range(N_EXPERTS)]
    group_scores = []
    for g in range(N_EXPERT_GROUPS):
        v = sel_rows[g * EXPERTS_PER_GROUP:(g + 1) * EXPERTS_PER_GROUP]
        pair_sums = [v[a] + v[b] for a in range(EXPERTS_PER_GROUP) for b in range(a + 1, EXPERTS_PER_GROUP)]
        group_scores.append(functools.reduce(jnp.maximum, pair_sums))
    _, gidx = _first_max(group_scores)
    in_sel = [_pick(gidx, [sel_rows[g * EXPERTS_PER_GROUP + j] for g in range(N_EXPERT_GROUPS)])
              for j in range(EXPERTS_PER_GROUP)]
    in_score = [_pick(gidx, [score_rows[g * EXPERTS_PER_GROUP + j] for g in range(N_EXPERT_GROUPS)])
                for j in range(EXPERTS_PER_GROUP)]
    _, loc0 = _first_max(in_sel)
    _, loc1 = _first_max([jnp.where(loc0 == j, -jnp.inf, in_sel[j]) for j in range(EXPERTS_PER_GROUP)])
    w0, w1 = _pick(loc0, in_score), _pick(loc1, in_score)
    den = w0 + w1
    e0 = gidx * EXPERTS_PER_GROUP + loc0
    e1 = gidx * EXPERTS_PER_GROUP + loc1
    eid_ref[0:1, :] = e0
    eid_ref[1:2, :] = e1
    wts_ref[0:1, :] = w0 / den
    wts_ref[1:2, :] = w1 / den

    eio = lax.broadcasted_iota(jnp.int32, score.shape, 0)
    oh0 = (eio == e0).astype(F32)
    oh1 = (eio == e1).astype(F32)
    both = oh0 + oh1
    before = jnp.dot(both.astype(BF16), tri_ref[...], preferred_element_type=F32) + cnt_scr[...]
    rank_ref[0:1, :] = jnp.sum(oh0 * before, axis=0, keepdims=True).astype(jnp.int32)
    rank_ref[1:2, :] = jnp.sum(oh1 * before, axis=0, keepdims=True).astype(jnp.int32)
    cnt_scr[...] = cnt_scr[...] + jnp.sum(both, axis=1, keepdims=True)
    cnt_ref[...] = cnt_scr[...]


def _merge(l, x, att_c, att_l, ret_c, ret_l, cm, z, mods, norm2, wa, wr, wm, wo, w_router_t, b_router):
    tm = ROW_TILE
    row = lambda i: (i, 0)
    n_ctx = T_CTX // tm
    ctx_row = lambda i: (jnp.minimum(i, n_ctx - 1), 0)
    lat_row = lambda i: (jnp.maximum(i - n_ctx, 0), 0)
    lw = lambda k, n: pl.BlockSpec((None, k, n), lambda i: (l, 0, 0))
    in_specs = [pl.BlockSpec((tm, D_MODEL), row),
                pl.BlockSpec((tm, ATT_Q), ctx_row), pl.BlockSpec((tm, ATT_Q), lat_row),
                pl.BlockSpec((tm, RET_W), ctx_row), pl.BlockSpec((tm, RET_W), lat_row),
                pl.BlockSpec((tm, MLP_W), row),
                pl.BlockSpec((tm, D_MODEL), row), pl.BlockSpec((tm, D_MODEL), row), pl.BlockSpec((tm, D_MODEL), row),
                pl.BlockSpec((None, None, N_MOD, D_MODEL), lambda i: (l, _mod_row(i), 0, 0)),
                pl.BlockSpec((None, 1, D_MODEL), lambda i: (l, 0, 0)),
                lw(ATT_Q, D_MODEL), lw(RET_W, D_MODEL), lw(MLP_W, D_MODEL), lw(D_MODEL, D_MODEL),
                pl.BlockSpec((N_EXPERTS, D_MODEL), lambda i: (0, 0)),
                pl.BlockSpec((N_EXPERTS, 1), lambda i: (0, 0)),
                pl.BlockSpec((tm, tm), lambda i: (0, 0))]
    lane = lambda i: (0, i)
    tri = jnp.asarray(np.triu(np.ones((tm, tm), np.float32), k=1), dtype=BF16)
    return pl.pallas_call(
        _merge_kernel,
        grid=(T_ALL // tm,),
        in_specs=in_specs,
        out_specs=[pl.BlockSpec((tm, D_MODEL), row), pl.BlockSpec((tm * TILE_ROWS, 128), row),
                   pl.BlockSpec((TOP_K, tm), lane), pl.BlockSpec((TOP_K, tm), lane),
                   pl.BlockSpec((TOP_K, tm), lane), pl.BlockSpec((N_EXPERTS, 1), lambda i: (0, 0))],
        out_shape=[jax.ShapeDtypeStruct((T_ALL, D_MODEL), F32), jax.ShapeDtypeStruct((T_ALL * TILE_ROWS, 128), F32),
                   jax.ShapeDtypeStruct((TOP_K, T_ALL), jnp.int32), jax.ShapeDtypeStruct((TOP_K, T_ALL), F32),
                   jax.ShapeDtypeStruct((TOP_K, T_ALL), jnp.int32), jax.ShapeDtypeStruct((N_EXPERTS, 1), F32)],
        scratch_shapes=[pltpu.VMEM((N_EXPERTS, 1), F32)],
        compiler_params=_params(("arbitrary",), V7X_VMEM_LIMIT_BYTES),
        name="merge_router",
    )(x, att_c, att_l, ret_c, ret_l, cm, z["ga"], z["gb"], z["gc"], mods, norm2.reshape(DEPTH, 1, D_MODEL),
      wa, wr, wm, wo, w_router_t, b_router.reshape(N_EXPERTS, 1), tri)


def _dispatch_plan(eid, rank, counts):
    expert = jnp.arange(N_EXPERTS, dtype=jnp.int32)
    padded = (counts + MOE_BLOCK - 1) // MOE_BLOCK * MOE_BLOCK
    pend = jnp.cumsum(padded)
    pstart = pend - padded
    dest = rank + jnp.sum(jnp.where(eid[..., None] == expert, pstart, 0), axis=-1)
    block_start = jnp.arange(N_MOE_BLOCKS, dtype=jnp.int32) * MOE_BLOCK
    block_e = jnp.minimum(jnp.sum((pend[None, :] <= block_start[:, None]).astype(jnp.int32), axis=1),
                          N_EXPERTS - 1)
    seg_end = jnp.sum(jnp.where(block_e[:, None] == expert, pstart + counts, 0), axis=-1)
    n_valid = jnp.clip(seg_end - block_start, 0, MOE_BLOCK).astype(jnp.int32)
    return block_e, n_valid, dest.reshape(N_ASSIGN)


ROWS_PER_TRIP = 8


def _for_rows(n, fn):
    full = lax.div(n, ROWS_PER_TRIP)

    def group(g, c):
        for u in range(ROWS_PER_TRIP):
            fn(g * ROWS_PER_TRIP + u)
        return c
    lax.fori_loop(0, full, group, 0)

    def single(r, c):
        fn(r)
        return c
    lax.fori_loop(full * ROWS_PER_TRIP, n, single, 0)


def _moe_kernel(be_ref, nv_ref, dest_ref, h_hbm, wg_ref, wu_ref, wd_ref, out_hbm,
                xbuf, ybuf, wg_s, wu_s, wd_s, src_s, dst_s, sem_in, sem_out):
    i = pl.program_id(0)
    last = pl.num_programs(0) - 1

    def tile(first_row):
        return pl.ds(pl.multiple_of(first_row, TILE_ROWS), TILE_ROWS)

    def start_gather(blk):
        buf = lax.rem(blk, 2)

        def one(r):
            pltpu.make_async_copy(h_hbm.at[tile(src_s[blk * MOE_BLOCK + r])], xbuf.at[buf, tile(r * TILE_ROWS)],
                                  sem_in.at[buf]).start()
        _for_rows(nv_ref[blk], one)

    def start_scatter(blk):
        buf = lax.rem(blk, 2)

        def one(r):
            pltpu.make_async_copy(ybuf.at[buf, tile(r * TILE_ROWS)], out_hbm.at[tile(dst_s[blk * MOE_BLOCK + r])],
                                  sem_out.at[buf]).start()
        _for_rows(nv_ref[blk], one)

    def wait_rows(blk, gather):
        buf, n = lax.rem(blk, 2), nv_ref[blk]
        group = ROWS_PER_TRIP * TILE_ROWS
        grouped = pl.multiple_of(lax.div(n, ROWS_PER_TRIP) * group, group)

        def copy(rows):
            if gather:
                return pltpu.make_async_copy(h_hbm.at[rows], xbuf.at[buf, rows], sem_in.at[buf])
            return pltpu.make_async_copy(ybuf.at[buf, rows], out_hbm.at[rows], sem_out.at[buf])

        @pl.when(grouped > 0)
        def _():
            copy(pl.ds(0, grouped)).wait()

        def single(r, c):
            copy(pl.ds(0, TILE_ROWS)).wait()
            return c
        lax.fori_loop(lax.div(n, ROWS_PER_TRIP) * ROWS_PER_TRIP, n, single, 0)

    @pl.when(i == 0)
    def _init():
        for k in range(TOP_K):
            def put(g, c):
                for u in range(ROWS_PER_TRIP):
                    t = g * ROWS_PER_TRIP + u
                    d = dest_ref[k * T_ALL + t]
                    src_s[d] = t * TILE_ROWS
                    dst_s[d] = (k * T_ALL + t) * TILE_ROWS
                return c
            lax.fori_loop(0, T_ALL // ROWS_PER_TRIP, put, 0)
        xbuf[...] = jnp.zeros_like(xbuf)
        start_gather(0)

    @pl.when(i < last)
    def _prefetch():
        start_gather(i + 1)

    @pl.when(i >= 2)
    def _drain():
        wait_rows(i - 2, gather=False)

    @pl.when(nv_ref[i] > 0)
    def _block():
        buf = lax.rem(i, 2)
        prev = be_ref[jnp.maximum(i - 1, 0)]

        @pl.when((i == 0) | (be_ref[i] != prev))
        def _cast():
            wg_s[...] = wg_ref[...].astype(BF16)
            wu_s[...] = wu_ref[...].astype(BF16)
            wd_s[...] = wd_ref[...].astype(BF16)

        wait_rows(i, gather=True)
        x = _load_row_tiles(xbuf.at[buf]).astype(BF16)
        a = jnp.dot(x, wg_s[...], preferred_element_type=F32)
        b = jnp.dot(x, wu_s[...], preferred_element_type=F32)
        mid = ((a * _sigmoid(a)) * b).astype(BF16)
        _store_row_tiles(ybuf.at[buf], jnp.dot(mid, wd_s[...], preferred_element_type=F32))
        start_scatter(i)

    @pl.when(i == last)
    def _finish():
        wait_rows(last - 1, gather=False)
        wait_rows(last, gather=False)


def _moe(l, h2, plan, w_gate, w_up, w_down):
    block_e, n_valid, dest = plan
    wspec = lambda k, n: pl.BlockSpec((None, None, k, n), lambda i, be, nv, de: (l, be[i], 0, 0))
    grid_spec = pltpu.PrefetchScalarGridSpec(
        num_scalar_prefetch=3,
        grid=(N_MOE_BLOCKS,),
        in_specs=[pl.BlockSpec(memory_space=pl.ANY),
                  wspec(D_MODEL, EXPERT_FF), wspec(D_MODEL, EXPERT_FF), wspec(EXPERT_FF, D_MODEL)],
        out_specs=pl.BlockSpec(memory_space=pl.ANY),
        scratch_shapes=[pltpu.VMEM((2, MOE_BLOCK * TILE_ROWS, 128), F32),
                        pltpu.VMEM((2, MOE_BLOCK * TILE_ROWS, 128), F32),
                        pltpu.VMEM((D_MODEL, EXPERT_FF), BF16), pltpu.VMEM((D_MODEL, EXPERT_FF), BF16),
                        pltpu.VMEM((EXPERT_FF, D_MODEL), BF16),
                        pltpu.SMEM((N_SLOTS,), jnp.int32), pltpu.SMEM((N_SLOTS,), jnp.int32),
                        pltpu.SemaphoreType.DMA((2,)), pltpu.SemaphoreType.DMA((2,))])
    return pl.pallas_call(
        _moe_kernel,
        grid_spec=grid_spec,
        out_shape=jax.ShapeDtypeStruct((TOP_K * T_ALL * TILE_ROWS, 128), F32),
        compiler_params=_params(("arbitrary",), V7X_VMEM_LIMIT_BYTES),
        name="moe_experts",
    )(block_e, n_valid, dest, h2, w_gate, w_up, w_down)


def _combine_kernel(x_ref, y0_ref, y1_ref, wc_ref, mod_ref, o_ref):
    wc = wc_ref[...]
    o_ref[...] = x_ref[...] + mod_ref[5:6, :] * (wc[:, 0:1] * _load_row_tiles(y0_ref)
                                                 + wc[:, 1:2] * _load_row_tiles(y1_ref))


def _final_combine(l, x, moe_out, wcol, mods):
    tm = ROW_TILE
    nt = T_ALL // tm
    row = lambda i: (i, 0)
    return pl.pallas_call(
        _combine_kernel,
        grid=(nt,),
        in_specs=[pl.BlockSpec((tm, D_MODEL), row), pl.BlockSpec((tm * TILE_ROWS, 128), row),
                  pl.BlockSpec((tm * TILE_ROWS, 128), lambda i: (nt + i, 0)),
                  pl.BlockSpec((tm, TOP_K), row),
                  pl.BlockSpec((None, None, N_MOD, D_MODEL), lambda i: (l, _mod_row(i), 0, 0))],
        out_specs=pl.BlockSpec((tm, D_MODEL), row),
        out_shape=jax.ShapeDtypeStruct((T_ALL, D_MODEL), F32),
        compiler_params=_params(("arbitrary",)),
        name="final_combine",
    )(x, moe_out, moe_out, wcol, mods)


def kernel(x_prompt, x_sample, cache_k, cache_v, state_ret_fwd, state_ret_bwd, c, c_ctx, w_mod, b_mod, norm1, norm2, w_in, q_norm, k_norm, ret_decay_fwd, ret_decay_bwd, ret_gn, mlp_norm, w_spatial, b_spatial, w_att_o, w_ret_o, w_mlp_o, w_out, w_router, b_router, w_gate, w_up, w_down):
    cond = jnp.zeros((8, D_MODEL), F32).at[0].set(c_ctx).at[1:1 + DEC_BATCH].set(c)
    mods = _modulation(cond, w_mod, b_mod)

    w_in_bf = w_in.astype(BF16)
    wa, wr, wm, wo = (w.astype(BF16) for w in (w_att_o, w_ret_o, w_mlp_o, w_out))
    ws_bf = w_spatial.astype(BF16)
    w_router_t = w_router.T
    dec = jnp.broadcast_to(jnp.stack([ret_decay_fwd, ret_decay_bwd], axis=1)[..., None, None],
                           (DEPTH, 2, N_RET_HEADS, 1, RET_DK))
    rows = DEC_SEQ // GRID_W
    rope_q = _rope_tables(rows, HEAD_DIM, N_Q_HEADS)
    rope_k = _rope_tables(rows, HEAD_DIM, N_KV_HEADS)
    rope_r = _rope_tables(rows, RET_DK, 1)

    x = jnp.concatenate([x_prompt.reshape(T_CTX, D_MODEL), x_sample.reshape(T_LAT, D_MODEL)], axis=0)
    moe_out, wcol = None, None
    ks, vs, sfs, sbs = [], [], [], []
    for l in range(DEPTH):
        x, z = _input_projection(l, x, mods, norm1, w_in_bf, moe_out, wcol)

        att_c, k_l, v_l = _attention(l, z, q_norm, k_norm, latent=False)
        att_l, = _attention(l, z, q_norm, k_norm, latent=True, rope=rope_q + rope_k,
                            cache_k=cache_k, cache_v=cache_v)
        ret_c, sf_l, sb_l = _retention(l, z, dec, ret_gn, latent=False)
        ret_l, _, _ = _retention(l, z, dec, ret_gn, latent=True, rope=rope_r,
                                 s0f=state_ret_fwd, s0b=state_ret_bwd)
        cm = _chunk_mlp(l, z, mlp_norm, ws_bf, b_spatial)
        x, h2, eid, wts, rank, cnt = _merge(l, x, att_c, att_l, ret_c, ret_l, cm, z, mods, norm2,
                                            wa, wr, wm, wo, w_router_t, b_router)
        plan = _dispatch_plan(eid, rank, cnt[:, 0].astype(jnp.int32))
        moe_out = _moe(l, h2, plan, w_gate, w_up, w_down)
        wcol = wts.T
        ks.append(k_l.reshape(BATCH, SEQ, ATT_KV))
        vs.append(v_l.reshape(BATCH, SEQ, ATT_KV))
        sfs.append(sf_l)
        sbs.append(sb_l)
    x = _final_combine(DEPTH - 1, x, moe_out, wcol, mods)

    y_prompt = x[:T_CTX].reshape(BATCH, SEQ, D_MODEL)
    y_sample = x[T_CTX:].reshape(DEC_BATCH, DEC_SEQ, D_MODEL)
    new_k = jnp.stack(ks, axis=1).reshape(BATCH, DEPTH, SEQ, N_KV_HEADS, HEAD_DIM)
    new_v = jnp.stack(vs, axis=1).reshape(BATCH, DEPTH, SEQ, N_KV_HEADS, HEAD_DIM)
    new_sf = jnp.stack(sfs, axis=1)
    new_sb = jnp.stack(sbs, axis=1)
    return (y_prompt, y_sample, new_k, new_v, new_sf, new_sb)
```

```python
import functools

import numpy as np
import jax
import jax.numpy as jnp
from jax import lax
from jax.experimental import pallas as pl
from jax.experimental.pallas import tpu as pltpu

F32 = jnp.float32
BF16 = jnp.bfloat16

D_MODEL = 1024
BATCH = 32
SEQ = 256
DEPTH = 4
DEC_BATCH = 2
DEC_SEQ = 1024
PAST_LEN = 256
GRID_W = 64
BLOCK = 128
ROPE_BASE = 10000.0
EPS = 1e-6
N_Q_HEADS = 8
N_KV_HEADS = 2
HEAD_DIM = 64
ATT_Q = N_Q_HEADS * HEAD_DIM
ATT_KV = N_KV_HEADS * HEAD_DIM
N_RET_HEADS = 4
RET_DK = 128
RET_DV = 128
RET_W = N_RET_HEADS * RET_DK
N_MLP_GROUPS = 4
MLP_W = 512
N_MOD = 6
N_EXPERTS = 16
N_EXPERT_GROUPS = 4
EXPERTS_PER_GROUP = N_EXPERTS // N_EXPERT_GROUPS
TOP_K = 2
EXPERT_FF = 512
MOE_BLOCK = 128

T_CTX = BATCH * SEQ
T_LAT = DEC_BATCH * DEC_SEQ
T_ALL = T_CTX + T_LAT
N_ASSIGN = T_ALL * TOP_K
N_MOE_BLOCKS = -(-N_ASSIGN // MOE_BLOCK) + N_EXPERTS
N_SLOTS = N_MOE_BLOCKS * MOE_BLOCK

ROW_TILE = 512
V7X_VMEM_LIMIT_BYTES = 56 * 1024 * 1024

Z_SPLITS = (("qa", ATT_Q), ("kv", 2 * ATT_KV), ("qr", RET_W), ("kr", RET_W), ("vr", RET_W), ("gr", RET_W),
            ("mu", MLP_W), ("mv", MLP_W), ("ga", D_MODEL), ("gb", D_MODEL), ("gc", D_MODEL))
DOT_CHUNK = 512
Z_OUTPUTS = tuple((n, w) for n, w in Z_SPLITS if n not in ("mu", "mv")) + (("cm", MLP_W),)


def _params(sem, vmem=None):
    return pltpu.CompilerParams(dimension_semantics=sem, vmem_limit_bytes=vmem)


def _rms(x, g):
    return x * lax.rsqrt(jnp.mean(x * x, axis=-1, keepdims=True) + EPS) * g


def _sigmoid(x):
    return jax.nn.sigmoid(x)


TILE_ROWS = D_MODEL // 128


def _store_row_tiles(ref, val):
    rows = val.shape[0]
    for c in range(TILE_ROWS):
        ref[pl.ds(c, rows, stride=TILE_ROWS), :] = val[:, c * 128:(c + 1) * 128]


def _load_row_tiles(ref):
    rows = ref.shape[0] // TILE_ROWS
    return jnp.concatenate([ref[pl.ds(c, rows, stride=TILE_ROWS), :] for c in range(TILE_ROWS)], axis=1)


def _mod_row(i):
    first_lat = T_CTX // ROW_TILE
    per_batch = DEC_SEQ // ROW_TILE
    return jnp.where(i < first_lat, 0, 1 + (i - first_lat) // per_batch)


def _mod_kernel(cond_ref, w_ref, b_ref, o_ref):
    c = cond_ref[...]
    s = c * _sigmoid(c)
    o_ref[...] = jnp.dot(s, w_ref[...], preferred_element_type=F32,
                         precision=lax.Precision.HIGHEST) + b_ref[...]


def _modulation(cond, w_mod, b_mod):
    tn = D_MODEL
    out = pl.pallas_call(
        _mod_kernel,
        grid=(DEPTH, N_MOD),
        in_specs=[pl.BlockSpec((8, D_MODEL), lambda l, j: (0, 0)),
                  pl.BlockSpec((None, D_MODEL, tn), lambda l, j: (l, 0, j)),
                  pl.BlockSpec((None, 1, tn), lambda l, j: (l, 0, j))],
        out_specs=pl.BlockSpec((None, 8, tn), lambda l, j: (l, 0, j)),
        out_shape=jax.ShapeDtypeStruct((DEPTH, 8, N_MOD * D_MODEL), F32),
        compiler_params=_params(("arbitrary", "arbitrary")),
        name="modulation",
    )(cond, w_mod, b_mod.reshape(DEPTH, 1, N_MOD * D_MODEL))
    return out.reshape(DEPTH, 8, N_MOD, D_MODEL)


def _in_kernel(*refs, combine):
    it = iter(refs)
    x_ref = next(it)
    if combine:
        y0_ref, y1_ref, wc_ref, pmod_ref = next(it), next(it), next(it), next(it)
    mod_ref, g_ref, w_ref, gn_ref, ws_ref, bs_ref = (next(it) for _ in range(6))
    if combine:
        xo_ref = next(it)
    outs = {name: next(it) for name, _ in Z_OUTPUTS}

    x = x_ref[...]
    if combine:
        wc = wc_ref[...]
        x = x + pmod_ref[5:6, :] * (wc[:, 0:1] * _load_row_tiles(y0_ref) + wc[:, 1:2] * _load_row_tiles(y1_ref))
        xo_ref[...] = x
    h = (_rms(x, g_ref[...]) * (1.0 + mod_ref[1:2, :]) + mod_ref[0:1, :]).astype(BF16)
    col = 0
    kept = {}
    for name, width in Z_SPLITS:
        for c in range(0, width, DOT_CHUNK):
            cw = min(DOT_CHUNK, width - c)
            r = jnp.dot(h, w_ref[:, col + c:col + c + cw], preferred_element_type=F32)
            if name in ("mu", "mv"):
                kept[name] = r
            elif name in ("ga", "gb", "gc"):
                outs[name][:, c:c + cw] = _sigmoid(r).astype(BF16)
            elif name == "gr":
                outs[name][:, c:c + cw] = (r * _sigmoid(r)).astype(BF16)
            else:
                outs[name][:, c:c + cw] = r.astype(BF16)
        col += width

    u = jax.nn.gelu(kept["mu"], approximate=True)
    vn = _rms(jax.nn.gelu(kept["mv"], approximate=True), gn_ref[...]).astype(BF16)
    gc = MLP_W // N_MLP_GROUPS
    for n in range(ROW_TILE // BLOCK):
        rows = slice(n * BLOCK, (n + 1) * BLOCK)
        for g in range(N_MLP_GROUPS):
            cols = slice(g * gc, (g + 1) * gc)
            f = jnp.dot(ws_ref[g], vn[rows, cols], preferred_element_type=F32) + bs_ref[g]
            outs["cm"][rows, cols] = (u[rows, cols] * f).astype(BF16)


def _input_projection(l, x, mods, norm1, w_in_bf, mlp_norm, w_spatial_bf, b_spatial, moe_out=None, wcol=None):
    combine = moe_out is not None
    tm = ROW_TILE
    nt = T_ALL // tm
    row = lambda i: (i, 0)
    mod_spec = lambda ll: pl.BlockSpec((None, None, N_MOD, D_MODEL), lambda i: (ll, _mod_row(i), 0, 0))
    in_specs = [pl.BlockSpec((tm, D_MODEL), row)]
    args = [x]
    if combine:
        in_specs += [pl.BlockSpec((tm * TILE_ROWS, 128), row),
                     pl.BlockSpec((tm * TILE_ROWS, 128), lambda i: (nt + i, 0)),
                     pl.BlockSpec((tm, TOP_K), row),
                     mod_spec(l - 1)]
        args += [moe_out, moe_out, wcol, mods]
    in_specs += [mod_spec(l),
                 pl.BlockSpec((None, 1, D_MODEL), lambda i: (l, 0, 0)),
                 pl.BlockSpec((None, D_MODEL, w_in_bf.shape[-1]), lambda i: (l, 0, 0),
                              pipeline_mode=pl.Buffered(1)),
                 pl.BlockSpec((None, 1, MLP_W), lambda i: (l, 0, 0)),
                 pl.BlockSpec((None, N_MLP_GROUPS, BLOCK, BLOCK), lambda i: (l, 0, 0, 0)),
                 pl.BlockSpec((None, N_MLP_GROUPS, BLOCK, 1), lambda i: (l, 0, 0, 0))]
    args += [mods, norm1.reshape(DEPTH, 1, D_MODEL), w_in_bf, mlp_norm.reshape(DEPTH, 1, MLP_W), w_spatial_bf,
             b_spatial.reshape(DEPTH, N_MLP_GROUPS, BLOCK, 1)]
    out_specs, out_shape = [], []
    if combine:
        out_specs.append(pl.BlockSpec((tm, D_MODEL), row))
        out_shape.append(jax.ShapeDtypeStruct((T_ALL, D_MODEL), F32))
    for _, width in Z_OUTPUTS:
        out_specs.append(pl.BlockSpec((tm, width), row))
        out_shape.append(jax.ShapeDtypeStruct((T_ALL, width), BF16))
    res = pl.pallas_call(
        functools.partial(_in_kernel, combine=combine),
        grid=(nt,),
        in_specs=in_specs, out_specs=out_specs, out_shape=out_shape,
        compiler_params=_params(("arbitrary",), V7X_VMEM_LIMIT_BYTES),
        name="input_projection",
    )(*args)
    if combine:
        x, res = res[0], res[1:]
    return x, {name: r for (name, _), r in zip(Z_OUTPUTS, res)}


def _rope_tables(rows, dim, reps):
    r = jnp.repeat(jnp.arange(rows, dtype=F32), GRID_W)
    col = jnp.tile(jnp.arange(GRID_W, dtype=F32), rows)
    half = dim // 2
    inv = ROPE_BASE ** (-jnp.arange(0, half, 2, dtype=F32) / half)
    ar = r[:, None] * inv
    ac = col[:, None] * inv
    ang = jnp.concatenate([ar, ar, ac, ac], axis=-1)
    cos, sin = jnp.cos(ang), jnp.sin(ang)
    first = (jnp.arange(dim) % (dim // 2)) < (dim // 4)
    sin_up = jnp.where(first, -sin, 0.0)
    sin_dn = jnp.where(first, 0.0, sin)
    t = lambda a: jnp.tile(a, (1, reps))
    return t(cos), t(sin_up), t(sin_dn)


def _rope(x, cos, sin_up, sin_dn, quarter):
    w = x.shape[-1]
    return x * cos + pltpu.roll(x, w - quarter, 1) * sin_up + pltpu.roll(x, quarter, 1) * sin_dn


def _attn_kernel(*refs, sk_new, use_rope, has_cache, emit_kv):
    it = iter(refs)
    q_ref, kv_ref, gq_ref, gk_ref, bdq_ref, bdk_ref = (next(it) for _ in range(6))
    if use_rope:
        cq_ref, suq_ref, sdq_ref, ck_ref, suk_ref, sdk_ref = (next(it) for _ in range(6))
    if has_cache:
        kc_ref, vc_ref = next(it), next(it)
    o_ref = next(it)
    if emit_kv:
        nk_ref, nv_ref = next(it), next(it)
    kd_scr, vd_scr = next(it), next(it)

    lo = lax.broadcasted_iota(jnp.int32, (1, 2 * HEAD_DIM), 1) < HEAD_DIM

    def dup_halves(a):
        r = pltpu.roll(a, HEAD_DIM, 1)
        return jnp.where(lo, a, r), jnp.where(lo, r, a)

    @pl.when(pl.program_id(1) == 0)
    def _prep():
        kv = kv_ref[...].astype(F32)
        k, v = kv[:, :ATT_KV], kv[:, ATT_KV:]
        ms = jnp.dot((k * k).astype(BF16), bdk_ref[...], preferred_element_type=F32)
        kn = k * lax.rsqrt(ms + EPS) * gk_ref[...]
        if emit_kv:
            nk_ref[...] = kn
            nv_ref[...] = v
        if use_rope:
            kn = _rope(kn, ck_ref[...], suk_ref[...], sdk_ref[...], HEAD_DIM // 4)
        k0, k1 = dup_halves(kn)
        v0, v1 = dup_halves(v)
        kd_scr[0, 0:sk_new, :] = k0.astype(BF16)
        kd_scr[1, 0:sk_new, :] = k1.astype(BF16)
        vd_scr[0, 0:sk_new, :] = v0.astype(BF16)
        vd_scr[1, 0:sk_new, :] = v1.astype(BF16)
        if has_cache:
            c0, c1 = dup_halves(kc_ref[...])
            d0, d1 = dup_halves(vc_ref[...])
            kd_scr[0, sk_new:, :] = c0.astype(BF16)
            kd_scr[1, sk_new:, :] = c1.astype(BF16)
            vd_scr[0, sk_new:, :] = d0.astype(BF16)
            vd_scr[1, sk_new:, :] = d1.astype(BF16)

    q = q_ref[...].astype(F32)
    ms = jnp.dot((q * q).astype(BF16), bdq_ref[...], preferred_element_type=F32)
    qn = q * lax.rsqrt(ms + EPS) * gq_ref[...]
    if use_rope:
        qn = _rope(qn, cq_ref[...], suq_ref[...], sdq_ref[...], HEAD_DIM // 4)
    qn = qn * (HEAD_DIM ** -0.5)
    heads_per_kv = N_Q_HEADS // N_KV_HEADS
    for j in range(N_Q_HEADS // 2):
        grp = (2 * j) // heads_per_kv
        kd, vd = kd_scr[grp], vd_scr[grp]
        qp = qn[:, 2 * HEAD_DIM * j:2 * HEAD_DIM * (j + 1)]
        halves = []
        for qm in (jnp.where(lo, qp, 0.0), jnp.where(lo, 0.0, qp)):
            s = lax.dot_general(qm.astype(BF16), kd, (((1,), (1,)), ((), ())), preferred_element_type=F32)
            e = jnp.exp(s - jnp.max(s, axis=-1, keepdims=True))
            den = jnp.sum(e, axis=-1, keepdims=True)
            halves.append(jnp.dot(e.astype(BF16), vd, preferred_element_type=F32) / den)
        o_ref[:, 2 * HEAD_DIM * j:2 * HEAD_DIM * (j + 1)] = jnp.where(lo, halves[0], halves[1]).astype(BF16)


def _block_diag_mean(width, group):
    idx = np.arange(width) // group
    return jnp.asarray((idx[:, None] == idx[None, :]).astype(np.float32) / group, dtype=BF16)


def _attention(l, z, q_norm, k_norm, *, latent, rope=None, cache_k=None, cache_v=None):
    if latent:
        nb, s, tq, row0 = DEC_BATCH, DEC_SEQ, 256, T_CTX
    else:
        nb, s, tq, row0 = BATCH, SEQ, SEQ, 0
    nq = s // tq
    sk = s + (PAST_LEN if latent else 0)
    qrow = lambda b, qi: (row0 // tq + b * nq + qi, 0)
    krow = lambda b, qi: (row0 // s + b, 0)
    const = lambda b, qi: (0, 0)
    lrow = lambda b, qi: (l, 0, 0)
    in_specs = [pl.BlockSpec((tq, ATT_Q), qrow),
                pl.BlockSpec((s, 2 * ATT_KV), krow),
                pl.BlockSpec((None, 1, ATT_Q), lrow),
                pl.BlockSpec((None, 1, ATT_KV), lrow),
                pl.BlockSpec((ATT_Q, ATT_Q), const),
                pl.BlockSpec((ATT_KV, ATT_KV), const)]
    args = [z["qa"], z["kv"],
            jnp.tile(q_norm, (1, N_Q_HEADS)).reshape(DEPTH, 1, ATT_Q),
            jnp.tile(k_norm, (1, N_KV_HEADS)).reshape(DEPTH, 1, ATT_KV),
            _block_diag_mean(ATT_Q, HEAD_DIM), _block_diag_mean(ATT_KV, HEAD_DIM)]
    if latent:
        cq, suq, sdq, ck, suk, sdk = rope
        in_specs += [pl.BlockSpec((tq, ATT_Q), lambda b, qi: (qi, 0))] * 3
        in_specs += [pl.BlockSpec((s, ATT_KV), const)] * 3
        in_specs += [pl.BlockSpec((None, None, PAST_LEN, ATT_KV), lambda b, qi: (b, l, 0, 0))] * 2
        args += [cq, suq, sdq, ck, suk, sdk,
                 cache_k.reshape(DEC_BATCH, DEPTH, PAST_LEN, ATT_KV),
                 cache_v.reshape(DEC_BATCH, DEPTH, PAST_LEN, ATT_KV)]
    rows = nb * s
    out_specs = [pl.BlockSpec((tq, ATT_Q), lambda b, qi: (b * nq + qi, 0))]
    out_shape = [jax.ShapeDtypeStruct((rows, ATT_Q), BF16)]
    if not latent:
        out_specs += [pl.BlockSpec((s, ATT_KV), lambda b, qi: (b, 0))] * 2
        out_shape += [jax.ShapeDtypeStruct((rows, ATT_KV), F32)] * 2
    return pl.pallas_call(
        functools.partial(_attn_kernel, sk_new=s, use_rope=latent, has_cache=latent, emit_kv=not latent),
        grid=(nb, nq),
        in_specs=in_specs, out_specs=out_specs, out_shape=out_shape,
        scratch_shapes=[pltpu.VMEM((N_KV_HEADS, sk, 2 * HEAD_DIM), BF16),
                        pltpu.VMEM((N_KV_HEADS, sk, 2 * HEAD_DIM), BF16)],
        compiler_params=_params(("arbitrary", "arbitrary"), V7X_VMEM_LIMIT_BYTES),
        name="attention_latent" if latent else "attention_context",
    )(*args)


def _ret_kernel(*refs, s, heads, use_rope, has_state):
    it = iter(refs)
    q_ref, k_ref, v_ref, g_ref, dec_ref, gn_ref = (next(it) for _ in range(6))
    rope = (next(it)[...], next(it)[...], next(it)[...]) if use_rope else None
    s0f_ref, s0b_ref = (next(it), next(it)) if has_state else (None, None)
    o_ref, sf_ref, sb_ref, oacc = next(it), next(it), next(it), next(it)
    for hh in range(heads):
        cols = slice(hh * RET_DK, (hh + 1) * RET_DK)
        _ret_head(q_ref.at[:, cols], k_ref.at[:, cols], v_ref.at[:, cols], g_ref.at[:, cols],
                  dec_ref.at[:, hh], gn_ref.at[hh], rope,
                  s0f_ref.at[hh] if has_state else None, s0b_ref.at[hh] if has_state else None,
                  o_ref.at[:, cols], sf_ref.at[hh], sb_ref.at[hh], oacc.at[hh], s)


def _ret_head(q_ref, k_ref, v_ref, g_ref, dec_ref, gn_ref, rope, s0f_ref, s0b_ref, o_ref, sf_ref, sb_ref, oacc, s):
    use_rope, has_state = rope is not None, s0f_ref is not None
    n_chunks = s // BLOCK
    lgf = -jnp.exp(dec_ref[0])
    lgb = -jnp.exp(dec_ref[1])
    lgf1, lgb1 = lgf[:, 0:1], lgb[:, 0:1]
    diff = (lax.broadcasted_iota(jnp.int32, (BLOCK, BLOCK), 0)
            - lax.broadcasted_iota(jnp.int32, (BLOCK, BLOCK), 1)).astype(F32)
    dsum = (jnp.where(diff >= 0, jnp.exp(diff * lgf), 0.0)
            + jnp.where(diff <= 0, jnp.exp(-diff * lgb), 0.0))
    ic = lax.broadcasted_iota(jnp.int32, (BLOCK, 1), 0).astype(F32)
    qdf, kdf, cdf = jnp.exp((ic + 1.0) * lgf1), jnp.exp((BLOCK - 1.0 - ic) * lgf1), jnp.exp(BLOCK * lgf1)
    qdb, kdb, cdb = jnp.exp((BLOCK - ic) * lgb1), jnp.exp(ic * lgb1), jnp.exp(BLOCK * lgb1)

    q = q_ref[...].astype(F32)
    k = k_ref[...].astype(F32) * (RET_DK ** -0.5)
    if use_rope:
        q = _rope(q, *rope, RET_DK // 4)
        k = _rope(k, *rope, RET_DK // 4)
    v = v_ref[...]

    def rows(a, n):
        return a[n * BLOCK:(n + 1) * BLOCK]

    def state_step(state, kn, kdec, vn, cdec):
        kd_t = jnp.transpose(kn * kdec).astype(BF16)
        return cdec * state + jnp.dot(kd_t, vn, preferred_element_type=F32)

    state = s0f_ref[...] if has_state else jnp.zeros((RET_DK, RET_DV), F32)
    for n in range(n_chunks):
        qn, kn, vn = rows(q, n), rows(k, n), rows(v, n)
        inner = lax.dot_general(qn.astype(BF16), kn.astype(BF16), (((1,), (1,)), ((), ())),
                                preferred_element_type=F32) * dsum
        o = (jnp.dot(inner.astype(BF16), vn, preferred_element_type=F32)
             + jnp.dot((qn * qdf).astype(BF16), state.astype(BF16), preferred_element_type=F32))
        state = state_step(state, kn, kdf, vn, cdf)
        oacc[n * BLOCK:(n + 1) * BLOCK, :] = o
    sf_ref[...] = state

    state = s0b_ref[...] if has_state else jnp.zeros((RET_DK, RET_DV), F32)
    for n in reversed(range(n_chunks)):
        qn, kn, vn = rows(q, n), rows(k, n), rows(v, n)
        o = rows(oacc, n) + jnp.dot((qn * qdb).astype(BF16), state.astype(BF16), preferred_element_type=F32)
        state = state_step(state, kn, kdb, vn, cdb)
        mu = jnp.mean(o, axis=-1, keepdims=True)
        var = jnp.mean(jnp.square(o - mu), axis=-1, keepdims=True)
        y = (o - mu) * lax.rsqrt(var + EPS) * gn_ref[...]
        gate = g_ref[n * BLOCK:(n + 1) * BLOCK, :].astype(F32)
        o_ref[n * BLOCK:(n + 1) * BLOCK, :] = (y * gate).astype(BF16)
    sb_ref[...] = state


def _retention(l, z, dec, ret_gn, *, latent, rope=None, s0f=None, s0b=None):
    if latent:
        nb, s, row0, heads = DEC_BATCH, DEC_SEQ, T_CTX, 1
    else:
        nb, s, row0, heads = BATCH, SEQ, 0, N_RET_HEADS
    width = heads * RET_DK
    hrow = lambda b, h: (row0 // s + b, h)
    in_specs = [pl.BlockSpec((s, width), hrow)] * 4
    in_specs += [pl.BlockSpec((None, 2, heads, 1, RET_DK), lambda b, h: (l, 0, h, 0, 0)),
                 pl.BlockSpec((None, heads, 1, RET_DV), lambda b, h: (l, h, 0, 0))]
    args = [z["qr"], z["kr"], z["vr"], z["gr"], dec, ret_gn.reshape(DEPTH, N_RET_HEADS, 1, RET_DV)]
    if latent:
        in_specs += [pl.BlockSpec((s, RET_DK), lambda b, h: (0, 0))] * 3
        in_specs += [pl.BlockSpec((None, None, heads, RET_DK, RET_DV), lambda b, h: (b, l, h, 0, 0))] * 2
        args += [*rope, s0f, s0b]
    st_spec = pl.BlockSpec((None, heads, RET_DK, RET_DV), lambda b, h: (b, h, 0, 0))
    st_shape = jax.ShapeDtypeStruct((nb, N_RET_HEADS, RET_DK, RET_DV), F32)
    return pl.pallas_call(
        functools.partial(_ret_kernel, s=s, heads=heads, use_rope=latent, has_state=latent),
        grid=(nb, N_RET_HEADS // heads),
        in_specs=in_specs,
        out_specs=[pl.BlockSpec((s, width), lambda b, h: (b, h)), st_spec, st_spec],
        out_shape=[jax.ShapeDtypeStruct((nb * s, RET_W), BF16), st_shape, st_shape],
        scratch_shapes=[pltpu.VMEM((heads, s, RET_DV), F32)],
        compiler_params=_params(("arbitrary", "arbitrary")),
        name="retention_latent" if latent else "retention_context",
    )(*args)


def _first_max(vals):
    best, idx = vals[0], jnp.zeros(vals[0].shape, jnp.int32)
    for j in range(1, len(vals)):
        upd = vals[j] > best
        best = jnp.where(upd, vals[j], best)
        idx = jnp.where(upd, j, idx)
    return best, idx


def _pick(idx, vals):
    out = vals[-1]
    for j in range(len(vals) - 2, -1, -1):
        out = jnp.where(idx == j, vals[j], out)
    return out


def _merge_kernel(x_ref, attc_ref, attl_ref, retc_ref, retl_ref, cm_ref, ga_ref, gb_ref, gc_ref, mod_ref, g2_ref,
                  wa_ref, wr_ref, wm_ref, wo_ref, wrt_ref, br_ref, tri_ref,
                  x1_ref, h2_ref, eid_ref, wts_ref, rank_ref, cnt_ref, cnt_scr):
    @pl.when(pl.program_id(0) == 0)
    def _zero_counts():
        cnt_scr[...] = jnp.zeros_like(cnt_scr)

    is_lat = pl.program_id(0) >= T_CTX // ROW_TILE
    att = jnp.where(is_lat, attl_ref[...], attc_ref[...])
    ret = jnp.where(is_lat, retl_ref[...], retc_ref[...])
    gate = lambda r: r[...].astype(F32)
    dot = lambda a, b: jnp.dot(a, b, preferred_element_type=F32)
    mix = (gate(ga_ref) * dot(att, wa_ref[...])
           + gate(gb_ref) * dot(ret, wr_ref[...])
           + gate(gc_ref) * dot(cm_ref[...], wm_ref[...]))
    x1 = x_ref[...] + mod_ref[2:3, :] * dot(mix.astype(BF16), wo_ref[...])
    x1_ref[...] = x1
    h2 = _rms(x1, g2_ref[...]) * (1.0 + mod_ref[4:5, :]) + mod_ref[3:4, :]
    _store_row_tiles(h2_ref, h2)

    logits = lax.dot_general(wrt_ref[...], h2, (((1,), (1,)), ((), ())), preferred_element_type=F32,
                             precision=lax.Precision.HIGHEST)
    score = _sigmoid(logits)
    sel = score + br_ref[...]
    sel_rows = [sel[e:e + 1, :] for e in range(N_EXPERTS)]
    score_rows = [score[e:e + 1, :] for e in range(N_EXPERTS)]
    group_scores = []
    for g in range(N_EXPERT_GROUPS):
        v = sel_rows[g * EXPERTS_PER_GROUP:(g + 1) * EXPERTS_PER_GROUP]
        pair_sums = [v[a] + v[b] for a in range(EXPERTS_PER_GROUP) for b in range(a + 1, EXPERTS_PER_GROUP)]
        group_scores.append(functools.reduce(jnp.maximum, pair_sums))
    _, gidx = _first_max(group_scores)
    in_sel = [_pick(gidx, [sel_rows[g * EXPERTS_PER_GROUP + j] for g in range(N_EXPERT_GROUPS)])
              for j in range(EXPERTS_PER_GROUP)]
    in_score = [_pick(gidx, [score_rows[g * EXPERTS_PER_GROUP + j] for g in range(N_EXPERT_GROUPS)])
                for j in range(EXPERTS_PER_GROUP)]
    _, loc0 = _first_max(in_sel)
    _, loc1 = _first_max([jnp.where(loc0 == j, -jnp.inf, in_sel[j]) for j in range(EXPERTS_PER_GROUP)])
    w0, w1 = _pick(loc0, in_score), _pick(loc1, in_score)
    den = w0 + w1
    e0 = gidx * EXPERTS_PER_GROUP + loc0
    e1 = gidx * EXPERTS_PER_GROUP + loc1
    eid_ref[0:1, :] = e0
    eid_ref[1:2, :] = e1
    wts_ref[0:1, :] = w0 / den
    wts_ref[1:2, :] = w1 / den

    eio = lax.broadcasted_iota(jnp.int32, score.shape, 0)
    oh0 = (eio == e0).astype(F32)
    oh1 = (eio == e1).astype(F32)
    both = oh0 + oh1
    before = jnp.dot(both.astype(BF16), tri_ref[...], preferred_element_type=F32) + cnt_scr[...]
    rank_ref[0:1, :] = jnp.sum(oh0 * before, axis=0, keepdims=True).astype(jnp.int32)
    rank_ref[1:2, :] = jnp.sum(oh1 * before, axis=0, keepdims=True).astype(jnp.int32)
    cnt_scr[...] = cnt_scr[...] + jnp.sum(both, axis=1, keepdims=True)
    cnt_ref[...] = cnt_scr[...]


def _merge(l, x, att_c, att_l, ret_c, ret_l, cm, z, mods, norm2, wa, wr, wm, wo, w_router_t, b_router):
    tm = ROW_TILE
    row = lambda i: (i, 0)
    n_ctx = T_CTX // tm
    ctx_row = lambda i: (jnp.minimum(i, n_ctx - 1), 0)
    lat_row = lambda i: (jnp.maximum(i - n_ctx, 0), 0)
    lw = lambda k, n: pl.BlockSpec((None, k, n), lambda i: (l, 0, 0))
    in_specs = [pl.BlockSpec((tm, D_MODEL), row),
                pl.BlockSpec((tm, ATT_Q), ctx_row), pl.BlockSpec((tm, ATT_Q), lat_row),
                pl.BlockSpec((tm, RET_W), ctx_row), pl.BlockSpec((tm, RET_W), lat_row),
                pl.BlockSpec((tm, MLP_W), row),
                pl.BlockSpec((tm, D_MODEL), row), pl.BlockSpec((tm, D_MODEL), row), pl.BlockSpec((tm, D_MODEL), row),
                pl.BlockSpec((None, None, N_MOD, D_MODEL), lambda i: (l, _mod_row(i), 0, 0)),
                pl.BlockSpec((None, 1, D_MODEL), lambda i: (l, 0, 0)),
                lw(ATT_Q, D_MODEL), lw(RET_W, D_MODEL), lw(MLP_W, D_MODEL), lw(D_MODEL, D_MODEL),
                pl.BlockSpec((N_EXPERTS, D_MODEL), lambda i: (0, 0)),
                pl.BlockSpec((N_EXPERTS, 1), lambda i: (0, 0)),
                pl.BlockSpec((tm, tm), lambda i: (0, 0))]
    lane = lambda i: (0, i)
    tri = jnp.asarray(np.triu(np.ones((tm, tm), np.float32), k=1), dtype=BF16)
    return pl.pallas_call(
        _merge_kernel,
        grid=(T_ALL // tm,),
        in_specs=in_specs,
        out_specs=[pl.BlockSpec((tm, D_MODEL), row), pl.BlockSpec((tm * TILE_ROWS, 128), row),
                   pl.BlockSpec((TOP_K, tm), lane), pl.BlockSpec((TOP_K, tm), lane),
                   pl.BlockSpec((TOP_K, tm), lane), pl.BlockSpec((N_EXPERTS, 1), lambda i: (0, 0))],
        out_shape=[jax.ShapeDtypeStruct((T_ALL, D_MODEL), F32), jax.ShapeDtypeStruct((T_ALL * TILE_ROWS, 128), F32),
                   jax.ShapeDtypeStruct((TOP_K, T_ALL), jnp.int32), jax.ShapeDtypeStruct((TOP_K, T_ALL), F32),
                   jax.ShapeDtypeStruct((TOP_K, T_ALL), jnp.int32), jax.ShapeDtypeStruct((N_EXPERTS, 1), F32)],
        scratch_shapes=[pltpu.VMEM((N_EXPERTS, 1), F32)],
        compiler_params=_params(("arbitrary",), V7X_VMEM_LIMIT_BYTES),
        name="merge_router",
    )(x, att_c, att_l, ret_c, ret_l, cm, z["ga"], z["gb"], z["gc"], mods, norm2.reshape(DEPTH, 1, D_MODEL),
      wa, wr, wm, wo, w_router_t, b_router.reshape(N_EXPERTS, 1), tri)


def _dispatch_plan(eid, rank, counts):
    expert = jnp.arange(N_EXPERTS, dtype=jnp.int32)
    padded = (counts + MOE_BLOCK - 1) // MOE_BLOCK * MOE_BLOCK
    pend = jnp.cumsum(padded)
    pstart = pend - padded
    dest = rank + jnp.sum(jnp.where(eid[..., None] == expert, pstart, 0), axis=-1)
    block_start = jnp.arange(N_MOE_BLOCKS, dtype=jnp.int32) * MOE_BLOCK
    block_e = jnp.minimum(jnp.sum((pend[None, :] <= block_start[:, None]).astype(jnp.int32), axis=1),
                          N_EXPERTS - 1)
    seg_end = jnp.sum(jnp.where(block_e[:, None] == expert, pstart + counts, 0), axis=-1)
    n_valid = jnp.clip(seg_end - block_start, 0, MOE_BLOCK).astype(jnp.int32)
    return block_e, n_valid, dest.reshape(N_ASSIGN)


ROWS_PER_TRIP = 8


N_DMA_PRIORITIES = 2


def _for_rows(n, fn):
    full = lax.div(n, ROWS_PER_TRIP)

    def group(g, c):
        for u in range(ROWS_PER_TRIP):
            fn(g * ROWS_PER_TRIP + u, u % N_DMA_PRIORITIES)
        return c
    lax.fori_loop(0, full, group, 0)

    def single(r, c):
        fn(r, 0)
        return c
    lax.fori_loop(full * ROWS_PER_TRIP, n, single, 0)


def _moe_kernel(be_ref, nv_ref, dest_ref, h_hbm, wg_ref, wu_ref, wd_ref, out_hbm,
                xbuf, ybuf, wg_s, wu_s, wd_s, src_s, dst_s, sem_in, sem_out):
    i = pl.program_id(0)
    last = pl.num_programs(0) - 1

    def tile(first_row):
        return pl.ds(pl.multiple_of(first_row, TILE_ROWS), TILE_ROWS)

    def start_gather(blk):
        buf = lax.rem(blk, 2)

        def one(r, priority):
            pltpu.make_async_copy(h_hbm.at[tile(src_s[blk * MOE_BLOCK + r])], xbuf.at[buf, tile(r * TILE_ROWS)],
                                  sem_in.at[buf]).start(priority=priority)
        _for_rows(nv_ref[blk], one)

    def start_scatter(blk):
        buf = lax.rem(blk, 2)

        def one(r, priority):
            pltpu.make_async_copy(ybuf.at[buf, tile(r * TILE_ROWS)], out_hbm.at[tile(dst_s[blk * MOE_BLOCK + r])],
                                  sem_out.at[buf]).start(priority=priority)
        _for_rows(nv_ref[blk], one)

    def wait_rows(blk, gather):
        buf, n = lax.rem(blk, 2), nv_ref[blk]
        group = ROWS_PER_TRIP * TILE_ROWS
        grouped = pl.multiple_of(lax.div(n, ROWS_PER_TRIP) * group, group)

        def copy(rows):
            if gather:
                return pltpu.make_async_copy(h_hbm.at[rows], xbuf.at[buf, rows], sem_in.at[buf])
            return pltpu.make_async_copy(ybuf.at[buf, rows], out_hbm.at[rows], sem_out.at[buf])

        @pl.when(grouped > 0)
        def _():
            copy(pl.ds(0, grouped)).wait()

        def single(r, c):
            copy(pl.ds(0, TILE_ROWS)).wait()
            return c
        lax.fori_loop(lax.div(n, ROWS_PER_TRIP) * ROWS_PER_TRIP, n, single, 0)

    @pl.when(i == 0)
    def _init():
        for k in range(TOP_K):
            def put(g, c):
                for u in range(ROWS_PER_TRIP):
                    t = g * ROWS_PER_TRIP + u
                    d = dest_ref[k * T_ALL + t]
                    src_s[d] = t * TILE_ROWS
                    dst_s[d] = (k * T_ALL + t) * TILE_ROWS
                return c
            lax.fori_loop(0, T_ALL // ROWS_PER_TRIP, put, 0)
        xbuf[...] = jnp.zeros_like(xbuf)
        start_gather(0)

    @pl.when(i < last)
    def _prefetch():
        start_gather(i + 1)

    @pl.when(i >= 2)
    def _drain():
        wait_rows(i - 2, gather=False)

    @pl.when(nv_ref[i] > 0)
    def _block():
        buf = lax.rem(i, 2)
        prev = be_ref[jnp.maximum(i - 1, 0)]

        @pl.when((i == 0) | (be_ref[i] != prev))
        def _cast():
            wg_s[...] = wg_ref[...].astype(BF16)
            wu_s[...] = wu_ref[...].astype(BF16)
            wd_s[...] = wd_ref[...].astype(BF16)

        wait_rows(i, gather=True)
        x = _load_row_tiles(xbuf.at[buf]).astype(BF16)
        a = jnp.dot(x, wg_s[...], preferred_element_type=F32)
        b = jnp.dot(x, wu_s[...], preferred_element_type=F32)
        mid = ((a * _sigmoid(a)) * b).astype(BF16)
        _store_row_tiles(ybuf.at[buf], jnp.dot(mid, wd_s[...], preferred_element_type=F32))
        start_scatter(i)

    @pl.when(i == last)
    def _finish():
        wait_rows(last - 1, gather=False)
        wait_rows(last, gather=False)


def _moe(l, h2, plan, w_gate, w_up, w_down):
    block_e, n_valid, dest = plan
    wspec = lambda k, n: pl.BlockSpec((None, None, k, n), lambda i, be, nv, de: (l, be[i], 0, 0))
    grid_spec = pltpu.PrefetchScalarGridSpec(
        num_scalar_prefetch=3,
        grid=(N_MOE_BLOCKS,),
        in_specs=[pl.BlockSpec(memory_space=pl.ANY),
                  wspec(D_MODEL, EXPERT_FF), wspec(D_MODEL, EXPERT_FF), wspec(EXPERT_FF, D_MODEL)],
        out_specs=pl.BlockSpec(memory_space=pl.ANY),
        scratch_shapes=[pltpu.VMEM((2, MOE_BLOCK * TILE_ROWS, 128), F32),
                        pltpu.VMEM((2, MOE_BLOCK * TILE_ROWS, 128), F32),
                        pltpu.VMEM((D_MODEL, EXPERT_FF), BF16), pltpu.VMEM((D_MODEL, EXPERT_FF), BF16),
                        pltpu.VMEM((EXPERT_FF, D_MODEL), BF16),
                        pltpu.SMEM((N_SLOTS,), jnp.int32), pltpu.SMEM((N_SLOTS,), jnp.int32),
                        pltpu.SemaphoreType.DMA((2,)), pltpu.SemaphoreType.DMA((2,))])
    return pl.pallas_call(
        _moe_kernel,
        grid_spec=grid_spec,
        out_shape=jax.ShapeDtypeStruct((TOP_K * T_ALL * TILE_ROWS, 128), F32),
        compiler_params=_params(("arbitrary",), V7X_VMEM_LIMIT_BYTES),
        name="moe_experts",
    )(block_e, n_valid, dest, h2, w_gate, w_up, w_down)


def _combine_kernel(x_ref, y0_ref, y1_ref, wc_ref, mod_ref, oc_ref, ol_ref):
    wc = wc_ref[...]
    x = x_ref[...] + mod_ref[5:6, :] * (wc[:, 0:1] * _load_row_tiles(y0_ref) + wc[:, 1:2] * _load_row_tiles(y1_ref))
    is_lat = pl.program_id(0) >= T_CTX // ROW_TILE

    @pl.when(jnp.logical_not(is_lat))
    def _():
        oc_ref[...] = x

    @pl.when(is_lat)
    def _():
        ol_ref[...] = x


def _final_combine(l, x, moe_out, wcol, mods):
    tm = ROW_TILE
    nt = T_ALL // tm
    n_ctx = T_CTX // tm
    row = lambda i: (i, 0)
    return pl.pallas_call(
        _combine_kernel,
        grid=(nt,),
        in_specs=[pl.BlockSpec((tm, D_MODEL), row), pl.BlockSpec((tm * TILE_ROWS, 128), row),
                  pl.BlockSpec((tm * TILE_ROWS, 128), lambda i: (nt + i, 0)),
                  pl.BlockSpec((tm, TOP_K), row),
                  pl.BlockSpec((None, None, N_MOD, D_MODEL), lambda i: (l, _mod_row(i), 0, 0))],
        out_specs=[pl.BlockSpec((tm, D_MODEL), lambda i: (jnp.minimum(i, n_ctx - 1), 0)),
                   pl.BlockSpec((tm, D_MODEL), lambda i: (jnp.maximum(i - n_ctx, 0), 0))],
        out_shape=[jax.ShapeDtypeStruct((T_CTX, D_MODEL), F32), jax.ShapeDtypeStruct((T_LAT, D_MODEL), F32)],
        compiler_params=_params(("arbitrary",)),
        name="final_combine",
    )(x, moe_out, moe_out, wcol, mods)


def kernel(x_prompt, x_sample, cache_k, cache_v, state_ret_fwd, state_ret_bwd, c, c_ctx, w_mod, b_mod, norm1, norm2, w_in, q_norm, k_norm, ret_decay_fwd, ret_decay_bwd, ret_gn, mlp_norm, w_spatial, b_spatial, w_att_o, w_ret_o, w_mlp_o, w_out, w_router, b_router, w_gate, w_up, w_down):
    cond = jnp.zeros((8, D_MODEL), F32).at[0].set(c_ctx).at[1:1 + DEC_BATCH].set(c)
    mods = _modulation(cond, w_mod, b_mod)

    w_in_bf = w_in.astype(BF16)
    wa, wr, wm, wo = (w.astype(BF16) for w in (w_att_o, w_ret_o, w_mlp_o, w_out))
    ws_bf = w_spatial.astype(BF16)
    w_router_t = w_router.T
    dec = jnp.broadcast_to(jnp.stack([ret_decay_fwd, ret_decay_bwd], axis=1)[..., None, None],
                           (DEPTH, 2, N_RET_HEADS, 1, RET_DK))
    rows = DEC_SEQ // GRID_W
    rope_q = _rope_tables(rows, HEAD_DIM, N_Q_HEADS)
    rope_k = _rope_tables(rows, HEAD_DIM, N_KV_HEADS)
    rope_r = _rope_tables(rows, RET_DK, 1)

    x = jnp.concatenate([x_prompt.reshape(T_CTX, D_MODEL), x_sample.reshape(T_LAT, D_MODEL)], axis=0)
    moe_out, wcol = None, None
    ks, vs, sfs, sbs = [], [], [], []
    for l in range(DEPTH):
        x, z = _input_projection(l, x, mods, norm1, w_in_bf, mlp_norm, ws_bf, b_spatial, moe_out, wcol)

        att_c, k_l, v_l = _attention(l, z, q_norm, k_norm, latent=False)
        att_l, = _attention(l, z, q_norm, k_norm, latent=True, rope=rope_q + rope_k,
                            cache_k=cache_k, cache_v=cache_v)
        ret_c, sf_l, sb_l = _retention(l, z, dec, ret_gn, latent=False)
        ret_l, _, _ = _retention(l, z, dec, ret_gn, latent=True, rope=rope_r,
                                 s0f=state_ret_fwd, s0b=state_ret_bwd)
        x, h2, eid, wts, rank, cnt = _merge(l, x, att_c, att_l, ret_c, ret_l, z["cm"], z, mods, norm2,
                                            wa, wr, wm, wo, w_router_t, b_router)
        plan = _dispatch_plan(eid, rank, cnt[:, 0].astype(jnp.int32))
        moe_out = _moe(l, h2, plan, w_gate, w_up, w_down)
        wcol = wts.T
        ks.append(k_l.reshape(BATCH, SEQ, ATT_KV))
        vs.append(v_l.reshape(BATCH, SEQ, ATT_KV))
        sfs.append(sf_l)
        sbs.append(sb_l)
    y_ctx, y_lat = _final_combine(DEPTH - 1, x, moe_out, wcol, mods)

    y_prompt = y_ctx.reshape(BATCH, SEQ, D_MODEL)
    y_sample = y_lat.reshape(DEC_BATCH, DEC_SEQ, D_MODEL)
    new_k = jnp.stack(ks, axis=1).reshape(BATCH, DEPTH, SEQ, N_KV_HEADS, HEAD_DIM)
    new_v = jnp.stack(vs, axis=1).reshape(BATCH, DEPTH, SEQ, N_KV_HEADS, HEAD_DIM)
    new_sf = jnp.stack(sfs, axis=1)
    new_sb = jnp.stack(sbs, axis=1)
    return (y_prompt, y_sample, new_k, new_v, new_sf, new_sb)
```

```python
import functools

import numpy as np
import jax
import jax.numpy as jnp
from jax import lax
from jax.experimental import pallas as pl
from jax.experimental.pallas import tpu as pltpu

F32 = jnp.float32
BF16 = jnp.bfloat16

D_MODEL = 1024
BATCH = 32
SEQ = 256
DEPTH = 4
DEC_BATCH = 2
DEC_SEQ = 1024
PAST_LEN = 256
GRID_W = 64
BLOCK = 128
ROPE_BASE = 10000.0
EPS = 1e-6
N_Q_HEADS = 8
N_KV_HEADS = 2
HEAD_DIM = 64
ATT_Q = N_Q_HEADS * HEAD_DIM
ATT_KV = N_KV_HEADS * HEAD_DIM
N_RET_HEADS = 4
RET_DK = 128
RET_DV = 128
RET_W = N_RET_HEADS * RET_DK
N_MLP_GROUPS = 4
MLP_W = 512
N_MOD = 6
N_EXPERTS = 16
N_EXPERT_GROUPS = 4
EXPERTS_PER_GROUP = N_EXPERTS // N_EXPERT_GROUPS
TOP_K = 2
EXPERT_FF = 512
MOE_BLOCK = 128

T_CTX = BATCH * SEQ
T_LAT = DEC_BATCH * DEC_SEQ
T_ALL = T_CTX + T_LAT
N_ASSIGN = T_ALL * TOP_K
N_MOE_BLOCKS = -(-N_ASSIGN // MOE_BLOCK) + N_EXPERTS
N_SLOTS = N_MOE_BLOCKS * MOE_BLOCK

ROW_TILE = 512
V7X_VMEM_LIMIT_BYTES = 56 * 1024 * 1024

Z_SPLITS = (("qa", ATT_Q), ("kv", 2 * ATT_KV), ("qr", RET_W), ("kr", RET_W), ("vr", RET_W), ("gr", RET_W),
            ("mu", MLP_W), ("mv", MLP_W), ("ga", D_MODEL), ("gb", D_MODEL), ("gc", D_MODEL))
DOT_CHUNK = 512
Z_OUTPUTS = tuple((n, w) for n, w in Z_SPLITS if n not in ("mu", "mv")) + (("cm", MLP_W),)


def _params(sem, vmem=None):
    return pltpu.CompilerParams(dimension_semantics=sem, vmem_limit_bytes=vmem)


def _rms(x, g):
    return x * lax.rsqrt(jnp.mean(x * x, axis=-1, keepdims=True) + EPS) * g


def _sigmoid(x):
    return jax.nn.sigmoid(x)


TILE_ROWS = D_MODEL // 128


def _store_row_tiles(ref, val):
    rows = val.shape[0]
    for c in range(TILE_ROWS):
        ref[pl.ds(c, rows, stride=TILE_ROWS), :] = val[:, c * 128:(c + 1) * 128]


def _load_row_tiles(ref):
    rows = ref.shape[0] // TILE_ROWS
    return jnp.concatenate([ref[pl.ds(c, rows, stride=TILE_ROWS), :] for c in range(TILE_ROWS)], axis=1)


def _mod_row(i):
    first_lat = T_CTX // ROW_TILE
    per_batch = DEC_SEQ // ROW_TILE
    return jnp.where(i < first_lat, 0, 1 + (i - first_lat) // per_batch)


def _mod_kernel(cond_ref, w_ref, b_ref, o_ref):
    c = cond_ref[...]
    s = c * _sigmoid(c)
    o_ref[...] = jnp.dot(s, w_ref[...], preferred_element_type=F32,
                         precision=lax.Precision.HIGHEST) + b_ref[...]


def _modulation(cond, w_mod, b_mod):
    tn = D_MODEL
    out = pl.pallas_call(
        _mod_kernel,
        grid=(DEPTH, N_MOD),
        in_specs=[pl.BlockSpec((8, D_MODEL), lambda l, j: (0, 0)),
                  pl.BlockSpec((None, D_MODEL, tn), lambda l, j: (l, 0, j)),
                  pl.BlockSpec((None, 1, tn), lambda l, j: (l, 0, j))],
        out_specs=pl.BlockSpec((None, 8, tn), lambda l, j: (l, 0, j)),
        out_shape=jax.ShapeDtypeStruct((DEPTH, 8, N_MOD * D_MODEL), F32),
        compiler_params=_params(("arbitrary", "arbitrary")),
        name="modulation",
    )(cond, w_mod, b_mod.reshape(DEPTH, 1, N_MOD * D_MODEL))
    return out.reshape(DEPTH, 8, N_MOD, D_MODEL)


def _in_kernel(*refs, combine):
    it = iter(refs)
    if combine:
        dest_ref = next(it)
    x_ref = next(it)
    if combine:
        y_hbm, wc_ref, pmod_ref = next(it), next(it), next(it)
    mod_ref, g_ref, w_ref, gn_ref, ws_ref, bs_ref = (next(it) for _ in range(6))
    if combine:
        xo_ref = next(it)
    outs = {name: next(it) for name, _ in Z_OUTPUTS}
    if combine:
        ybuf, sem = next(it), next(it)

    x = x_ref[...]
    if combine:
        wc = wc_ref[...]
        y0, y1 = _gathered_expert_rows(dest_ref, y_hbm, ybuf, sem)
        x = x + pmod_ref[5:6, :] * (wc[:, 0:1] * y0 + wc[:, 1:2] * y1)
        xo_ref[...] = x
    h = (_rms(x, g_ref[...]) * (1.0 + mod_ref[1:2, :]) + mod_ref[0:1, :]).astype(BF16)
    col = 0
    kept = {}
    for name, width in Z_SPLITS:
        for c in range(0, width, DOT_CHUNK):
            cw = min(DOT_CHUNK, width - c)
            r = jnp.dot(h, w_ref[:, col + c:col + c + cw], preferred_element_type=F32)
            if name in ("mu", "mv"):
                kept[name] = r
            elif name in ("ga", "gb", "gc"):
                outs[name][:, c:c + cw] = _sigmoid(r).astype(BF16)
            elif name == "gr":
                outs[name][:, c:c + cw] = (r * _sigmoid(r)).astype(BF16)
            else:
                outs[name][:, c:c + cw] = r.astype(BF16)
        col += width

    u = jax.nn.gelu(kept["mu"], approximate=True)
    vn = _rms(jax.nn.gelu(kept["mv"], approximate=True), gn_ref[...]).astype(BF16)
    gc = MLP_W // N_MLP_GROUPS
    for n in range(ROW_TILE // BLOCK):
        rows = slice(n * BLOCK, (n + 1) * BLOCK)
        for g in range(N_MLP_GROUPS):
            cols = slice(g * gc, (g + 1) * gc)
            f = jnp.dot(ws_ref[g], vn[rows, cols], preferred_element_type=F32) + bs_ref[g]
            outs["cm"][rows, cols] = (u[rows, cols] * f).astype(BF16)
    if combine:
        _finish_expert_rows(y_hbm, ybuf, sem)


def _input_projection(l, x, mods, norm1, w_in_bf, mlp_norm, w_spatial_bf, b_spatial,
                      moe_out=None, dest_rows=None, wcol=None):
    combine = moe_out is not None
    tm = ROW_TILE
    nt = T_ALL // tm
    row = lambda i, *_: (i, 0)
    mod_spec = lambda ll: pl.BlockSpec((None, None, N_MOD, D_MODEL), lambda i, *_: (ll, _mod_row(i), 0, 0))
    in_specs = [pl.BlockSpec((tm, D_MODEL), row)]
    args = [x]
    if combine:
        in_specs += [pl.BlockSpec(memory_space=pl.ANY),
                     pl.BlockSpec((tm, TOP_K), row),
                     mod_spec(l - 1)]
        args = [dest_rows, x, moe_out, wcol, mods]
    in_specs += [mod_spec(l),
                 pl.BlockSpec((None, 1, D_MODEL), lambda i, *_: (l, 0, 0)),
                 pl.BlockSpec((None, D_MODEL, w_in_bf.shape[-1]), lambda i, *_: (l, 0, 0),
                              pipeline_mode=pl.Buffered(1)),
                 pl.BlockSpec((None, 1, MLP_W), lambda i, *_: (l, 0, 0)),
                 pl.BlockSpec((None, N_MLP_GROUPS, BLOCK, BLOCK), lambda i, *_: (l, 0, 0, 0)),
                 pl.BlockSpec((None, N_MLP_GROUPS, BLOCK, 1), lambda i, *_: (l, 0, 0, 0))]
    args += [mods, norm1.reshape(DEPTH, 1, D_MODEL), w_in_bf, mlp_norm.reshape(DEPTH, 1, MLP_W), w_spatial_bf,
             b_spatial.reshape(DEPTH, N_MLP_GROUPS, BLOCK, 1)]
    out_specs, out_shape = [], []
    if combine:
        out_specs.append(pl.BlockSpec((tm, D_MODEL), row))
        out_shape.append(jax.ShapeDtypeStruct((T_ALL, D_MODEL), F32))
    for _, width in Z_OUTPUTS:
        out_specs.append(pl.BlockSpec((tm, width), row))
        out_shape.append(jax.ShapeDtypeStruct((T_ALL, width), BF16))
    grid_spec = pltpu.PrefetchScalarGridSpec(
        num_scalar_prefetch=1 if combine else 0,
        grid=(nt,),
        in_specs=in_specs, out_specs=out_specs,
        scratch_shapes=_expert_rows_scratch() if combine else [])
    res = pl.pallas_call(
        functools.partial(_in_kernel, combine=combine),
        grid_spec=grid_spec, out_shape=out_shape,
        compiler_params=_params(("arbitrary",), V7X_VMEM_LIMIT_BYTES),
        name="input_projection",
    )(*args)
    if combine:
        x, res = res[0], res[1:]
    return x, {name: r for (name, _), r in zip(Z_OUTPUTS, res)}


def _rope_tables(rows, dim, reps):
    r = jnp.repeat(jnp.arange(rows, dtype=F32), GRID_W)
    col = jnp.tile(jnp.arange(GRID_W, dtype=F32), rows)
    half = dim // 2
    inv = ROPE_BASE ** (-jnp.arange(0, half, 2, dtype=F32) / half)
    ar = r[:, None] * inv
    ac = col[:, None] * inv
    ang = jnp.concatenate([ar, ar, ac, ac], axis=-1)
    cos, sin = jnp.cos(ang), jnp.sin(ang)
    first = (jnp.arange(dim) % (dim // 2)) < (dim // 4)
    sin_up = jnp.where(first, -sin, 0.0)
    sin_dn = jnp.where(first, 0.0, sin)
    t = lambda a: jnp.tile(a, (1, reps))
    return t(cos), t(sin_up), t(sin_dn)


def _rope(x, cos, sin_up, sin_dn, quarter):
    w = x.shape[-1]
    return x * cos + pltpu.roll(x, w - quarter, 1) * sin_up + pltpu.roll(x, quarter, 1) * sin_dn


def _attn_kernel(*refs, sk_new, use_rope, has_cache, emit_kv):
    it = iter(refs)
    q_ref, kv_ref, gq_ref, gk_ref, bdq_ref, bdk_ref = (next(it) for _ in range(6))
    if use_rope:
        cq_ref, suq_ref, sdq_ref, ck_ref, suk_ref, sdk_ref = (next(it) for _ in range(6))
    if has_cache:
        kc_ref, vc_ref = next(it), next(it)
    o_ref = next(it)
    if emit_kv:
        nk_ref, nv_ref = next(it), next(it)
    kd_scr, vd_scr = next(it), next(it)

    lo = lax.broadcasted_iota(jnp.int32, (1, 2 * HEAD_DIM), 1) < HEAD_DIM

    def dup_halves(a):
        r = pltpu.roll(a, HEAD_DIM, 1)
        return jnp.where(lo, a, r), jnp.where(lo, r, a)

    @pl.when(pl.program_id(1) == 0)
    def _prep():
        kv = kv_ref[...].astype(F32)
        k, v = kv[:, :ATT_KV], kv[:, ATT_KV:]
        ms = jnp.dot((k * k).astype(BF16), bdk_ref[...], preferred_element_type=F32)
        kn = k * lax.rsqrt(ms + EPS) * gk_ref[...]
        if emit_kv:
            nk_ref[...] = kn
            nv_ref[...] = v
        if use_rope:
            kn = _rope(kn, ck_ref[...], suk_ref[...], sdk_ref[...], HEAD_DIM // 4)
        k0, k1 = dup_halves(kn)
        v0, v1 = dup_halves(v)
        kd_scr[0, 0:sk_new, :] = k0.astype(BF16)
        kd_scr[1, 0:sk_new, :] = k1.astype(BF16)
        vd_scr[0, 0:sk_new, :] = v0.astype(BF16)
        vd_scr[1, 0:sk_new, :] = v1.astype(BF16)
        if has_cache:
            c0, c1 = dup_halves(kc_ref[...])
            d0, d1 = dup_halves(vc_ref[...])
            kd_scr[0, sk_new:, :] = c0.astype(BF16)
            kd_scr[1, sk_new:, :] = c1.astype(BF16)
            vd_scr[0, sk_new:, :] = d0.astype(BF16)
            vd_scr[1, sk_new:, :] = d1.astype(BF16)

    q = q_ref[...].astype(F32)
    ms = jnp.dot((q * q).astype(BF16), bdq_ref[...], preferred_element_type=F32)
    qn = q * lax.rsqrt(ms + EPS) * gq_ref[...]
    if use_rope:
        qn = _rope(qn, cq_ref[...], suq_ref[...], sdq_ref[...], HEAD_DIM // 4)
    qn = qn * (HEAD_DIM ** -0.5)
    heads_per_kv = N_Q_HEADS // N_KV_HEADS
    for j in range(N_Q_HEADS // 2):
        grp = (2 * j) // heads_per_kv
        kd, vd = kd_scr[grp], vd_scr[grp]
        qp = qn[:, 2 * HEAD_DIM * j:2 * HEAD_DIM * (j + 1)]
        halves = []
        for qm in (jnp.where(lo, qp, 0.0), jnp.where(lo, 0.0, qp)):
            s = lax.dot_general(qm.astype(BF16), kd, (((1,), (1,)), ((), ())), preferred_element_type=F32)
            e = jnp.exp(s - jnp.max(s, axis=-1, keepdims=True))
            den = jnp.sum(e, axis=-1, keepdims=True)
            halves.append(jnp.dot(e.astype(BF16), vd, preferred_element_type=F32) / den)
        o_ref[:, 2 * HEAD_DIM * j:2 * HEAD_DIM * (j + 1)] = jnp.where(lo, halves[0], halves[1]).astype(BF16)


def _block_diag_mean(width, group):
    idx = np.arange(width) // group
    return jnp.asarray((idx[:, None] == idx[None, :]).astype(np.float32) / group, dtype=BF16)


def _attention(l, z, q_norm, k_norm, *, latent, rope=None, cache_k=None, cache_v=None):
    if latent:
        nb, s, tq, row0 = DEC_BATCH, DEC_SEQ, 256, T_CTX
    else:
        nb, s, tq, row0 = BATCH, SEQ, SEQ, 0
    nq = s // tq
    sk = s + (PAST_LEN if latent else 0)
    qrow = lambda b, qi: (row0 // tq + b * nq + qi, 0)
    krow = lambda b, qi: (row0 // s + b, 0)
    const = lambda b, qi: (0, 0)
    lrow = lambda b, qi: (l, 0, 0)
    in_specs = [pl.BlockSpec((tq, ATT_Q), qrow),
                pl.BlockSpec((s, 2 * ATT_KV), krow),
                pl.BlockSpec((None, 1, ATT_Q), lrow),
                pl.BlockSpec((None, 1, ATT_KV), lrow),
                pl.BlockSpec((ATT_Q, ATT_Q), const),
                pl.BlockSpec((ATT_KV, ATT_KV), const)]
    args = [z["qa"], z["kv"],
            jnp.tile(q_norm, (1, N_Q_HEADS)).reshape(DEPTH, 1, ATT_Q),
            jnp.tile(k_norm, (1, N_KV_HEADS)).reshape(DEPTH, 1, ATT_KV),
            _block_diag_mean(ATT_Q, HEAD_DIM), _block_diag_mean(ATT_KV, HEAD_DIM)]
    if latent:
        cq, suq, sdq, ck, suk, sdk = rope
        in_specs += [pl.BlockSpec((tq, ATT_Q), lambda b, qi: (qi, 0))] * 3
        in_specs += [pl.BlockSpec((s, ATT_KV), const)] * 3
        in_specs += [pl.BlockSpec((None, None, PAST_LEN, ATT_KV), lambda b, qi: (b, l, 0, 0))] * 2
        args += [cq, suq, sdq, ck, suk, sdk,
                 cache_k.reshape(DEC_BATCH, DEPTH, PAST_LEN, ATT_KV),
                 cache_v.reshape(DEC_BATCH, DEPTH, PAST_LEN, ATT_KV)]
    rows = nb * s
    out_specs = [pl.BlockSpec((tq, ATT_Q), lambda b, qi: (b * nq + qi, 0))]
    out_shape = [jax.ShapeDtypeStruct((rows, ATT_Q), BF16)]
    if not latent:
        out_specs += [pl.BlockSpec((s, ATT_KV), lambda b, qi: (b, 0))] * 2
        out_shape += [jax.ShapeDtypeStruct((rows, ATT_KV), F32)] * 2
    return pl.pallas_call(
        functools.partial(_attn_kernel, sk_new=s, use_rope=latent, has_cache=latent, emit_kv=not latent),
        grid=(nb, nq),
        in_specs=in_specs, out_specs=out_specs, out_shape=out_shape,
        scratch_shapes=[pltpu.VMEM((N_KV_HEADS, sk, 2 * HEAD_DIM), BF16),
                        pltpu.VMEM((N_KV_HEADS, sk, 2 * HEAD_DIM), BF16)],
        compiler_params=_params(("arbitrary", "arbitrary"), V7X_VMEM_LIMIT_BYTES),
        name="attention_latent" if latent else "attention_context",
    )(*args)


def _ret_kernel(*refs, s, heads, use_rope, has_state):
    it = iter(refs)
    q_ref, k_ref, v_ref, g_ref, dec_ref, gn_ref = (next(it) for _ in range(6))
    rope = (next(it)[...], next(it)[...], next(it)[...]) if use_rope else None
    s0f_ref, s0b_ref = (next(it), next(it)) if has_state else (None, None)
    o_ref, sf_ref, sb_ref, oacc = next(it), next(it), next(it), next(it)
    for hh in range(heads):
        cols = slice(hh * RET_DK, (hh + 1) * RET_DK)
        _ret_head(q_ref.at[:, cols], k_ref.at[:, cols], v_ref.at[:, cols], g_ref.at[:, cols],
                  dec_ref.at[:, hh], gn_ref.at[hh], rope,
                  s0f_ref.at[hh] if has_state else None, s0b_ref.at[hh] if has_state else None,
                  o_ref.at[:, cols], sf_ref.at[hh], sb_ref.at[hh], oacc.at[hh], s)


def _ret_head(q_ref, k_ref, v_ref, g_ref, dec_ref, gn_ref, rope, s0f_ref, s0b_ref, o_ref, sf_ref, sb_ref, oacc, s):
    use_rope, has_state = rope is not None, s0f_ref is not None
    n_chunks = s // BLOCK
    lgf = -jnp.exp(dec_ref[0])
    lgb = -jnp.exp(dec_ref[1])
    lgf1, lgb1 = lgf[:, 0:1], lgb[:, 0:1]
    diff = (lax.broadcasted_iota(jnp.int32, (BLOCK, BLOCK), 0)
            - lax.broadcasted_iota(jnp.int32, (BLOCK, BLOCK), 1)).astype(F32)
    dsum = (jnp.where(diff >= 0, jnp.exp(diff * lgf), 0.0)
            + jnp.where(diff <= 0, jnp.exp(-diff * lgb), 0.0))
    ic = lax.broadcasted_iota(jnp.int32, (BLOCK, 1), 0).astype(F32)
    qdf, kdf, cdf = jnp.exp((ic + 1.0) * lgf1), jnp.exp((BLOCK - 1.0 - ic) * lgf1), jnp.exp(BLOCK * lgf1)
    qdb, kdb, cdb = jnp.exp((BLOCK - ic) * lgb1), jnp.exp(ic * lgb1), jnp.exp(BLOCK * lgb1)

    q = q_ref[...].astype(F32)
    k = k_ref[...].astype(F32) * (RET_DK ** -0.5)
    if use_rope:
        q = _rope(q, *rope, RET_DK // 4)
        k = _rope(k, *rope, RET_DK // 4)
    v = v_ref[...]

    def rows(a, n):
        return a[n * BLOCK:(n + 1) * BLOCK]

    def state_step(state, kn, kdec, vn, cdec):
        kd_t = jnp.transpose(kn * kdec).astype(BF16)
        return cdec * state + jnp.dot(kd_t, vn, preferred_element_type=F32)

    state = s0f_ref[...] if has_state else jnp.zeros((RET_DK, RET_DV), F32)
    for n in range(n_chunks):
        qn, kn, vn = rows(q, n), rows(k, n), rows(v, n)
        inner = lax.dot_general(qn.astype(BF16), kn.astype(BF16), (((1,), (1,)), ((), ())),
                                preferred_element_type=F32) * dsum
        o = (jnp.dot(inner.astype(BF16), vn, preferred_element_type=F32)
             + jnp.dot((qn * qdf).astype(BF16), state.astype(BF16), preferred_element_type=F32))
        state = state_step(state, kn, kdf, vn, cdf)
        oacc[n * BLOCK:(n + 1) * BLOCK, :] = o
    sf_ref[...] = state

    state = s0b_ref[...] if has_state else jnp.zeros((RET_DK, RET_DV), F32)
    for n in reversed(range(n_chunks)):
        qn, kn, vn = rows(q, n), rows(k, n), rows(v, n)
        o = rows(oacc, n) + jnp.dot((qn * qdb).astype(BF16), state.astype(BF16), preferred_element_type=F32)
        state = state_step(state, kn, kdb, vn, cdb)
        mu = jnp.mean(o, axis=-1, keepdims=True)
        var = jnp.mean(jnp.square(o - mu), axis=-1, keepdims=True)
        y = (o - mu) * lax.rsqrt(var + EPS) * gn_ref[...]
        gate = g_ref[n * BLOCK:(n + 1) * BLOCK, :].astype(F32)
        o_ref[n * BLOCK:(n + 1) * BLOCK, :] = (y * gate).astype(BF16)
    sb_ref[...] = state


def _retention(l, z, dec, ret_gn, *, latent, rope=None, s0f=None, s0b=None):
    if latent:
        nb, s, row0, heads = DEC_BATCH, DEC_SEQ, T_CTX, 1
    else:
        nb, s, row0, heads = BATCH, SEQ, 0, N_RET_HEADS
    width = heads * RET_DK
    hrow = lambda b, h: (row0 // s + b, h)
    in_specs = [pl.BlockSpec((s, width), hrow)] * 4
    in_specs += [pl.BlockSpec((None, 2, heads, 1, RET_DK), lambda b, h: (l, 0, h, 0, 0)),
                 pl.BlockSpec((None, heads, 1, RET_DV), lambda b, h: (l, h, 0, 0))]
    args = [z["qr"], z["kr"], z["vr"], z["gr"], dec, ret_gn.reshape(DEPTH, N_RET_HEADS, 1, RET_DV)]
    if latent:
        in_specs += [pl.BlockSpec((s, RET_DK), lambda b, h: (0, 0))] * 3
        in_specs += [pl.BlockSpec((None, None, heads, RET_DK, RET_DV), lambda b, h: (b, l, h, 0, 0))] * 2
        args += [*rope, s0f, s0b]
    st_spec = pl.BlockSpec((None, heads, RET_DK, RET_DV), lambda b, h: (b, h, 0, 0))
    st_shape = jax.ShapeDtypeStruct((nb, N_RET_HEADS, RET_DK, RET_DV), F32)
    return pl.pallas_call(
        functools.partial(_ret_kernel, s=s, heads=heads, use_rope=latent, has_state=latent),
        grid=(nb, N_RET_HEADS // heads),
        in_specs=in_specs,
        out_specs=[pl.BlockSpec((s, width), lambda b, h: (b, h)), st_spec, st_spec],
        out_shape=[jax.ShapeDtypeStruct((nb * s, RET_W), BF16), st_shape, st_shape],
        scratch_shapes=[pltpu.VMEM((heads, s, RET_DV), F32)],
        compiler_params=_params(("arbitrary", "arbitrary")),
        name="retention_latent" if latent else "retention_context",
    )(*args)


def _first_max(vals):
    best, idx = vals[0], jnp.zeros(vals[0].shape, jnp.int32)
    for j in range(1, len(vals)):
        upd = vals[j] > best
        best = jnp.where(upd, vals[j], best)
        idx = jnp.where(upd, j, idx)
    return best, idx


def _pick(idx, vals):
    out = vals[-1]
    for j in range(len(vals) - 2, -1, -1):
        out = jnp.where(idx == j, vals[j], out)
    return out


def _merge_kernel(x_ref, attc_ref, attl_ref, retc_ref, retl_ref, cm_ref, ga_ref, gb_ref, gc_ref, mod_ref, g2_ref,
                  wa_ref, wr_ref, wm_ref, wo_ref, wrt_ref, br_ref, tri_ref,
                  x1_ref, h2_ref, eid_ref, wts_ref, rank_ref, cnt_ref, cnt_scr):
    @pl.when(pl.program_id(0) == 0)
    def _zero_counts():
        cnt_scr[...] = jnp.zeros_like(cnt_scr)

    is_lat = pl.program_id(0) >= T_CTX // ROW_TILE
    att = jnp.where(is_lat, attl_ref[...], attc_ref[...])
    ret = jnp.where(is_lat, retl_ref[...], retc_ref[...])
    gate = lambda r: r[...].astype(F32)
    dot = lambda a, b: jnp.dot(a, b, preferred_element_type=F32)
    mix = (gate(ga_ref) * dot(att, wa_ref[...])
           + gate(gb_ref) * dot(ret, wr_ref[...])
           + gate(gc_ref) * dot(cm_ref[...], wm_ref[...]))
    x1 = x_ref[...] + mod_ref[2:3, :] * dot(mix.astype(BF16), wo_ref[...])
    x1_ref[...] = x1
    h2 = _rms(x1, g2_ref[...]) * (1.0 + mod_ref[4:5, :]) + mod_ref[3:4, :]
    _store_row_tiles(h2_ref, h2)

    logits = lax.dot_general(wrt_ref[...], h2, (((1,), (1,)), ((), ())), preferred_element_type=F32,
                             precision=lax.Precision.HIGHEST)
    score = _sigmoid(logits)
    sel = score + br_ref[...]
    sel_rows = [sel[e:e + 1, :] for e in range(N_EXPERTS)]
    score_rows = [score[e:e + 1, :] for e in range(N_EXPERTS)]
    group_scores = []
    for g in range(N_EXPERT_GROUPS):
        v = sel_rows[g * EXPERTS_PER_GROUP:(g + 1) * EXPERTS_PER_GROUP]
        pair_sums = [v[a] + v[b] for a in range(EXPERTS_PER_GROUP) for b in range(a + 1, EXPERTS_PER_GROUP)]
        group_scores.append(functools.reduce(jnp.maximum, pair_sums))
    _, gidx = _first_max(group_scores)
    in_sel = [_pick(gidx, [sel_rows[g * EXPERTS_PER_GROUP + j] for g in range(N_EXPERT_GROUPS)])
              for j in range(EXPERTS_PER_GROUP)]
    in_score = [_pick(gidx, [score_rows[g * EXPERTS_PER_GROUP + j] for g in range(N_EXPERT_GROUPS)])
                for j in range(EXPERTS_PER_GROUP)]
    _, loc0 = _first_max(in_sel)
    _, loc1 = _first_max([jnp.where(loc0 == j, -jnp.inf, in_sel[j]) for j in range(EXPERTS_PER_GROUP)])
    w0, w1 = _pick(loc0, in_score), _pick(loc1, in_score)
    den = w0 + w1
    e0 = gidx * EXPERTS_PER_GROUP + loc0
    e1 = gidx * EXPERTS_PER_GROUP + loc1
    eid_ref[0:1, :] = e0
    eid_ref[1:2, :] = e1
    wts_ref[0:1, :] = w0 / den
    wts_ref[1:2, :] = w1 / den

    eio = lax.broadcasted_iota(jnp.int32, score.shape, 0)
    oh0 = (eio == e0).astype(F32)
    oh1 = (eio == e1).astype(F32)
    both = oh0 + oh1
    before = jnp.dot(both.astype(BF16), tri_ref[...], preferred_element_type=F32) + cnt_scr[...]
    rank_ref[0:1, :] = jnp.sum(oh0 * before, axis=0, keepdims=True).astype(jnp.int32)
    rank_ref[1:2, :] = jnp.sum(oh1 * before, axis=0, keepdims=True).astype(jnp.int32)
    cnt_scr[...] = cnt_scr[...] + jnp.sum(both, axis=1, keepdims=True)
    cnt_ref[...] = cnt_scr[...]


def _merge(l, x, att_c, att_l, ret_c, ret_l, cm, z, mods, norm2, wa, wr, wm, wo, w_router_t, b_router):
    tm = ROW_TILE
    row = lambda i: (i, 0)
    n_ctx = T_CTX // tm
    ctx_row = lambda i: (jnp.minimum(i, n_ctx - 1), 0)
    lat_row = lambda i: (jnp.maximum(i - n_ctx, 0), 0)
    lw = lambda k, n: pl.BlockSpec((None, k, n), lambda i: (l, 0, 0))
    in_specs = [pl.BlockSpec((tm, D_MODEL), row),
                pl.BlockSpec((tm, ATT_Q), ctx_row), pl.BlockSpec((tm, ATT_Q), lat_row),
                pl.BlockSpec((tm, RET_W), ctx_row), pl.BlockSpec((tm, RET_W), lat_row),
                pl.BlockSpec((tm, MLP_W), row),
                pl.BlockSpec((tm, D_MODEL), row), pl.BlockSpec((tm, D_MODEL), row), pl.BlockSpec((tm, D_MODEL), row),
                pl.BlockSpec((None, None, N_MOD, D_MODEL), lambda i: (l, _mod_row(i), 0, 0)),
                pl.BlockSpec((None, 1, D_MODEL), lambda i: (l, 0, 0)),
                lw(ATT_Q, D_MODEL), lw(RET_W, D_MODEL), lw(MLP_W, D_MODEL), lw(D_MODEL, D_MODEL),
                pl.BlockSpec((N_EXPERTS, D_MODEL), lambda i: (0, 0)),
                pl.BlockSpec((N_EXPERTS, 1), lambda i: (0, 0)),
                pl.BlockSpec((tm, tm), lambda i: (0, 0))]
    lane = lambda i: (0, i)
    tri = jnp.asarray(np.triu(np.ones((tm, tm), np.float32), k=1), dtype=BF16)
    return pl.pallas_call(
        _merge_kernel,
        grid=(T_ALL // tm,),
        in_specs=in_specs,
        out_specs=[pl.BlockSpec((tm, D_MODEL), row), pl.BlockSpec((tm * TILE_ROWS, 128), row),
                   pl.BlockSpec((TOP_K, tm), lane), pl.BlockSpec((TOP_K, tm), lane),
                   pl.BlockSpec((TOP_K, tm), lane), pl.BlockSpec((N_EXPERTS, 1), lambda i: (0, 0))],
        out_shape=[jax.ShapeDtypeStruct((T_ALL, D_MODEL), F32), jax.ShapeDtypeStruct((T_ALL * TILE_ROWS, 128), F32),
                   jax.ShapeDtypeStruct((TOP_K, T_ALL), jnp.int32), jax.ShapeDtypeStruct((TOP_K, T_ALL), F32),
                   jax.ShapeDtypeStruct((TOP_K, T_ALL), jnp.int32), jax.ShapeDtypeStruct((N_EXPERTS, 1), F32)],
        scratch_shapes=[pltpu.VMEM((N_EXPERTS, 1), F32)],
        compiler_params=_params(("arbitrary",), V7X_VMEM_LIMIT_BYTES),
        name="merge_router",
    )(x, att_c, att_l, ret_c, ret_l, cm, z["ga"], z["gb"], z["gc"], mods, norm2.reshape(DEPTH, 1, D_MODEL),
      wa, wr, wm, wo, w_router_t, b_router.reshape(N_EXPERTS, 1), tri)


def _dispatch_plan(eid, rank, counts):
    expert = jnp.arange(N_EXPERTS, dtype=jnp.int32)
    padded = (counts + MOE_BLOCK - 1) // MOE_BLOCK * MOE_BLOCK
    pend = jnp.cumsum(padded)
    pstart = pend - padded
    dest = rank + jnp.sum(jnp.where(eid[..., None] == expert, pstart, 0), axis=-1)
    block_start = jnp.arange(N_MOE_BLOCKS, dtype=jnp.int32) * MOE_BLOCK
    block_e = jnp.minimum(jnp.sum((pend[None, :] <= block_start[:, None]).astype(jnp.int32), axis=1),
                          N_EXPERTS - 1)
    seg_end = jnp.sum(jnp.where(block_e[:, None] == expert, pstart + counts, 0), axis=-1)
    n_valid = jnp.clip(seg_end - block_start, 0, MOE_BLOCK).astype(jnp.int32)
    return block_e, n_valid, dest.reshape(N_ASSIGN)


ROWS_PER_TRIP = 8


N_DMA_PRIORITIES = 2


def _for_rows(n, fn):
    full = lax.div(n, ROWS_PER_TRIP)

    def group(g, c):
        for u in range(ROWS_PER_TRIP):
            fn(g * ROWS_PER_TRIP + u, u % N_DMA_PRIORITIES)
        return c
    lax.fori_loop(0, full, group, 0)

    def single(r, c):
        fn(r, 0)
        return c
    lax.fori_loop(full * ROWS_PER_TRIP, n, single, 0)


def _moe_kernel(be_ref, nv_ref, dest_ref, h_hbm, wg_ref, wu_ref, wd_ref, o_ref,
                xbuf, wg_s, wu_s, wd_s, src_s, sem_in):
    i = pl.program_id(0)
    last = pl.num_programs(0) - 1

    def tile(first_row):
        return pl.ds(pl.multiple_of(first_row, TILE_ROWS), TILE_ROWS)

    def start_gather(blk):
        buf = lax.rem(blk, 2)

        def one(r, priority):
            pltpu.make_async_copy(h_hbm.at[tile(src_s[blk * MOE_BLOCK + r])], xbuf.at[buf, tile(r * TILE_ROWS)],
                                  sem_in.at[buf]).start(priority=priority)
        _for_rows(nv_ref[blk], one)

    def wait_gather(blk):
        buf, n = lax.rem(blk, 2), nv_ref[blk]
        group = ROWS_PER_TRIP * TILE_ROWS
        grouped = pl.multiple_of(lax.div(n, ROWS_PER_TRIP) * group, group)

        def copy(rows):
            return pltpu.make_async_copy(h_hbm.at[rows], xbuf.at[buf, rows], sem_in.at[buf])

        @pl.when(grouped > 0)
        def _():
            copy(pl.ds(0, grouped)).wait()

        def single(r, c):
            copy(pl.ds(0, TILE_ROWS)).wait()
            return c
        lax.fori_loop(lax.div(n, ROWS_PER_TRIP) * ROWS_PER_TRIP, n, single, 0)

    @pl.when(i == 0)
    def _init():
        for k in range(TOP_K):
            def put(g, c):
                for u in range(ROWS_PER_TRIP):
                    t = g * ROWS_PER_TRIP + u
                    src_s[dest_ref[k * T_ALL + t]] = t * TILE_ROWS
                return c
            lax.fori_loop(0, T_ALL // ROWS_PER_TRIP, put, 0)
        xbuf[...] = jnp.zeros_like(xbuf)
        start_gather(0)

    @pl.when(i < last)
    def _prefetch():
        start_gather(i + 1)

    @pl.when(nv_ref[i] > 0)
    def _block():
        buf = lax.rem(i, 2)
        prev = be_ref[jnp.maximum(i - 1, 0)]

        @pl.when((i == 0) | (be_ref[i] != prev))
        def _cast():
            wg_s[...] = wg_ref[...].astype(BF16)
            wu_s[...] = wu_ref[...].astype(BF16)
            wd_s[...] = wd_ref[...].astype(BF16)

        wait_gather(i)
        x = _load_row_tiles(xbuf.at[buf]).astype(BF16)
        a = jnp.dot(x, wg_s[...], preferred_element_type=F32)
        b = jnp.dot(x, wu_s[...], preferred_element_type=F32)
        mid = ((a * _sigmoid(a)) * b).astype(BF16)
        _store_row_tiles(o_ref, jnp.dot(mid, wd_s[...], preferred_element_type=F32))

    @pl.when(nv_ref[i] == 0)
    def _unused():
        o_ref[...] = jnp.zeros_like(o_ref)


def _moe(l, h2, plan, w_gate, w_up, w_down):
    block_e, n_valid, dest = plan
    wspec = lambda k, n: pl.BlockSpec((None, None, k, n), lambda i, be, nv, de: (l, be[i], 0, 0))
    grid_spec = pltpu.PrefetchScalarGridSpec(
        num_scalar_prefetch=3,
        grid=(N_MOE_BLOCKS,),
        in_specs=[pl.BlockSpec(memory_space=pl.ANY),
                  wspec(D_MODEL, EXPERT_FF), wspec(D_MODEL, EXPERT_FF), wspec(EXPERT_FF, D_MODEL)],
        out_specs=pl.BlockSpec((MOE_BLOCK * TILE_ROWS, 128), lambda i, be, nv, de: (i, 0)),
        scratch_shapes=[pltpu.VMEM((2, MOE_BLOCK * TILE_ROWS, 128), F32),
                        pltpu.VMEM((D_MODEL, EXPERT_FF), BF16), pltpu.VMEM((D_MODEL, EXPERT_FF), BF16),
                        pltpu.VMEM((EXPERT_FF, D_MODEL), BF16),
                        pltpu.SMEM((N_SLOTS,), jnp.int32),
                        pltpu.SemaphoreType.DMA((2,))])
    return pl.pallas_call(
        _moe_kernel,
        grid_spec=grid_spec,
        out_shape=jax.ShapeDtypeStruct((N_SLOTS * TILE_ROWS, 128), F32),
        compiler_params=_params(("arbitrary",), V7X_VMEM_LIMIT_BYTES),
        name="moe_experts",
    )(block_e, n_valid, dest, h2, w_gate, w_up, w_down)


def _start_expert_rows(dest_ref, y_hbm, ybuf, sem, tile_idx, slot):
    base = tile_idx * ROW_TILE
    for k in range(TOP_K):
        for t in range(ROW_TILE):
            first = pl.multiple_of(dest_ref[k * T_ALL + base + t], TILE_ROWS)
            pltpu.make_async_copy(y_hbm.at[pl.ds(first, TILE_ROWS)],
                                  ybuf.at[slot, k, pl.ds(t * TILE_ROWS, TILE_ROWS)],
                                  sem.at[slot]).start(priority=t % N_DMA_PRIORITIES)


def _wait_expert_rows(y_hbm, ybuf, sem, slot):
    for k in range(TOP_K):
        pltpu.make_async_copy(y_hbm.at[pl.ds(0, ROW_TILE * TILE_ROWS)], ybuf.at[slot, k], sem.at[slot]).wait()


def _gathered_expert_rows(dest_ref, y_hbm, ybuf, sem):
    i = pl.program_id(0)
    last = pl.num_programs(0) - 1
    slot = lax.rem(i, 2)

    @pl.when(i == 0)
    def _first():
        _start_expert_rows(dest_ref, y_hbm, ybuf, sem, 0, 0)

    _start_expert_rows(dest_ref, y_hbm, ybuf, sem, jnp.minimum(i + 1, last), 1 - slot)
    _wait_expert_rows(y_hbm, ybuf, sem, slot)
    return _load_row_tiles(ybuf.at[slot, 0]), _load_row_tiles(ybuf.at[slot, 1])


def _finish_expert_rows(y_hbm, ybuf, sem):
    i = pl.program_id(0)

    @pl.when(i == pl.num_programs(0) - 1)
    def _():
        _wait_expert_rows(y_hbm, ybuf, sem, 1 - lax.rem(i, 2))


def _combine_kernel(dest_ref, x_ref, y_hbm, wc_ref, mod_ref, oc_ref, ol_ref, ybuf, sem):
    wc = wc_ref[...]
    y0, y1 = _gathered_expert_rows(dest_ref, y_hbm, ybuf, sem)
    x = x_ref[...] + mod_ref[5:6, :] * (wc[:, 0:1] * y0 + wc[:, 1:2] * y1)
    _finish_expert_rows(y_hbm, ybuf, sem)
    is_lat = pl.program_id(0) >= T_CTX // ROW_TILE

    @pl.when(jnp.logical_not(is_lat))
    def _():
        oc_ref[...] = x

    @pl.when(is_lat)
    def _():
        ol_ref[...] = x


def _expert_rows_scratch():
    return [pltpu.VMEM((2, TOP_K, ROW_TILE * TILE_ROWS, 128), F32), pltpu.SemaphoreType.DMA((2,))]


def _final_combine(l, x, moe_out, dest_rows, wcol, mods):
    tm = ROW_TILE
    nt = T_ALL // tm
    n_ctx = T_CTX // tm
    row = lambda i, de: (i, 0)
    grid_spec = pltpu.PrefetchScalarGridSpec(
        num_scalar_prefetch=1,
        grid=(nt,),
        in_specs=[pl.BlockSpec((tm, D_MODEL), row), pl.BlockSpec(memory_space=pl.ANY),
                  pl.BlockSpec((tm, TOP_K), row),
                  pl.BlockSpec((None, None, N_MOD, D_MODEL), lambda i, de: (l, _mod_row(i), 0, 0))],
        out_specs=[pl.BlockSpec((tm, D_MODEL), lambda i, de: (jnp.minimum(i, n_ctx - 1), 0)),
                   pl.BlockSpec((tm, D_MODEL), lambda i, de: (jnp.maximum(i - n_ctx, 0), 0))],
        scratch_shapes=_expert_rows_scratch())
    return pl.pallas_call(
        _combine_kernel,
        grid_spec=grid_spec,
        out_shape=[jax.ShapeDtypeStruct((T_CTX, D_MODEL), F32), jax.ShapeDtypeStruct((T_LAT, D_MODEL), F32)],
        compiler_params=_params(("arbitrary",)),
        name="final_combine",
    )(dest_rows, x, moe_out, wcol, mods)


def kernel(x_prompt, x_sample, cache_k, cache_v, state_ret_fwd, state_ret_bwd, c, c_ctx, w_mod, b_mod, norm1, norm2, w_in, q_norm, k_norm, ret_decay_fwd, ret_decay_bwd, ret_gn, mlp_norm, w_spatial, b_spatial, w_att_o, w_ret_o, w_mlp_o, w_out, w_router, b_router, w_gate, w_up, w_down):
    cond = jnp.zeros((8, D_MODEL), F32).at[0].set(c_ctx).at[1:1 + DEC_BATCH].set(c)
    mods = _modulation(cond, w_mod, b_mod)

    w_in_bf = w_in.astype(BF16)
    wa, wr, wm, wo = (w.astype(BF16) for w in (w_att_o, w_ret_o, w_mlp_o, w_out))
    ws_bf = w_spatial.astype(BF16)
    w_router_t = w_router.T
    dec = jnp.broadcast_to(jnp.stack([ret_decay_fwd, ret_decay_bwd], axis=1)[..., None, None],
                           (DEPTH, 2, N_RET_HEADS, 1, RET_DK))
    rows = DEC_SEQ // GRID_W
    rope_q = _rope_tables(rows, HEAD_DIM, N_Q_HEADS)
    rope_k = _rope_tables(rows, HEAD_DIM, N_KV_HEADS)
    rope_r = _rope_tables(rows, RET_DK, 1)

    x = jnp.concatenate([x_prompt.reshape(T_CTX, D_MODEL), x_sample.reshape(T_LAT, D_MODEL)], axis=0)
    moe_out, dest_rows, wcol = None, None, None
    ks, vs, sfs, sbs = [], [], [], []
    for l in range(DEPTH):
        x, z = _input_projection(l, x, mods, norm1, w_in_bf, mlp_norm, ws_bf, b_spatial, moe_out, dest_rows, wcol)

        att_c, k_l, v_l = _attention(l, z, q_norm, k_norm, latent=False)
        att_l, = _attention(l, z, q_norm, k_norm, latent=True, rope=rope_q + rope_k,
                            cache_k=cache_k, cache_v=cache_v)
        ret_c, sf_l, sb_l = _retention(l, z, dec, ret_gn, latent=False)
        ret_l, _, _ = _retention(l, z, dec, ret_gn, latent=True, rope=rope_r,
                                 s0f=state_ret_fwd, s0b=state_ret_bwd)
        x, h2, eid, wts, rank, cnt = _merge(l, x, att_c, att_l, ret_c, ret_l, z["cm"], z, mods, norm2,
                                            wa, wr, wm, wo, w_router_t, b_router)
        plan = _dispatch_plan(eid, rank, cnt[:, 0].astype(jnp.int32))
        moe_out = _moe(l, h2, plan, w_gate, w_up, w_down)
        dest_rows = plan[2] * TILE_ROWS
        wcol = wts.T
        ks.append(k_l.reshape(BATCH, SEQ, ATT_KV))
        vs.append(v_l.reshape(BATCH, SEQ, ATT_KV))
        sfs.append(sf_l)
        sbs.append(sb_l)
    y_ctx, y_lat = _final_combine(DEPTH - 1, x, moe_out, dest_rows, wcol, mods)

    y_prompt = y_ctx.reshape(BATCH, SEQ, D_MODEL)
    y_sample = y_lat.reshape(DEC_BATCH, DEC_SEQ, D_MODEL)
    new_k = jnp.stack(ks, axis=1).reshape(BATCH, DEPTH, SEQ, N_KV_HEADS, HEAD_DIM)
    new_v = jnp.stack(vs, axis=1).reshape(BATCH, DEPTH, SEQ, N_KV_HEADS, HEAD_DIM)
    new_sf = jnp.stack(sfs, axis=1)
    new_sb = jnp.stack(sbs, axis=1)
    return (y_prompt, y_sample, new_k, new_v, new_sf, new_sb)
```

```python
import functools

import numpy as np
import jax
import jax.numpy as jnp
from jax import lax
from jax.experimental import pallas as pl
from jax.experimental.pallas import tpu as pltpu

F32 = jnp.float32
BF16 = jnp.bfloat16

D_MODEL = 1024
BATCH = 32
SEQ = 256
DEPTH = 4
DEC_BATCH = 2
DEC_SEQ = 1024
PAST_LEN = 256
GRID_W = 64
BLOCK = 128
ROPE_BASE = 10000.0
EPS = 1e-6
N_Q_HEADS = 8
N_KV_HEADS = 2
HEAD_DIM = 64
ATT_Q = N_Q_HEADS * HEAD_DIM
ATT_KV = N_KV_HEADS * HEAD_DIM
N_RET_HEADS = 4
RET_DK = 128
RET_DV = 128
RET_W = N_RET_HEADS * RET_DK
N_MLP_GROUPS = 4
MLP_W = 512
N_MOD = 6
N_EXPERTS = 16
N_EXPERT_GROUPS = 4
EXPERTS_PER_GROUP = N_EXPERTS // N_EXPERT_GROUPS
TOP_K = 2
EXPERT_FF = 512
MOE_BLOCK = 128

T_CTX = BATCH * SEQ
T_LAT = DEC_BATCH * DEC_SEQ
T_ALL = T_CTX + T_LAT
N_ASSIGN = T_ALL * TOP_K
N_MOE_BLOCKS = -(-N_ASSIGN // MOE_BLOCK) + N_EXPERTS
N_SLOTS = N_MOE_BLOCKS * MOE_BLOCK

ROW_TILE = 512
V7X_VMEM_LIMIT_BYTES = 56 * 1024 * 1024
V7X_MOE_VMEM_LIMIT_BYTES = 60 * 1024 * 1024

Z_SPLITS = (("qa", ATT_Q), ("kv", 2 * ATT_KV), ("qr", RET_W), ("kr", RET_W), ("vr", RET_W), ("gr", RET_W),
            ("mu", MLP_W), ("mv", MLP_W), ("ga", D_MODEL), ("gb", D_MODEL), ("gc", D_MODEL))
DOT_CHUNK = 512
Z_OUTPUTS = tuple((n, w) for n, w in Z_SPLITS if n not in ("mu", "mv")) + (("cm", MLP_W),)


def _params(sem, vmem=None):
    return pltpu.CompilerParams(dimension_semantics=sem, vmem_limit_bytes=vmem)


def _rms(x, g):
    return x * lax.rsqrt(jnp.mean(x * x, axis=-1, keepdims=True) + EPS) * g


def _sigmoid(x):
    return jax.nn.sigmoid(x)


TILE_ROWS = D_MODEL // 128


def _store_row_tiles(ref, val):
    rows = val.shape[0]
    for c in range(TILE_ROWS):
        ref[pl.ds(c, rows, stride=TILE_ROWS), :] = val[:, c * 128:(c + 1) * 128]


def _load_row_tiles(ref):
    rows = ref.shape[0] // TILE_ROWS
    return jnp.concatenate([ref[pl.ds(c, rows, stride=TILE_ROWS), :] for c in range(TILE_ROWS)], axis=1)


def _mod_row(i):
    first_lat = T_CTX // ROW_TILE
    per_batch = DEC_SEQ // ROW_TILE
    return jnp.where(i < first_lat, 0, 1 + (i - first_lat) // per_batch)


def _mod_kernel(cond_ref, w_ref, b_ref, o_ref):
    c = cond_ref[...]
    s = c * _sigmoid(c)
    o_ref[...] = jnp.dot(s, w_ref[...], preferred_element_type=F32,
                         precision=lax.Precision.HIGHEST) + b_ref[...]


def _modulation(cond, w_mod, b_mod):
    tn = D_MODEL
    out = pl.pallas_call(
        _mod_kernel,
        grid=(DEPTH, N_MOD),
        in_specs=[pl.BlockSpec((8, D_MODEL), lambda l, j: (0, 0)),
                  pl.BlockSpec((None, D_MODEL, tn), lambda l, j: (l, 0, j)),
                  pl.BlockSpec((None, 1, tn), lambda l, j: (l, 0, j))],
        out_specs=pl.BlockSpec((None, 8, tn), lambda l, j: (l, 0, j)),
        out_shape=jax.ShapeDtypeStruct((DEPTH, 8, N_MOD * D_MODEL), F32),
        compiler_params=_params(("arbitrary", "arbitrary")),
        name="modulation",
    )(cond, w_mod, b_mod.reshape(DEPTH, 1, N_MOD * D_MODEL))
    return out.reshape(DEPTH, 8, N_MOD, D_MODEL)


def _in_kernel(*refs, combine):
    it = iter(refs)
    if combine:
        dest_ref = next(it)
    x_ref = next(it)
    if combine:
        y_hbm, wc_ref, pmod_ref = next(it), next(it), next(it)
    mod_ref, g_ref, w_ref, gn_ref, ws_ref, bs_ref = (next(it) for _ in range(6))
    if combine:
        xo_ref = next(it)
    outs = {name: next(it) for name, _ in Z_OUTPUTS}
    if combine:
        ybuf, sem = next(it), next(it)

    x = x_ref[...]
    if combine:
        wc = wc_ref[...]
        y0, y1 = _gathered_expert_rows(dest_ref, y_hbm, ybuf, sem)
        x = x + pmod_ref[5:6, :] * (wc[:, 0:1] * y0 + wc[:, 1:2] * y1)
        xo_ref[...] = x
    h = (_rms(x, g_ref[...]) * (1.0 + mod_ref[1:2, :]) + mod_ref[0:1, :]).astype(BF16)
    col = 0
    kept = {}
    for name, width in Z_SPLITS:
        for c in range(0, width, DOT_CHUNK):
            cw = min(DOT_CHUNK, width - c)
            r = jnp.dot(h, w_ref[:, col + c:col + c + cw], preferred_element_type=F32)
            if name in ("mu", "mv"):
                kept[name] = r
            elif name in ("ga", "gb", "gc"):
                outs[name][:, c:c + cw] = _sigmoid(r).astype(BF16)
            elif name == "gr":
                outs[name][:, c:c + cw] = (r * _sigmoid(r)).astype(BF16)
            else:
                outs[name][:, c:c + cw] = r.astype(BF16)
        col += width

    u = jax.nn.gelu(kept["mu"], approximate=True)
    vn = _rms(jax.nn.gelu(kept["mv"], approximate=True), gn_ref[...]).astype(BF16)
    gc = MLP_W // N_MLP_GROUPS
    for n in range(ROW_TILE // BLOCK):
        rows = slice(n * BLOCK, (n + 1) * BLOCK)
        for g in range(N_MLP_GROUPS):
            cols = slice(g * gc, (g + 1) * gc)
            f = jnp.dot(ws_ref[g], vn[rows, cols], preferred_element_type=F32) + bs_ref[g]
            outs["cm"][rows, cols] = (u[rows, cols] * f).astype(BF16)
    if combine:
        _finish_expert_rows(y_hbm, ybuf, sem)


def _input_projection(l, x, mods, norm1, w_in_bf, mlp_norm, w_spatial_bf, b_spatial,
                      moe_out=None, dest_rows=None, wcol=None):
    combine = moe_out is not None
    tm = ROW_TILE
    nt = T_ALL // tm
    row = lambda i, *_: (i, 0)
    mod_spec = lambda ll: pl.BlockSpec((None, None, N_MOD, D_MODEL), lambda i, *_: (ll, _mod_row(i), 0, 0))
    in_specs = [pl.BlockSpec((tm, D_MODEL), row)]
    args = [x]
    if combine:
        in_specs += [pl.BlockSpec(memory_space=pl.ANY),
                     pl.BlockSpec((tm, TOP_K), row),
                     mod_spec(l - 1)]
        args = [dest_rows, x, moe_out, wcol, mods]
    in_specs += [mod_spec(l),
                 pl.BlockSpec((None, 1, D_MODEL), lambda i, *_: (l, 0, 0)),
                 pl.BlockSpec((None, D_MODEL, w_in_bf.shape[-1]), lambda i, *_: (l, 0, 0),
                              pipeline_mode=pl.Buffered(1)),
                 pl.BlockSpec((None, 1, MLP_W), lambda i, *_: (l, 0, 0)),
                 pl.BlockSpec((None, N_MLP_GROUPS, BLOCK, BLOCK), lambda i, *_: (l, 0, 0, 0)),
                 pl.BlockSpec((None, N_MLP_GROUPS, BLOCK, 1), lambda i, *_: (l, 0, 0, 0))]
    args += [mods, norm1.reshape(DEPTH, 1, D_MODEL), w_in_bf, mlp_norm.reshape(DEPTH, 1, MLP_W), w_spatial_bf,
             b_spatial.reshape(DEPTH, N_MLP_GROUPS, BLOCK, 1)]
    out_specs, out_shape = [], []
    if combine:
        out_specs.append(pl.BlockSpec((tm, D_MODEL), row))
        out_shape.append(jax.ShapeDtypeStruct((T_ALL, D_MODEL), F32))
    for _, width in Z_OUTPUTS:
        out_specs.append(pl.BlockSpec((tm, width), row))
        out_shape.append(jax.ShapeDtypeStruct((T_ALL, width), BF16))
    grid_spec = pltpu.PrefetchScalarGridSpec(
        num_scalar_prefetch=1 if combine else 0,
        grid=(nt,),
        in_specs=in_specs, out_specs=out_specs,
        scratch_shapes=_expert_rows_scratch() if combine else [])
    res = pl.pallas_call(
        functools.partial(_in_kernel, combine=combine),
        grid_spec=grid_spec, out_shape=out_shape,
        compiler_params=_params(("arbitrary",), V7X_VMEM_LIMIT_BYTES),
        name="input_projection",
    )(*args)
    if combine:
        x, res = res[0], res[1:]
    return x, {name: r for (name, _), r in zip(Z_OUTPUTS, res)}


def _rope_tables(rows, dim, reps):
    r = jnp.repeat(jnp.arange(rows, dtype=F32), GRID_W)
    col = jnp.tile(jnp.arange(GRID_W, dtype=F32), rows)
    half = dim // 2
    inv = ROPE_BASE ** (-jnp.arange(0, half, 2, dtype=F32) / half)
    ar = r[:, None] * inv
    ac = col[:, None] * inv
    ang = jnp.concatenate([ar, ar, ac, ac], axis=-1)
    cos, sin = jnp.cos(ang), jnp.sin(ang)
    first = (jnp.arange(dim) % (dim // 2)) < (dim // 4)
    sin_up = jnp.where(first, -sin, 0.0)
    sin_dn = jnp.where(first, 0.0, sin)
    t = lambda a: jnp.tile(a, (1, reps))
    return t(cos), t(sin_up), t(sin_dn)


def _rope(x, cos, sin_up, sin_dn, quarter):
    w = x.shape[-1]
    return x * cos + pltpu.roll(x, w - quarter, 1) * sin_up + pltpu.roll(x, quarter, 1) * sin_dn


def _attn_kernel(*refs, sk_new, use_rope, has_cache, emit_kv):
    it = iter(refs)
    q_ref, kv_ref, gq_ref, gk_ref, bdq_ref, bdk_ref = (next(it) for _ in range(6))
    if use_rope:
        cq_ref, suq_ref, sdq_ref, ck_ref, suk_ref, sdk_ref = (next(it) for _ in range(6))
    if has_cache:
        kc_ref, vc_ref = next(it), next(it)
    o_ref = next(it)
    if emit_kv:
        nk_ref, nv_ref = next(it), next(it)
    kd_scr, vd_scr = next(it), next(it)

    lo = lax.broadcasted_iota(jnp.int32, (1, 2 * HEAD_DIM), 1) < HEAD_DIM

    def dup_halves(a):
        r = pltpu.roll(a, HEAD_DIM, 1)
        return jnp.where(lo, a, r), jnp.where(lo, r, a)

    @pl.when(pl.program_id(1) == 0)
    def _prep():
        kv = kv_ref[...].astype(F32)
        k, v = kv[:, :ATT_KV], kv[:, ATT_KV:]
        ms = jnp.dot((k * k).astype(BF16), bdk_ref[...], preferred_element_type=F32)
        kn = k * lax.rsqrt(ms + EPS) * gk_ref[...]
        if emit_kv:
            nk_ref[...] = kn
            nv_ref[...] = v
        if use_rope:
            kn = _rope(kn, ck_ref[...], suk_ref[...], sdk_ref[...], HEAD_DIM // 4)
        k0, k1 = dup_halves(kn)
        v0, v1 = dup_halves(v)
        kd_scr[0, 0:sk_new, :] = k0.astype(BF16)
        kd_scr[1, 0:sk_new, :] = k1.astype(BF16)
        vd_scr[0, 0:sk_new, :] = v0.astype(BF16)
        vd_scr[1, 0:sk_new, :] = v1.astype(BF16)
        if has_cache:
            c0, c1 = dup_halves(kc_ref[...])
            d0, d1 = dup_halves(vc_ref[...])
            kd_scr[0, sk_new:, :] = c0.astype(BF16)
            kd_scr[1, sk_new:, :] = c1.astype(BF16)
            vd_scr[0, sk_new:, :] = d0.astype(BF16)
            vd_scr[1, sk_new:, :] = d1.astype(BF16)

    q = q_ref[...].astype(F32)
    ms = jnp.dot((q * q).astype(BF16), bdq_ref[...], preferred_element_type=F32)
    qn = q * lax.rsqrt(ms + EPS) * gq_ref[...]
    if use_rope:
        qn = _rope(qn, cq_ref[...], suq_ref[...], sdq_ref[...], HEAD_DIM // 4)
    qn = qn * (HEAD_DIM ** -0.5)
    heads_per_kv = N_Q_HEADS // N_KV_HEADS
    for j in range(N_Q_HEADS // 2):
        grp = (2 * j) // heads_per_kv
        kd, vd = kd_scr[grp], vd_scr[grp]
        qp = qn[:, 2 * HEAD_DIM * j:2 * HEAD_DIM * (j + 1)]
        halves = []
        for qm in (jnp.where(lo, qp, 0.0), jnp.where(lo, 0.0, qp)):
            s = lax.dot_general(qm.astype(BF16), kd, (((1,), (1,)), ((), ())), preferred_element_type=F32)
            e = jnp.exp(s - jnp.max(s, axis=-1, keepdims=True))
            den = jnp.sum(e, axis=-1, keepdims=True)
            halves.append(jnp.dot(e.astype(BF16), vd, preferred_element_type=F32) / den)
        o_ref[:, 2 * HEAD_DIM * j:2 * HEAD_DIM * (j + 1)] = jnp.where(lo, halves[0], halves[1]).astype(BF16)


def _block_diag_mean(width, group):
    idx = np.arange(width) // group
    return jnp.asarray((idx[:, None] == idx[None, :]).astype(np.float32) / group, dtype=BF16)


def _attention(l, z, q_norm, k_norm, *, latent, rope=None, cache_k=None, cache_v=None):
    if latent:
        nb, s, tq, row0 = DEC_BATCH, DEC_SEQ, 256, T_CTX
    else:
        nb, s, tq, row0 = BATCH, SEQ, SEQ, 0
    nq = s // tq
    sk = s + (PAST_LEN if latent else 0)
    qrow = lambda b, qi: (row0 // tq + b * nq + qi, 0)
    krow = lambda b, qi: (row0 // s + b, 0)
    const = lambda b, qi: (0, 0)
    lrow = lambda b, qi: (l, 0, 0)
    in_specs = [pl.BlockSpec((tq, ATT_Q), qrow),
                pl.BlockSpec((s, 2 * ATT_KV), krow),
                pl.BlockSpec((None, 1, ATT_Q), lrow),
                pl.BlockSpec((None, 1, ATT_KV), lrow),
                pl.BlockSpec((ATT_Q, ATT_Q), const),
                pl.BlockSpec((ATT_KV, ATT_KV), const)]
    args = [z["qa"], z["kv"],
            jnp.tile(q_norm, (1, N_Q_HEADS)).reshape(DEPTH, 1, ATT_Q),
            jnp.tile(k_norm, (1, N_KV_HEADS)).reshape(DEPTH, 1, ATT_KV),
            _block_diag_mean(ATT_Q, HEAD_DIM), _block_diag_mean(ATT_KV, HEAD_DIM)]
    if latent:
        cq, suq, sdq, ck, suk, sdk = rope
        in_specs += [pl.BlockSpec((tq, ATT_Q), lambda b, qi: (qi, 0))] * 3
        in_specs += [pl.BlockSpec((s, ATT_KV), const)] * 3
        in_specs += [pl.BlockSpec((None, None, PAST_LEN, ATT_KV), lambda b, qi: (b, l, 0, 0))] * 2
        args += [cq, suq, sdq, ck, suk, sdk,
                 cache_k.reshape(DEC_BATCH, DEPTH, PAST_LEN, ATT_KV),
                 cache_v.reshape(DEC_BATCH, DEPTH, PAST_LEN, ATT_KV)]
    rows = nb * s
    out_specs = [pl.BlockSpec((tq, ATT_Q), lambda b, qi: (b * nq + qi, 0))]
    out_shape = [jax.ShapeDtypeStruct((rows, ATT_Q), BF16)]
    if not latent:
        out_specs += [pl.BlockSpec((s, ATT_KV), lambda b, qi: (b, 0))] * 2
        out_shape += [jax.ShapeDtypeStruct((rows, ATT_KV), F32)] * 2
    return pl.pallas_call(
        functools.partial(_attn_kernel, sk_new=s, use_rope=latent, has_cache=latent, emit_kv=not latent),
        grid=(nb, nq),
        in_specs=in_specs, out_specs=out_specs, out_shape=out_shape,
        scratch_shapes=[pltpu.VMEM((N_KV_HEADS, sk, 2 * HEAD_DIM), BF16),
                        pltpu.VMEM((N_KV_HEADS, sk, 2 * HEAD_DIM), BF16)],
        compiler_params=_params(("arbitrary", "arbitrary"), V7X_VMEM_LIMIT_BYTES),
        name="attention_latent" if latent else "attention_context",
    )(*args)


def _ret_kernel(*refs, s, heads, use_rope, has_state):
    it = iter(refs)
    q_ref, k_ref, v_ref, g_ref, dec_ref, gn_ref = (next(it) for _ in range(6))
    rope = (next(it)[...], next(it)[...], next(it)[...]) if use_rope else None
    s0f_ref, s0b_ref = (next(it), next(it)) if has_state else (None, None)
    o_ref, sf_ref, sb_ref, oacc = next(it), next(it), next(it), next(it)
    for hh in range(heads):
        cols = slice(hh * RET_DK, (hh + 1) * RET_DK)
        _ret_head(q_ref.at[:, cols], k_ref.at[:, cols], v_ref.at[:, cols], g_ref.at[:, cols],
                  dec_ref.at[:, hh], gn_ref.at[hh], rope,
                  s0f_ref.at[hh] if has_state else None, s0b_ref.at[hh] if has_state else None,
                  o_ref.at[:, cols], sf_ref.at[hh], sb_ref.at[hh], oacc.at[hh], s)


def _ret_head(q_ref, k_ref, v_ref, g_ref, dec_ref, gn_ref, rope, s0f_ref, s0b_ref, o_ref, sf_ref, sb_ref, oacc, s):
    use_rope, has_state = rope is not None, s0f_ref is not None
    n_chunks = s // BLOCK
    lgf = -jnp.exp(dec_ref[0])
    lgb = -jnp.exp(dec_ref[1])
    lgf1, lgb1 = lgf[:, 0:1], lgb[:, 0:1]
    diff = (lax.broadcasted_iota(jnp.int32, (BLOCK, BLOCK), 0)
            - lax.broadcasted_iota(jnp.int32, (BLOCK, BLOCK), 1)).astype(F32)
    dsum = (jnp.where(diff >= 0, jnp.exp(diff * lgf), 0.0)
            + jnp.where(diff <= 0, jnp.exp(-diff * lgb), 0.0))
    ic = lax.broadcasted_iota(jnp.int32, (BLOCK, 1), 0).astype(F32)
    qdf, kdf, cdf = jnp.exp((ic + 1.0) * lgf1), jnp.exp((BLOCK - 1.0 - ic) * lgf1), jnp.exp(BLOCK * lgf1)
    qdb, kdb, cdb = jnp.exp((BLOCK - ic) * lgb1), jnp.exp(ic * lgb1), jnp.exp(BLOCK * lgb1)

    q = q_ref[...].astype(F32)
    k = k_ref[...].astype(F32) * (RET_DK ** -0.5)
    if use_rope:
        q = _rope(q, *rope, RET_DK // 4)
        k = _rope(k, *rope, RET_DK // 4)
    v = v_ref[...]

    def rows(a, n):
        return a[n * BLOCK:(n + 1) * BLOCK]

    def state_step(state, kn, kdec, vn, cdec):
        kd_t = jnp.transpose(kn * kdec).astype(BF16)
        return cdec * state + jnp.dot(kd_t, vn, preferred_element_type=F32)

    state = s0f_ref[...] if has_state else jnp.zeros((RET_DK, RET_DV), F32)
    for n in range(n_chunks):
        qn, kn, vn = rows(q, n), rows(k, n), rows(v, n)
        inner = lax.dot_general(qn.astype(BF16), kn.astype(BF16), (((1,), (1,)), ((), ())),
                                preferred_element_type=F32) * dsum
        o = (jnp.dot(inner.astype(BF16), vn, preferred_element_type=F32)
             + jnp.dot((qn * qdf).astype(BF16), state.astype(BF16), preferred_element_type=F32))
        state = state_step(state, kn, kdf, vn, cdf)
        oacc[n * BLOCK:(n + 1) * BLOCK, :] = o
    sf_ref[...] = state

    state = s0b_ref[...] if has_state else jnp.zeros((RET_DK, RET_DV), F32)
    for n in reversed(range(n_chunks)):
        qn, kn, vn = rows(q, n), rows(k, n), rows(v, n)
        o = rows(oacc, n) + jnp.dot((qn * qdb).astype(BF16), state.astype(BF16), preferred_element_type=F32)
        state = state_step(state, kn, kdb, vn, cdb)
        mu = jnp.mean(o, axis=-1, keepdims=True)
        var = jnp.mean(jnp.square(o - mu), axis=-1, keepdims=True)
        y = (o - mu) * lax.rsqrt(var + EPS) * gn_ref[...]
        gate = g_ref[n * BLOCK:(n + 1) * BLOCK, :].astype(F32)
        o_ref[n * BLOCK:(n + 1) * BLOCK, :] = (y * gate).astype(BF16)
    sb_ref[...] = state


def _retention(l, z, dec, ret_gn, *, latent, rope=None, s0f=None, s0b=None):
    if latent:
        nb, s, row0, heads = DEC_BATCH, DEC_SEQ, T_CTX, 1
    else:
        nb, s, row0, heads = BATCH, SEQ, 0, N_RET_HEADS
    width = heads * RET_DK
    hrow = lambda b, h: (row0 // s + b, h)
    in_specs = [pl.BlockSpec((s, width), hrow)] * 4
    in_specs += [pl.BlockSpec((None, 2, heads, 1, RET_DK), lambda b, h: (l, 0, h, 0, 0)),
                 pl.BlockSpec((None, heads, 1, RET_DV), lambda b, h: (l, h, 0, 0))]
    args = [z["qr"], z["kr"], z["vr"], z["gr"], dec, ret_gn.reshape(DEPTH, N_RET_HEADS, 1, RET_DV)]
    if latent:
        in_specs += [pl.BlockSpec((s, RET_DK), lambda b, h: (0, 0))] * 3
        in_specs += [pl.BlockSpec((None, None, heads, RET_DK, RET_DV), lambda b, h: (b, l, h, 0, 0))] * 2
        args += [*rope, s0f, s0b]
    st_spec = pl.BlockSpec((None, heads, RET_DK, RET_DV), lambda b, h: (b, h, 0, 0))
    st_shape = jax.ShapeDtypeStruct((nb, N_RET_HEADS, RET_DK, RET_DV), F32)
    return pl.pallas_call(
        functools.partial(_ret_kernel, s=s, heads=heads, use_rope=latent, has_state=latent),
        grid=(nb, N_RET_HEADS // heads),
        in_specs=in_specs,
        out_specs=[pl.BlockSpec((s, width), lambda b, h: (b, h)), st_spec, st_spec],
        out_shape=[jax.ShapeDtypeStruct((nb * s, RET_W), BF16), st_shape, st_shape],
        scratch_shapes=[pltpu.VMEM((heads, s, RET_DV), F32)],
        compiler_params=_params(("arbitrary", "arbitrary")),
        name="retention_latent" if latent else "retention_context",
    )(*args)


def _first_max(vals):
    best, idx = vals[0], jnp.zeros(vals[0].shape, jnp.int32)
    for j in range(1, len(vals)):
        upd = vals[j] > best
        best = jnp.where(upd, vals[j], best)
        idx = jnp.where(upd, j, idx)
    return best, idx


def _pick(idx, vals):
    out = vals[-1]
    for j in range(len(vals) - 2, -1, -1):
        out = jnp.where(idx == j, vals[j], out)
    return out


def _merge_kernel(x_ref, attc_ref, attl_ref, retc_ref, retl_ref, cm_ref, ga_ref, gb_ref, gc_ref, mod_ref, g2_ref,
                  wa_ref, wr_ref, wm_ref, wo_ref, wrt_ref, br_ref, tri_ref,
                  x1_ref, h2_ref, eid_ref, wts_ref, rank_ref, cnt_ref, cnt_scr):
    @pl.when(pl.program_id(0) == 0)
    def _zero_counts():
        cnt_scr[...] = jnp.zeros_like(cnt_scr)

    is_lat = pl.program_id(0) >= T_CTX // ROW_TILE
    att = jnp.where(is_lat, attl_ref[...], attc_ref[...])
    ret = jnp.where(is_lat, retl_ref[...], retc_ref[...])
    gate = lambda r: r[...].astype(F32)
    dot = lambda a, b: jnp.dot(a, b, preferred_element_type=F32)
    mix = (gate(ga_ref) * dot(att, wa_ref[...])
           + gate(gb_ref) * dot(ret, wr_ref[...])
           + gate(gc_ref) * dot(cm_ref[...], wm_ref[...]))
    x1 = x_ref[...] + mod_ref[2:3, :] * dot(mix.astype(BF16), wo_ref[...])
    x1_ref[...] = x1
    h2 = _rms(x1, g2_ref[...]) * (1.0 + mod_ref[4:5, :]) + mod_ref[3:4, :]
    _store_row_tiles(h2_ref, h2)

    logits = lax.dot_general(wrt_ref[...], h2, (((1,), (1,)), ((), ())), preferred_element_type=F32,
                             precision=lax.Precision.HIGHEST)
    score = _sigmoid(logits)
    sel = score + br_ref[...]
    sel_rows = [sel[e:e + 1, :] for e in range(N_EXPERTS)]
    score_rows = [score[e:e + 1, :] for e in range(N_EXPERTS)]
    group_scores = []
    for g in range(N_EXPERT_GROUPS):
        v = sel_rows[g * EXPERTS_PER_GROUP:(g + 1) * EXPERTS_PER_GROUP]
        pair_sums = [v[a] + v[b] for a in range(EXPERTS_PER_GROUP) for b in range(a + 1, EXPERTS_PER_GROUP)]
        group_scores.append(functools.reduce(jnp.maximum, pair_sums))
    _, gidx = _first_max(group_scores)
    in_sel = [_pick(gidx, [sel_rows[g * EXPERTS_PER_GROUP + j] for g in range(N_EXPERT_GROUPS)])
              for j in range(EXPERTS_PER_GROUP)]
    in_score = [_pick(gidx, [score_rows[g * EXPERTS_PER_GROUP + j] for g in range(N_EXPERT_GROUPS)])
                for j in range(EXPERTS_PER_GROUP)]
    _, loc0 = _first_max(in_sel)
    _, loc1 = _first_max([jnp.where(loc0 == j, -jnp.inf, in_sel[j]) for j in range(EXPERTS_PER_GROUP)])
    w0, w1 = _pick(loc0, in_score), _pick(loc1, in_score)
    den = w0 + w1
    e0 = gidx * EXPERTS_PER_GROUP + loc0
    e1 = gidx * EXPERTS_PER_GROUP + loc1
    eid_ref[0:1, :] = e0
    eid_ref[1:2, :] = e1
    wts_ref[0:1, :] = w0 / den
    wts_ref[1:2, :] = w1 / den

    eio = lax.broadcasted_iota(jnp.int32, score.shape, 0)
    oh0 = (eio == e0).astype(F32)
    oh1 = (eio == e1).astype(F32)
    both = oh0 + oh1
    before = jnp.dot(both.astype(BF16), tri_ref[...], preferred_element_type=F32) + cnt_scr[...]
    rank_ref[0:1, :] = jnp.sum(oh0 * before, axis=0, keepdims=True).astype(jnp.int32)
    rank_ref[1:2, :] = jnp.sum(oh1 * before, axis=0, keepdims=True).astype(jnp.int32)
    cnt_scr[...] = cnt_scr[...] + jnp.sum(both, axis=1, keepdims=True)
    cnt_ref[...] = cnt_scr[...]


def _merge(l, x, att_c, att_l, ret_c, ret_l, cm, z, mods, norm2, wa, wr, wm, wo, w_router_t, b_router):
    tm = ROW_TILE
    row = lambda i: (i, 0)
    n_ctx = T_CTX // tm
    ctx_row = lambda i: (jnp.minimum(i, n_ctx - 1), 0)
    lat_row = lambda i: (jnp.maximum(i - n_ctx, 0), 0)
    lw = lambda k, n: pl.BlockSpec((None, k, n), lambda i: (l, 0, 0))
    in_specs = [pl.BlockSpec((tm, D_MODEL), row),
                pl.BlockSpec((tm, ATT_Q), ctx_row), pl.BlockSpec((tm, ATT_Q), lat_row),
                pl.BlockSpec((tm, RET_W), ctx_row), pl.BlockSpec((tm, RET_W), lat_row),
                pl.BlockSpec((tm, MLP_W), row),
                pl.BlockSpec((tm, D_MODEL), row), pl.BlockSpec((tm, D_MODEL), row), pl.BlockSpec((tm, D_MODEL), row),
                pl.BlockSpec((None, None, N_MOD, D_MODEL), lambda i: (l, _mod_row(i), 0, 0)),
                pl.BlockSpec((None, 1, D_MODEL), lambda i: (l, 0, 0)),
                lw(ATT_Q, D_MODEL), lw(RET_W, D_MODEL), lw(MLP_W, D_MODEL), lw(D_MODEL, D_MODEL),
                pl.BlockSpec((N_EXPERTS, D_MODEL), lambda i: (0, 0)),
                pl.BlockSpec((N_EXPERTS, 1), lambda i: (0, 0)),
                pl.BlockSpec((tm, tm), lambda i: (0, 0))]
    lane = lambda i: (0, i)
    tri = jnp.asarray(np.triu(np.ones((tm, tm), np.float32), k=1), dtype=BF16)
    return pl.pallas_call(
        _merge_kernel,
        grid=(T_ALL // tm,),
        in_specs=in_specs,
        out_specs=[pl.BlockSpec((tm, D_MODEL), row), pl.BlockSpec((tm * TILE_ROWS, 128), row),
                   pl.BlockSpec((TOP_K, tm), lane), pl.BlockSpec((TOP_K, tm), lane),
                   pl.BlockSpec((TOP_K, tm), lane), pl.BlockSpec((N_EXPERTS, 1), lambda i: (0, 0))],
        out_shape=[jax.ShapeDtypeStruct((T_ALL, D_MODEL), F32), jax.ShapeDtypeStruct((T_ALL * TILE_ROWS, 128), F32),
                   jax.ShapeDtypeStruct((TOP_K, T_ALL), jnp.int32), jax.ShapeDtypeStruct((TOP_K, T_ALL), F32),
                   jax.ShapeDtypeStruct((TOP_K, T_ALL), jnp.int32), jax.ShapeDtypeStruct((N_EXPERTS, 1), F32)],
        scratch_shapes=[pltpu.VMEM((N_EXPERTS, 1), F32)],
        compiler_params=_params(("arbitrary",), V7X_VMEM_LIMIT_BYTES),
        name="merge_router",
    )(x, att_c, att_l, ret_c, ret_l, cm, z["ga"], z["gb"], z["gc"], mods, norm2.reshape(DEPTH, 1, D_MODEL),
      wa, wr, wm, wo, w_router_t, b_router.reshape(N_EXPERTS, 1), tri)


def _dispatch_plan(eid, rank, counts):
    expert = jnp.arange(N_EXPERTS, dtype=jnp.int32)
    padded = (counts + MOE_BLOCK - 1) // MOE_BLOCK * MOE_BLOCK
    pend = jnp.cumsum(padded)
    pstart = pend - padded
    dest = rank + jnp.sum(jnp.where(eid[..., None] == expert, pstart, 0), axis=-1)
    block_start = jnp.arange(N_MOE_BLOCKS, dtype=jnp.int32) * MOE_BLOCK
    block_e = jnp.minimum(jnp.sum((pend[None, :] <= block_start[:, None]).astype(jnp.int32), axis=1),
                          N_EXPERTS - 1)
    pad_ranges = jnp.stack([jnp.append(pstart + counts, pend[-1]),
                            jnp.append(pend, N_SLOTS)]).astype(jnp.int32)
    return block_e.astype(jnp.int32), pad_ranges, dest.reshape(N_ASSIGN)


ROWS_PER_TRIP = 8


N_DMA_PRIORITIES = 2


def _moe_kernel(be_ref, pad_ref, dest_ref, h_ref, wg_ref, wu_ref, wd_ref, o_ref, xbuf, wg_s, wu_s, wd_s, src_s):
    i = pl.program_id(0)
    last = pl.num_programs(0) - 1

    def gather(blk, buf):
        for r in range(MOE_BLOCK):
            first = pl.multiple_of(src_s[blk * MOE_BLOCK + r], TILE_ROWS)
            xbuf[buf, r * TILE_ROWS:(r + 1) * TILE_ROWS, :] = h_ref[pl.ds(first, TILE_ROWS), :]

    @pl.when(i == 0)
    def _init():
        def clear(j, c):
            src_s[j] = 0
            return c
        for e in range(N_EXPERTS + 1):
            lax.fori_loop(pad_ref[0, e], pad_ref[1, e], clear, 0)

        def put(g, c):
            for u in range(ROWS_PER_TRIP):
                t = g * ROWS_PER_TRIP + u
                for k in range(TOP_K):
                    src_s[dest_ref[k * T_ALL + t]] = t * TILE_ROWS
            return c
        lax.fori_loop(0, T_ALL // ROWS_PER_TRIP, put, 0)
        gather(0, 0)

    prev = be_ref[jnp.maximum(i - 1, 0)]

    @pl.when((i == 0) | (be_ref[i] != prev))
    def _cast():
        wg_s[...] = wg_ref[...].astype(BF16)
        wu_s[...] = wu_ref[...].astype(BF16)
        wd_s[...] = wd_ref[...].astype(BF16)

    buf = lax.rem(i, 2)
    gather(jnp.minimum(i + 1, last), 1 - buf)
    x = _load_row_tiles(xbuf.at[buf]).astype(BF16)
    a = jnp.dot(x, wg_s[...], preferred_element_type=F32)
    b = jnp.dot(x, wu_s[...], preferred_element_type=F32)
    mid = ((a * _sigmoid(a)) * b).astype(BF16)
    _store_row_tiles(o_ref, jnp.dot(mid, wd_s[...], preferred_element_type=F32))


def _moe(l, h2, plan, w_gate, w_up, w_down):
    block_e, pad_ranges, dest = plan
    once = pl.Buffered(1)
    wspec = lambda k, n: pl.BlockSpec((None, None, k, n), lambda i, be, pr, de: (l, be[i], 0, 0),
                                      pipeline_mode=once)
    grid_spec = pltpu.PrefetchScalarGridSpec(
        num_scalar_prefetch=3,
        grid=(N_MOE_BLOCKS,),
        in_specs=[pl.BlockSpec((T_ALL * TILE_ROWS, 128), lambda i, be, pr, de: (0, 0), pipeline_mode=once),
                  wspec(D_MODEL, EXPERT_FF), wspec(D_MODEL, EXPERT_FF), wspec(EXPERT_FF, D_MODEL)],
        out_specs=pl.BlockSpec((MOE_BLOCK * TILE_ROWS, 128), lambda i, be, pr, de: (i, 0)),
        scratch_shapes=[pltpu.VMEM((2, MOE_BLOCK * TILE_ROWS, 128), F32),
                        pltpu.VMEM((D_MODEL, EXPERT_FF), BF16), pltpu.VMEM((D_MODEL, EXPERT_FF), BF16),
                        pltpu.VMEM((EXPERT_FF, D_MODEL), BF16),
                        pltpu.SMEM((N_SLOTS,), jnp.int32)])
    return pl.pallas_call(
        _moe_kernel,
        grid_spec=grid_spec,
        out_shape=jax.ShapeDtypeStruct((N_SLOTS * TILE_ROWS, 128), F32),
        compiler_params=_params(("arbitrary",), V7X_MOE_VMEM_LIMIT_BYTES),
        name="moe_experts",
    )(block_e, pad_ranges, dest, h2, w_gate, w_up, w_down)


def _start_expert_rows(dest_ref, y_hbm, ybuf, sem, tile_idx, slot):
    base = tile_idx * ROW_TILE
    for k in range(TOP_K):
        for t in range(ROW_TILE):
            first = pl.multiple_of(dest_ref[k * T_ALL + base + t], TILE_ROWS)
            pltpu.make_async_copy(y_hbm.at[pl.ds(first, TILE_ROWS)],
                                  ybuf.at[slot, k, pl.ds(t * TILE_ROWS, TILE_ROWS)],
                                  sem.at[slot]).start(priority=t % N_DMA_PRIORITIES)


def _wait_expert_rows(y_hbm, ybuf, sem, slot):
    for k in range(TOP_K):
        pltpu.make_async_copy(y_hbm.at[pl.ds(0, ROW_TILE * TILE_ROWS)], ybuf.at[slot, k], sem.at[slot]).wait()


def _gathered_expert_rows(dest_ref, y_hbm, ybuf, sem):
    i = pl.program_id(0)
    last = pl.num_programs(0) - 1
    slot = lax.rem(i, 2)

    @pl.when(i == 0)
    def _first():
        _start_expert_rows(dest_ref, y_hbm, ybuf, sem, 0, 0)

    _start_expert_rows(dest_ref, y_hbm, ybuf, sem, jnp.minimum(i + 1, last), 1 - slot)
    _wait_expert_rows(y_hbm, ybuf, sem, slot)
    return _load_row_tiles(ybuf.at[slot, 0]), _load_row_tiles(ybuf.at[slot, 1])


def _finish_expert_rows(y_hbm, ybuf, sem):
    i = pl.program_id(0)

    @pl.when(i == pl.num_programs(0) - 1)
    def _():
        _wait_expert_rows(y_hbm, ybuf, sem, 1 - lax.rem(i, 2))


def _combine_kernel(dest_ref, x_ref, y_hbm, wc_ref, mod_ref, oc_ref, ol_ref, ybuf, sem):
    wc = wc_ref[...]
    y0, y1 = _gathered_expert_rows(dest_ref, y_hbm, ybuf, sem)
    x = x_ref[...] + mod_ref[5:6, :] * (wc[:, 0:1] * y0 + wc[:, 1:2] * y1)
    _finish_expert_rows(y_hbm, ybuf, sem)
    is_lat = pl.program_id(0) >= T_CTX // ROW_TILE

    @pl.when(jnp.logical_not(is_lat))
    def _():
        oc_ref[...] = x

    @pl.when(is_lat)
    def _():
        ol_ref[...] = x


def _expert_rows_scratch():
    return [pltpu.VMEM((2, TOP_K, ROW_TILE * TILE_ROWS, 128), F32), pltpu.SemaphoreType.DMA((2,))]


def _final_combine(l, x, moe_out, dest_rows, wcol, mods):
    tm = ROW_TILE
    nt = T_ALL // tm
    n_ctx = T_CTX // tm
    row = lambda i, de: (i, 0)
    grid_spec = pltpu.PrefetchScalarGridSpec(
        num_scalar_prefetch=1,
        grid=(nt,),
        in_specs=[pl.BlockSpec((tm, D_MODEL), row), pl.BlockSpec(memory_space=pl.ANY),
                  pl.BlockSpec((tm, TOP_K), row),
                  pl.BlockSpec((None, None, N_MOD, D_MODEL), lambda i, de: (l, _mod_row(i), 0, 0))],
        out_specs=[pl.BlockSpec((tm, D_MODEL), lambda i, de: (jnp.minimum(i, n_ctx - 1), 0)),
                   pl.BlockSpec((tm, D_MODEL), lambda i, de: (jnp.maximum(i - n_ctx, 0), 0))],
        scratch_shapes=_expert_rows_scratch())
    return pl.pallas_call(
        _combine_kernel,
        grid_spec=grid_spec,
        out_shape=[jax.ShapeDtypeStruct((T_CTX, D_MODEL), F32), jax.ShapeDtypeStruct((T_LAT, D_MODEL), F32)],
        compiler_params=_params(("arbitrary",)),
        name="final_combine",
    )(dest_rows, x, moe_out, wcol, mods)


def kernel(x_prompt, x_sample, cache_k, cache_v, state_ret_fwd, state_ret_bwd, c, c_ctx, w_mod, b_mod, norm1, norm2, w_in, q_norm, k_norm, ret_decay_fwd, ret_decay_bwd, ret_gn, mlp_norm, w_spatial, b_spatial, w_att_o, w_ret_o, w_mlp_o, w_out, w_router, b_router, w_gate, w_up, w_down):
    cond = jnp.zeros((8, D_MODEL), F32).at[0].set(c_ctx).at[1:1 + DEC_BATCH].set(c)
    mods = _modulation(cond, w_mod, b_mod)

    w_in_bf = w_in.astype(BF16)
    wa, wr, wm, wo = (w.astype(BF16) for w in (w_att_o, w_ret_o, w_mlp_o, w_out))
    ws_bf = w_spatial.astype(BF16)
    w_router_t = w_router.T
    dec = jnp.broadcast_to(jnp.stack([ret_decay_fwd, ret_decay_bwd], axis=1)[..., None, None],
                           (DEPTH, 2, N_RET_HEADS, 1, RET_DK))
    rows = DEC_SEQ // GRID_W
    rope_q = _rope_tables(rows, HEAD_DIM, N_Q_HEADS)
    rope_k = _rope_tables(rows, HEAD_DIM, N_KV_HEADS)
    rope_r = _rope_tables(rows, RET_DK, 1)

    x = jnp.concatenate([x_prompt.reshape(T_CTX, D_MODEL), x_sample.reshape(T_LAT, D_MODEL)], axis=0)
    moe_out, dest_rows, wcol = None, None, None
    ks, vs, sfs, sbs = [], [], [], []
    for l in range(DEPTH):
        x, z = _input_projection(l, x, mods, norm1, w_in_bf, mlp_norm, ws_bf, b_spatial, moe_out, dest_rows, wcol)

        att_c, k_l, v_l = _attention(l, z, q_norm, k_norm, latent=False)
        att_l, = _attention(l, z, q_norm, k_norm, latent=True, rope=rope_q + rope_k,
                            cache_k=cache_k, cache_v=cache_v)
        ret_c, sf_l, sb_l = _retention(l, z, dec, ret_gn, latent=False)
        ret_l, _, _ = _retention(l, z, dec, ret_gn, latent=True, rope=rope_r,
                                 s0f=state_ret_fwd, s0b=state_ret_bwd)
        x, h2, eid, wts, rank, cnt = _merge(l, x, att_c, att_l, ret_c, ret_l, z["cm"], z, mods, norm2,
                                            wa, wr, wm, wo, w_router_t, b_router)
        plan = _dispatch_plan(eid, rank, cnt[:, 0].astype(jnp.int32))
        moe_out = _moe(l, h2, plan, w_gate, w_up, w_down)
        dest_rows = plan[2] * TILE_ROWS
        wcol = wts.T
        ks.append(k_l.reshape(BATCH, SEQ, ATT_KV))
        vs.append(v_l.reshape(BATCH, SEQ, ATT_KV))
        sfs.append(sf_l)
        sbs.append(sb_l)
    y_ctx, y_lat = _final_combine(DEPTH - 1, x, moe_out, dest_rows, wcol, mods)

    y_prompt = y_ctx.reshape(BATCH, SEQ, D_MODEL)
    y_sample = y_lat.reshape(DEC_BATCH, DEC_SEQ, D_MODEL)
    new_k = jnp.stack(ks, axis=1).reshape(BATCH, DEPTH, SEQ, N_KV_HEADS, HEAD_DIM)
    new_v = jnp.stack(vs, axis=1).reshape(BATCH, DEPTH, SEQ, N_KV_HEADS, HEAD_DIM)
    new_sf = jnp.stack(sfs, axis=1)
    new_sb = jnp.stack(sbs, axis=1)
    return (y_prompt, y_sample, new_k, new_v, new_sf, new_sb)
```

```python
import functools

import numpy as np
import jax
import jax.numpy as jnp
from jax import lax
from jax.experimental import pallas as pl
from jax.experimental.pallas import tpu as pltpu

F32 = jnp.float32
BF16 = jnp.bfloat16

D_MODEL = 1024
BATCH = 32
SEQ = 256
DEPTH = 4
DEC_BATCH = 2
DEC_SEQ = 1024
PAST_LEN = 256
GRID_W = 64
BLOCK = 128
ROPE_BASE = 10000.0
EPS = 1e-6
N_Q_HEADS = 8
N_KV_HEADS = 2
HEAD_DIM = 64
ATT_Q = N_Q_HEADS * HEAD_DIM
ATT_KV = N_KV_HEADS * HEAD_DIM
N_RET_HEADS = 4
RET_DK = 128
RET_DV = 128
RET_W = N_RET_HEADS * RET_DK
N_MLP_GROUPS = 4
MLP_W = 512
N_MOD = 6
N_EXPERTS = 16
N_EXPERT_GROUPS = 4
EXPERTS_PER_GROUP = N_EXPERTS // N_EXPERT_GROUPS
TOP_K = 2
EXPERT_FF = 512
MOE_BLOCK = 256

T_CTX = BATCH * SEQ
T_LAT = DEC_BATCH * DEC_SEQ
T_ALL = T_CTX + T_LAT
N_ASSIGN = T_ALL * TOP_K
N_MOE_BLOCKS = -(-N_ASSIGN // MOE_BLOCK) + N_EXPERTS
N_SLOTS = N_MOE_BLOCKS * MOE_BLOCK

ROW_TILE = 512
V7X_VMEM_LIMIT_BYTES = 56 * 1024 * 1024
V7X_MOE_VMEM_LIMIT_BYTES = 60 * 1024 * 1024

Z_SPLITS = (("qa", ATT_Q), ("kv", 2 * ATT_KV), ("qr", RET_W), ("kr", RET_W), ("vr", RET_W), ("gr", RET_W),
            ("mu", MLP_W), ("mv", MLP_W), ("ga", D_MODEL), ("gb", D_MODEL), ("gc", D_MODEL))
DOT_CHUNK = 512
Z_OUTPUTS = tuple((n, w) for n, w in Z_SPLITS if n not in ("mu", "mv")) + (("cm", MLP_W),)


def _params(sem, vmem=None):
    return pltpu.CompilerParams(dimension_semantics=sem, vmem_limit_bytes=vmem)


def _rms(x, g):
    return x * lax.rsqrt(jnp.mean(x * x, axis=-1, keepdims=True) + EPS) * g


def _sigmoid(x):
    return jax.nn.sigmoid(x)


TILE_ROWS = D_MODEL // 128


def _store_row_tiles(ref, val):
    rows = val.shape[0]
    for c in range(TILE_ROWS):
        ref[pl.ds(c, rows, stride=TILE_ROWS), :] = val[:, c * 128:(c + 1) * 128]


def _load_row_tiles(ref):
    rows = ref.shape[0] // TILE_ROWS
    return jnp.concatenate([ref[pl.ds(c, rows, stride=TILE_ROWS), :] for c in range(TILE_ROWS)], axis=1)


def _mod_row(i):
    first_lat = T_CTX // ROW_TILE
    per_batch = DEC_SEQ // ROW_TILE
    return jnp.where(i < first_lat, 0, 1 + (i - first_lat) // per_batch)


def _mod_kernel(cond_ref, w_ref, b_ref, o_ref):
    c = cond_ref[...]
    s = c * _sigmoid(c)
    o_ref[...] = jnp.dot(s, w_ref[...], preferred_element_type=F32,
                         precision=lax.Precision.HIGHEST) + b_ref[...]


def _modulation(cond, w_mod, b_mod):
    tn = D_MODEL
    out = pl.pallas_call(
        _mod_kernel,
        grid=(DEPTH, N_MOD),
        in_specs=[pl.BlockSpec((8, D_MODEL), lambda l, j: (0, 0)),
                  pl.BlockSpec((None, D_MODEL, tn), lambda l, j: (l, 0, j)),
                  pl.BlockSpec((None, 1, tn), lambda l, j: (l, 0, j))],
        out_specs=pl.BlockSpec((None, 8, tn), lambda l, j: (l, 0, j)),
        out_shape=jax.ShapeDtypeStruct((DEPTH, 8, N_MOD * D_MODEL), F32),
        compiler_params=_params(("arbitrary", "arbitrary")),
        name="modulation",
    )(cond, w_mod, b_mod.reshape(DEPTH, 1, N_MOD * D_MODEL))
    return out.reshape(DEPTH, 8, N_MOD, D_MODEL)


def _in_kernel(*refs, combine):
    it = iter(refs)
    if combine:
        dest_ref, x_ref, y_hbm, wc_ref, pmod_ref = (next(it) for _ in range(5))
    else:
        xc_ref, xl_ref = next(it), next(it)
    mod_ref, g_ref, w_ref, gn_ref, ws_ref, bs_ref = (next(it) for _ in range(6))
    xo_ref = next(it)
    outs = {name: next(it) for name, _ in Z_OUTPUTS}
    if combine:
        ybuf, sem = next(it), next(it)

    if combine:
        wc = wc_ref[...]
        y0, y1 = _gathered_expert_rows(dest_ref, y_hbm, ybuf, sem)
        x = x_ref[...] + pmod_ref[5:6, :] * (wc[:, 0:1] * y0 + wc[:, 1:2] * y1)
    else:
        x = jnp.where(pl.program_id(0) >= T_CTX // ROW_TILE, xl_ref[...], xc_ref[...])
    xo_ref[...] = x
    h = (_rms(x, g_ref[...]) * (1.0 + mod_ref[1:2, :]) + mod_ref[0:1, :]).astype(BF16)
    col = 0
    kept = {}
    for name, width in Z_SPLITS:
        for c in range(0, width, DOT_CHUNK):
            cw = min(DOT_CHUNK, width - c)
            r = jnp.dot(h, w_ref[:, col + c:col + c + cw], preferred_element_type=F32)
            if name in ("mu", "mv"):
                kept[name] = r
            elif name in ("ga", "gb", "gc"):
                outs[name][:, c:c + cw] = _sigmoid(r).astype(BF16)
            elif name == "gr":
                outs[name][:, c:c + cw] = (r * _sigmoid(r)).astype(BF16)
            else:
                outs[name][:, c:c + cw] = r.astype(BF16)
        col += width

    u = jax.nn.gelu(kept["mu"], approximate=True)
    vn = _rms(jax.nn.gelu(kept["mv"], approximate=True), gn_ref[...]).astype(BF16)
    gc = MLP_W // N_MLP_GROUPS
    for n in range(ROW_TILE // BLOCK):
        rows = slice(n * BLOCK, (n + 1) * BLOCK)
        for g in range(N_MLP_GROUPS):
            cols = slice(g * gc, (g + 1) * gc)
            f = jnp.dot(ws_ref[g], vn[rows, cols], preferred_element_type=F32) + bs_ref[g]
            outs["cm"][rows, cols] = (u[rows, cols] * f).astype(BF16)
    if combine:
        _finish_expert_rows(y_hbm, ybuf, sem)


def _input_projection(l, x, mods, norm1, w_in_bf, mlp_norm, w_spatial_bf, b_spatial,
                      moe_out=None, dest_rows=None, wcol=None):
    combine = moe_out is not None
    tm = ROW_TILE
    nt = T_ALL // tm
    n_ctx = T_CTX // tm
    row = lambda i, *_: (i, 0)
    mod_spec = lambda ll: pl.BlockSpec((None, None, N_MOD, D_MODEL), lambda i, *_: (ll, _mod_row(i), 0, 0))
    if combine:
        in_specs = [pl.BlockSpec((tm, D_MODEL), row),
                    pl.BlockSpec(memory_space=pl.ANY),
                    pl.BlockSpec((tm, TOP_K), row),
                    mod_spec(l - 1)]
        args = [dest_rows, x, moe_out, wcol, mods]
    else:
        in_specs = [pl.BlockSpec((tm, D_MODEL), lambda i: (jnp.minimum(i, n_ctx - 1), 0)),
                    pl.BlockSpec((tm, D_MODEL), lambda i: (jnp.maximum(i - n_ctx, 0), 0))]
        args = list(x)
    in_specs += [mod_spec(l),
                 pl.BlockSpec((None, 1, D_MODEL), lambda i, *_: (l, 0, 0)),
                 pl.BlockSpec((None, D_MODEL, w_in_bf.shape[-1]), lambda i, *_: (l, 0, 0),
                              pipeline_mode=pl.Buffered(1)),
                 pl.BlockSpec((None, 1, MLP_W), lambda i, *_: (l, 0, 0)),
                 pl.BlockSpec((None, N_MLP_GROUPS, BLOCK, BLOCK), lambda i, *_: (l, 0, 0, 0)),
                 pl.BlockSpec((None, N_MLP_GROUPS, BLOCK, 1), lambda i, *_: (l, 0, 0, 0))]
    args += [mods, norm1.reshape(DEPTH, 1, D_MODEL), w_in_bf, mlp_norm.reshape(DEPTH, 1, MLP_W), w_spatial_bf,
             b_spatial.reshape(DEPTH, N_MLP_GROUPS, BLOCK, 1)]
    out_specs = [pl.BlockSpec((tm, D_MODEL), row)]
    out_shape = [jax.ShapeDtypeStruct((T_ALL, D_MODEL), F32)]
    for _, width in Z_OUTPUTS:
        out_specs.append(pl.BlockSpec((tm, width), row))
        out_shape.append(jax.ShapeDtypeStruct((T_ALL, width), BF16))
    grid_spec = pltpu.PrefetchScalarGridSpec(
        num_scalar_prefetch=1 if combine else 0,
        grid=(nt,),
        in_specs=in_specs, out_specs=out_specs,
        scratch_shapes=_expert_rows_scratch() if combine else [])
    res = pl.pallas_call(
        functools.partial(_in_kernel, combine=combine),
        grid_spec=grid_spec, out_shape=out_shape,
        compiler_params=_params(("arbitrary",), V7X_VMEM_LIMIT_BYTES),
        name="input_projection",
    )(*args)
    return res[0], {name: r for (name, _), r in zip(Z_OUTPUTS, res[1:])}


def _rope_tables(rows, dim, reps):
    r = jnp.repeat(jnp.arange(rows, dtype=F32), GRID_W)
    col = jnp.tile(jnp.arange(GRID_W, dtype=F32), rows)
    half = dim // 2
    inv = ROPE_BASE ** (-jnp.arange(0, half, 2, dtype=F32) / half)
    ar = r[:, None] * inv
    ac = col[:, None] * inv
    ang = jnp.concatenate([ar, ar, ac, ac], axis=-1)
    cos, sin = jnp.cos(ang), jnp.sin(ang)
    first = (jnp.arange(dim) % (dim // 2)) < (dim // 4)
    sin_up = jnp.where(first, -sin, 0.0)
    sin_dn = jnp.where(first, 0.0, sin)
    t = lambda a: jnp.tile(a, (1, reps))
    return t(cos), t(sin_up), t(sin_dn)


def _rope(x, cos, sin_up, sin_dn, quarter):
    w = x.shape[-1]
    return x * cos + pltpu.roll(x, w - quarter, 1) * sin_up + pltpu.roll(x, quarter, 1) * sin_dn


def _attn_kernel(*refs, sk_new, use_rope, has_cache, emit_kv):
    it = iter(refs)
    q_ref, kv_ref, gq_ref, gk_ref, bdq_ref, bdk_ref = (next(it) for _ in range(6))
    if use_rope:
        cq_ref, suq_ref, sdq_ref, ck_ref, suk_ref, sdk_ref = (next(it) for _ in range(6))
    if has_cache:
        kc_ref, vc_ref = next(it), next(it)
    o_ref = next(it)
    if emit_kv:
        nk_ref, nv_ref = next(it), next(it)
    kd_scr, vd_scr = next(it), next(it)

    lo = lax.broadcasted_iota(jnp.int32, (1, 2 * HEAD_DIM), 1) < HEAD_DIM

    def dup_halves(a):
        r = pltpu.roll(a, HEAD_DIM, 1)
        return jnp.where(lo, a, r), jnp.where(lo, r, a)

    @pl.when(pl.program_id(1) == 0)
    def _prep():
        kv = kv_ref[...].astype(F32)
        k, v = kv[:, :ATT_KV], kv[:, ATT_KV:]
        ms = jnp.dot((k * k).astype(BF16), bdk_ref[...], preferred_element_type=F32)
        kn = k * lax.rsqrt(ms + EPS) * gk_ref[...]
        if emit_kv:
            nk_ref[...] = kn
            nv_ref[...] = v
        if use_rope:
            kn = _rope(kn, ck_ref[...], suk_ref[...], sdk_ref[...], HEAD_DIM // 4)
        k0, k1 = dup_halves(kn)
        v0, v1 = dup_halves(v)
        kd_scr[0, 0:sk_new, :] = k0.astype(BF16)
        kd_scr[1, 0:sk_new, :] = k1.astype(BF16)
        vd_scr[0, 0:sk_new, :] = v0.astype(BF16)
        vd_scr[1, 0:sk_new, :] = v1.astype(BF16)
        if has_cache:
            c0, c1 = dup_halves(kc_ref[...])
            d0, d1 = dup_halves(vc_ref[...])
            kd_scr[0, sk_new:, :] = c0.astype(BF16)
            kd_scr[1, sk_new:, :] = c1.astype(BF16)
            vd_scr[0, sk_new:, :] = d0.astype(BF16)
            vd_scr[1, sk_new:, :] = d1.astype(BF16)

    q = q_ref[...].astype(F32)
    ms = jnp.dot((q * q).astype(BF16), bdq_ref[...], preferred_element_type=F32)
    qn = q * lax.rsqrt(ms + EPS) * gq_ref[...]
    if use_rope:
        qn = _rope(qn, cq_ref[...], suq_ref[...], sdq_ref[...], HEAD_DIM // 4)
    qn = qn * (HEAD_DIM ** -0.5)
    heads_per_kv = N_Q_HEADS // N_KV_HEADS
    for j in range(N_Q_HEADS // 2):
        grp = (2 * j) // heads_per_kv
        kd, vd = kd_scr[grp], vd_scr[grp]
        qp = qn[:, 2 * HEAD_DIM * j:2 * HEAD_DIM * (j + 1)]
        halves = []
        for qm in (jnp.where(lo, qp, 0.0), jnp.where(lo, 0.0, qp)):
            s = lax.dot_general(qm.astype(BF16), kd, (((1,), (1,)), ((), ())), preferred_element_type=F32)
            e = jnp.exp(s - jnp.max(s, axis=-1, keepdims=True))
            den = jnp.sum(e, axis=-1, keepdims=True)
            halves.append(jnp.dot(e.astype(BF16), vd, preferred_element_type=F32) / den)
        o_ref[:, 2 * HEAD_DIM * j:2 * HEAD_DIM * (j + 1)] = jnp.where(lo, halves[0], halves[1]).astype(BF16)


def _block_diag_mean(width, group):
    idx = np.arange(width) // group
    return jnp.asarray((idx[:, None] == idx[None, :]).astype(np.float32) / group, dtype=BF16)


def _attention(l, z, q_norm, k_norm, *, latent, rope=None, cache_k=None, cache_v=None):
    if latent:
        nb, s, tq, row0 = DEC_BATCH, DEC_SEQ, 256, T_CTX
    else:
        nb, s, tq, row0 = BATCH, SEQ, SEQ, 0
    nq = s // tq
    sk = s + (PAST_LEN if latent else 0)
    qrow = lambda b, qi: (row0 // tq + b * nq + qi, 0)
    krow = lambda b, qi: (row0 // s + b, 0)
    const = lambda b, qi: (0, 0)
    lrow = lambda b, qi: (l, 0, 0)
    in_specs = [pl.BlockSpec((tq, ATT_Q), qrow),
                pl.BlockSpec((s, 2 * ATT_KV), krow),
                pl.BlockSpec((None, 1, ATT_Q), lrow),
                pl.BlockSpec((None, 1, ATT_KV), lrow),
                pl.BlockSpec((ATT_Q, ATT_Q), const),
                pl.BlockSpec((ATT_KV, ATT_KV), const)]
    args = [z["qa"], z["kv"],
            jnp.tile(q_norm, (1, N_Q_HEADS)).reshape(DEPTH, 1, ATT_Q),
            jnp.tile(k_norm, (1, N_KV_HEADS)).reshape(DEPTH, 1, ATT_KV),
            _block_diag_mean(ATT_Q, HEAD_DIM), _block_diag_mean(ATT_KV, HEAD_DIM)]
    if latent:
        cq, suq, sdq, ck, suk, sdk = rope
        in_specs += [pl.BlockSpec((tq, ATT_Q), lambda b, qi: (qi, 0))] * 3
        in_specs += [pl.BlockSpec((s, ATT_KV), const)] * 3
        in_specs += [pl.BlockSpec((None, None, PAST_LEN, ATT_KV), lambda b, qi: (b, l, 0, 0))] * 2
        args += [cq, suq, sdq, ck, suk, sdk,
                 cache_k.reshape(DEC_BATCH, DEPTH, PAST_LEN, ATT_KV),
                 cache_v.reshape(DEC_BATCH, DEPTH, PAST_LEN, ATT_KV)]
    rows = nb * s
    out_specs = [pl.BlockSpec((tq, ATT_Q), lambda b, qi: (b * nq + qi, 0))]
    out_shape = [jax.ShapeDtypeStruct((rows, ATT_Q), BF16)]
    if not latent:
        out_specs += [pl.BlockSpec((s, ATT_KV), lambda b, qi: (b, 0))] * 2
        out_shape += [jax.ShapeDtypeStruct((rows, ATT_KV), F32)] * 2
    return pl.pallas_call(
        functools.partial(_attn_kernel, sk_new=s, use_rope=latent, has_cache=latent, emit_kv=not latent),
        grid=(nb, nq),
        in_specs=in_specs, out_specs=out_specs, out_shape=out_shape,
        scratch_shapes=[pltpu.VMEM((N_KV_HEADS, sk, 2 * HEAD_DIM), BF16),
                        pltpu.VMEM((N_KV_HEADS, sk, 2 * HEAD_DIM), BF16)],
        compiler_params=_params(("arbitrary", "arbitrary"), V7X_VMEM_LIMIT_BYTES),
        name="attention_latent" if latent else "attention_context",
    )(*args)


def _ret_kernel(*refs, s, heads, use_rope, has_state):
    it = iter(refs)
    q_ref, k_ref, v_ref, g_ref, dec_ref, gn_ref = (next(it) for _ in range(6))
    rope = (next(it)[...], next(it)[...], next(it)[...]) if use_rope else None
    s0f_ref, s0b_ref = (next(it), next(it)) if has_state else (None, None)
    o_ref, sf_ref, sb_ref, oacc = next(it), next(it), next(it), next(it)
    for hh in range(heads):
        cols = slice(hh * RET_DK, (hh + 1) * RET_DK)
        _ret_head(q_ref.at[:, cols], k_ref.at[:, cols], v_ref.at[:, cols], g_ref.at[:, cols],
                  dec_ref.at[:, hh], gn_ref.at[hh], rope,
                  s0f_ref.at[hh] if has_state else None, s0b_ref.at[hh] if has_state else None,
                  o_ref.at[:, cols], sf_ref.at[hh], sb_ref.at[hh], oacc.at[hh], s)


def _ret_head(q_ref, k_ref, v_ref, g_ref, dec_ref, gn_ref, rope, s0f_ref, s0b_ref, o_ref, sf_ref, sb_ref, oacc, s):
    use_rope, has_state = rope is not None, s0f_ref is not None
    n_chunks = s // BLOCK
    lgf = -jnp.exp(dec_ref[0])
    lgb = -jnp.exp(dec_ref[1])
    lgf1, lgb1 = lgf[:, 0:1], lgb[:, 0:1]
    diff = (lax.broadcasted_iota(jnp.int32, (BLOCK, BLOCK), 0)
            - lax.broadcasted_iota(jnp.int32, (BLOCK, BLOCK), 1)).astype(F32)
    dsum = (jnp.where(diff >= 0, jnp.exp(diff * lgf), 0.0)
            + jnp.where(diff <= 0, jnp.exp(-diff * lgb), 0.0))
    ic = lax.broadcasted_iota(jnp.int32, (BLOCK, 1), 0).astype(F32)
    qdf, kdf, cdf = jnp.exp((ic + 1.0) * lgf1), jnp.exp((BLOCK - 1.0 - ic) * lgf1), jnp.exp(BLOCK * lgf1)
    qdb, kdb, cdb = jnp.exp((BLOCK - ic) * lgb1), jnp.exp(ic * lgb1), jnp.exp(BLOCK * lgb1)

    q = q_ref[...].astype(F32)
    k = k_ref[...].astype(F32) * (RET_DK ** -0.5)
    if use_rope:
        q = _rope(q, *rope, RET_DK // 4)
        k = _rope(k, *rope, RET_DK // 4)
    v = v_ref[...]

    def rows(a, n):
        return a[n * BLOCK:(n + 1) * BLOCK]

    def state_step(state, kn, kdec, vn, cdec):
        kd_t = jnp.transpose(kn * kdec).astype(BF16)
        return cdec * state + jnp.dot(kd_t, vn, preferred_element_type=F32)

    state = s0f_ref[...] if has_state else jnp.zeros((RET_DK, RET_DV), F32)
    for n in range(n_chunks):
        qn, kn, vn = rows(q, n), rows(k, n), rows(v, n)
        inner = lax.dot_general(qn.astype(BF16), kn.astype(BF16), (((1,), (1,)), ((), ())),
                                preferred_element_type=F32) * dsum
        o = (jnp.dot(inner.astype(BF16), vn, preferred_element_type=F32)
             + jnp.dot((qn * qdf).astype(BF16), state.astype(BF16), preferred_element_type=F32))
        state = state_step(state, kn, kdf, vn, cdf)
        oacc[n * BLOCK:(n + 1) * BLOCK, :] = o
    sf_ref[...] = state

    state = s0b_ref[...] if has_state else jnp.zeros((RET_DK, RET_DV), F32)
    for n in reversed(range(n_chunks)):
        qn, kn, vn = rows(q, n), rows(k, n), rows(v, n)
        o = rows(oacc, n) + jnp.dot((qn * qdb).astype(BF16), state.astype(BF16), preferred_element_type=F32)
        state = state_step(state, kn, kdb, vn, cdb)
        mu = jnp.mean(o, axis=-1, keepdims=True)
        var = jnp.mean(jnp.square(o - mu), axis=-1, keepdims=True)
        y = (o - mu) * lax.rsqrt(var + EPS) * gn_ref[...]
        gate = g_ref[n * BLOCK:(n + 1) * BLOCK, :].astype(F32)
        o_ref[n * BLOCK:(n + 1) * BLOCK, :] = (y * gate).astype(BF16)
    sb_ref[...] = state


def _retention(l, z, dec, ret_gn, *, latent, rope=None, s0f=None, s0b=None):
    if latent:
        nb, s, row0, heads = DEC_BATCH, DEC_SEQ, T_CTX, 1
    else:
        nb, s, row0, heads = BATCH, SEQ, 0, N_RET_HEADS
    width = heads * RET_DK
    hrow = lambda b, h: (row0 // s + b, h)
    in_specs = [pl.BlockSpec((s, width), hrow)] * 4
    in_specs += [pl.BlockSpec((None, 2, heads, 1, RET_DK), lambda b, h: (l, 0, h, 0, 0)),
                 pl.BlockSpec((None, heads, 1, RET_DV), lambda b, h: (l, h, 0, 0))]
    args = [z["qr"], z["kr"], z["vr"], z["gr"], dec, ret_gn.reshape(DEPTH, N_RET_HEADS, 1, RET_DV)]
    if latent:
        in_specs += [pl.BlockSpec((s, RET_DK), lambda b, h: (0, 0))] * 3
        in_specs += [pl.BlockSpec((None, None, heads, RET_DK, RET_DV), lambda b, h: (b, l, h, 0, 0))] * 2
        args += [*rope, s0f, s0b]
    st_spec = pl.BlockSpec((None, heads, RET_DK, RET_DV), lambda b, h: (b, h, 0, 0))
    st_shape = jax.ShapeDtypeStruct((nb, N_RET_HEADS, RET_DK, RET_DV), F32)
    return pl.pallas_call(
        functools.partial(_ret_kernel, s=s, heads=heads, use_rope=latent, has_state=latent),
        grid=(nb, N_RET_HEADS // heads),
        in_specs=in_specs,
        out_specs=[pl.BlockSpec((s, width), lambda b, h: (b, h)), st_spec, st_spec],
        out_shape=[jax.ShapeDtypeStruct((nb * s, RET_W), BF16), st_shape, st_shape],
        scratch_shapes=[pltpu.VMEM((heads, s, RET_DV), F32)],
        compiler_params=_params(("arbitrary", "arbitrary")),
        name="retention_latent" if latent else "retention_context",
    )(*args)


def _first_max(vals):
    best, idx = vals[0], jnp.zeros(vals[0].shape, jnp.int32)
    for j in range(1, len(vals)):
        upd = vals[j] > best
        best = jnp.where(upd, vals[j], best)
        idx = jnp.where(upd, j, idx)
    return best, idx


def _pick(idx, vals):
    out = vals[-1]
    for j in range(len(vals) - 2, -1, -1):
        out = jnp.where(idx == j, vals[j], out)
    return out


def _merge_kernel(x_ref, attc_ref, attl_ref, retc_ref, retl_ref, cm_ref, ga_ref, gb_ref, gc_ref, mod_ref, g2_ref,
                  wa_ref, wr_ref, wm_ref, wo_ref, wrt_ref, br_ref, tri_ref,
                  x1_ref, h2_ref, eid_ref, wts_ref, rank_ref, cnt_ref, cnt_scr):
    @pl.when(pl.program_id(0) == 0)
    def _zero_counts():
        cnt_scr[...] = jnp.zeros_like(cnt_scr)

    is_lat = pl.program_id(0) >= T_CTX // ROW_TILE
    att = jnp.where(is_lat, attl_ref[...], attc_ref[...])
    ret = jnp.where(is_lat, retl_ref[...], retc_ref[...])
    gate = lambda r: r[...].astype(F32)
    dot = lambda a, b: jnp.dot(a, b, preferred_element_type=F32)
    mix = (gate(ga_ref) * dot(att, wa_ref[...])
           + gate(gb_ref) * dot(ret, wr_ref[...])
           + gate(gc_ref) * dot(cm_ref[...], wm_ref[...]))
    x1 = x_ref[...] + mod_ref[2:3, :] * dot(mix.astype(BF16), wo_ref[...])
    x1_ref[...] = x1
    h2 = _rms(x1, g2_ref[...]) * (1.0 + mod_ref[4:5, :]) + mod_ref[3:4, :]
    _store_row_tiles(h2_ref, h2)

    logits = lax.dot_general(wrt_ref[...], h2, (((1,), (1,)), ((), ())), preferred_element_type=F32,
                             precision=lax.Precision.HIGHEST)
    score = _sigmoid(logits)
    sel = score + br_ref[...]
    sel_rows = [sel[e:e + 1, :] for e in range(N_EXPERTS)]
    score_rows = [score[e:e + 1, :] for e in range(N_EXPERTS)]
    group_scores = []
    for g in range(N_EXPERT_GROUPS):
        v = sel_rows[g * EXPERTS_PER_GROUP:(g + 1) * EXPERTS_PER_GROUP]
        pair_sums = [v[a] + v[b] for a in range(EXPERTS_PER_GROUP) for b in range(a + 1, EXPERTS_PER_GROUP)]
        group_scores.append(functools.reduce(jnp.maximum, pair_sums))
    _, gidx = _first_max(group_scores)
    in_sel = [_pick(gidx, [sel_rows[g * EXPERTS_PER_GROUP + j] for g in range(N_EXPERT_GROUPS)])
              for j in range(EXPERTS_PER_GROUP)]
    in_score = [_pick(gidx, [score_rows[g * EXPERTS_PER_GROUP + j] for g in range(N_EXPERT_GROUPS)])
                for j in range(EXPERTS_PER_GROUP)]
    _, loc0 = _first_max(in_sel)
    _, loc1 = _first_max([jnp.where(loc0 == j, -jnp.inf, in_sel[j]) for j in range(EXPERTS_PER_GROUP)])
    w0, w1 = _pick(loc0, in_score), _pick(loc1, in_score)
    den = w0 + w1
    e0 = gidx * EXPERTS_PER_GROUP + loc0
    e1 = gidx * EXPERTS_PER_GROUP + loc1
    eid_ref[0:1, :] = e0
    eid_ref[1:2, :] = e1
    wts_ref[0:1, :] = w0 / den
    wts_ref[1:2, :] = w1 / den

    eio = lax.broadcasted_iota(jnp.int32, score.shape, 0)
    oh0 = (eio == e0).astype(F32)
    oh1 = (eio == e1).astype(F32)
    both = oh0 + oh1
    before = jnp.dot(both.astype(BF16), tri_ref[...], preferred_element_type=F32) + cnt_scr[...]
    rank_ref[0:1, :] = jnp.sum(oh0 * before, axis=0, keepdims=True).astype(jnp.int32)
    rank_ref[1:2, :] = jnp.sum(oh1 * before, axis=0, keepdims=True).astype(jnp.int32)
    cnt_scr[...] = cnt_scr[...] + jnp.sum(both, axis=1, keepdims=True)
    cnt_ref[...] = cnt_scr[...]


def _merge(l, x, att_c, att_l, ret_c, ret_l, cm, z, mods, norm2, wa, wr, wm, wo, w_router_t, b_router):
    tm = ROW_TILE
    row = lambda i: (i, 0)
    n_ctx = T_CTX // tm
    ctx_row = lambda i: (jnp.minimum(i, n_ctx - 1), 0)
    lat_row = lambda i: (jnp.maximum(i - n_ctx, 0), 0)
    lw = lambda k, n: pl.BlockSpec((None, k, n), lambda i: (l, 0, 0))
    in_specs = [pl.BlockSpec((tm, D_MODEL), row),
                pl.BlockSpec((tm, ATT_Q), ctx_row), pl.BlockSpec((tm, ATT_Q), lat_row),
                pl.BlockSpec((tm, RET_W), ctx_row), pl.BlockSpec((tm, RET_W), lat_row),
                pl.BlockSpec((tm, MLP_W), row),
                pl.BlockSpec((tm, D_MODEL), row), pl.BlockSpec((tm, D_MODEL), row), pl.BlockSpec((tm, D_MODEL), row),
                pl.BlockSpec((None, None, N_MOD, D_MODEL), lambda i: (l, _mod_row(i), 0, 0)),
                pl.BlockSpec((None, 1, D_MODEL), lambda i: (l, 0, 0)),
                lw(ATT_Q, D_MODEL), lw(RET_W, D_MODEL), lw(MLP_W, D_MODEL), lw(D_MODEL, D_MODEL),
                pl.BlockSpec((N_EXPERTS, D_MODEL), lambda i: (0, 0)),
                pl.BlockSpec((N_EXPERTS, 1), lambda i: (0, 0)),
                pl.BlockSpec((tm, tm), lambda i: (0, 0))]
    lane = lambda i: (0, i)
    tri = jnp.asarray(np.triu(np.ones((tm, tm), np.float32), k=1), dtype=BF16)
    return pl.pallas_call(
        _merge_kernel,
        grid=(T_ALL // tm,),
        in_specs=in_specs,
        out_specs=[pl.BlockSpec((tm, D_MODEL), row), pl.BlockSpec((tm * TILE_ROWS, 128), row),
                   pl.BlockSpec((TOP_K, tm), lane), pl.BlockSpec((TOP_K, tm), lane),
                   pl.BlockSpec((TOP_K, tm), lane), pl.BlockSpec((N_EXPERTS, 1), lambda i: (0, 0))],
        out_shape=[jax.ShapeDtypeStruct((T_ALL, D_MODEL), F32), jax.ShapeDtypeStruct((T_ALL * TILE_ROWS, 128), F32),
                   jax.ShapeDtypeStruct((TOP_K, T_ALL), jnp.int32), jax.ShapeDtypeStruct((TOP_K, T_ALL), F32),
                   jax.ShapeDtypeStruct((TOP_K, T_ALL), jnp.int32), jax.ShapeDtypeStruct((N_EXPERTS, 1), F32)],
        scratch_shapes=[pltpu.VMEM((N_EXPERTS, 1), F32)],
        compiler_params=_params(("arbitrary",), V7X_VMEM_LIMIT_BYTES),
        name="merge_router",
    )(x, att_c, att_l, ret_c, ret_l, cm, z["ga"], z["gb"], z["gc"], mods, norm2.reshape(DEPTH, 1, D_MODEL),
      wa, wr, wm, wo, w_router_t, b_router.reshape(N_EXPERTS, 1), tri)


def _dispatch_plan(eid, rank, counts):
    expert = jnp.arange(N_EXPERTS, dtype=jnp.int32)
    padded = (counts + MOE_BLOCK - 1) // MOE_BLOCK * MOE_BLOCK
    pend = jnp.cumsum(padded)
    pstart = pend - padded
    dest = rank + jnp.sum(jnp.where(eid[..., None] == expert, pstart, 0), axis=-1)
    block_start = jnp.arange(N_MOE_BLOCKS, dtype=jnp.int32) * MOE_BLOCK
    block_e = jnp.minimum(jnp.sum((pend[None, :] <= block_start[:, None]).astype(jnp.int32), axis=1),
                          N_EXPERTS - 1)
    pad_ranges = jnp.stack([jnp.append(pstart + counts, pend[-1]),
                            jnp.append(pend, N_SLOTS)]).astype(jnp.int32)
    return block_e.astype(jnp.int32), pad_ranges, dest.reshape(N_ASSIGN)


ROWS_PER_TRIP = 8


N_DMA_PRIORITIES = 2


def _moe_kernel(be_ref, pad_ref, dest_ref, h_ref, wg_ref, wu_ref, wd_ref, o_ref, xbuf, wg_s, wu_s, wd_s, src_s):
    i = pl.program_id(0)
    last = pl.num_programs(0) - 1

    def gather(blk, buf):
        for r in range(MOE_BLOCK):
            first = pl.multiple_of(src_s[blk * MOE_BLOCK + r], TILE_ROWS)
            xbuf[buf, r * TILE_ROWS:(r + 1) * TILE_ROWS, :] = h_ref[pl.ds(first, TILE_ROWS), :]

    @pl.when(i == 0)
    def _init():
        def clear(j, c):
            src_s[j] = 0
            return c
        for e in range(N_EXPERTS + 1):
            lax.fori_loop(pad_ref[0, e], pad_ref[1, e], clear, 0)

        def put(g, c):
            for u in range(ROWS_PER_TRIP):
                t = g * ROWS_PER_TRIP + u
                for k in range(TOP_K):
                    src_s[dest_ref[k * T_ALL + t]] = t * TILE_ROWS
            return c
        lax.fori_loop(0, T_ALL // ROWS_PER_TRIP, put, 0)
        gather(0, 0)

    prev = be_ref[jnp.maximum(i - 1, 0)]
    used = i * MOE_BLOCK < pad_ref[0, N_EXPERTS]

    @pl.when(used & ((i == 0) | (be_ref[i] != prev)))
    def _cast():
        wg_s[...] = wg_ref[...].astype(BF16)
        wu_s[...] = wu_ref[...].astype(BF16)
        wd_s[...] = wd_ref[...].astype(BF16)

    @pl.when(used)
    def _block():
        buf = lax.rem(i, 2)
        gather(jnp.minimum(i + 1, last), 1 - buf)
        x = _load_row_tiles(xbuf.at[buf]).astype(BF16)
        a = jnp.dot(x, wg_s[...], preferred_element_type=F32)
        b = jnp.dot(x, wu_s[...], preferred_element_type=F32)
        mid = ((a * _sigmoid(a)) * b).astype(BF16)
        _store_row_tiles(o_ref, jnp.dot(mid, wd_s[...], preferred_element_type=F32))

    @pl.when(jnp.logical_not(used))
    def _unused():
        o_ref[...] = jnp.zeros_like(o_ref)


def _moe(l, h2, plan, w_gate, w_up, w_down):
    block_e, pad_ranges, dest = plan
    once = pl.Buffered(1)
    wspec = lambda k, n: pl.BlockSpec((None, None, k, n), lambda i, be, pr, de: (l, be[i], 0, 0),
                                      pipeline_mode=once)
    grid_spec = pltpu.PrefetchScalarGridSpec(
        num_scalar_prefetch=3,
        grid=(N_MOE_BLOCKS,),
        in_specs=[pl.BlockSpec((T_ALL * TILE_ROWS, 128), lambda i, be, pr, de: (0, 0), pipeline_mode=once),
                  wspec(D_MODEL, EXPERT_FF), wspec(D_MODEL, EXPERT_FF), wspec(EXPERT_FF, D_MODEL)],
        out_specs=pl.BlockSpec((MOE_BLOCK * TILE_ROWS, 128), lambda i, be, pr, de: (i, 0)),
        scratch_shapes=[pltpu.VMEM((2, MOE_BLOCK * TILE_ROWS, 128), F32),
                        pltpu.VMEM((D_MODEL, EXPERT_FF), BF16), pltpu.VMEM((D_MODEL, EXPERT_FF), BF16),
                        pltpu.VMEM((EXPERT_FF, D_MODEL), BF16),
                        pltpu.SMEM((N_SLOTS,), jnp.int32)])
    return pl.pallas_call(
        _moe_kernel,
        grid_spec=grid_spec,
        out_shape=jax.ShapeDtypeStruct((N_SLOTS * TILE_ROWS, 128), F32),
        compiler_params=_params(("arbitrary",), V7X_MOE_VMEM_LIMIT_BYTES),
        name="moe_experts",
    )(block_e, pad_ranges, dest, h2, w_gate, w_up, w_down)


def _start_expert_rows(dest_ref, y_hbm, ybuf, sem, tile_idx, slot):
    base = tile_idx * ROW_TILE
    for k in range(TOP_K):
        for t in range(ROW_TILE):
            first = pl.multiple_of(dest_ref[k * T_ALL + base + t], TILE_ROWS)
            pltpu.make_async_copy(y_hbm.at[pl.ds(first, TILE_ROWS)],
                                  ybuf.at[slot, k, pl.ds(t * TILE_ROWS, TILE_ROWS)],
                                  sem.at[slot]).start(priority=t % N_DMA_PRIORITIES)


def _wait_expert_rows(y_hbm, ybuf, sem, slot):
    for k in range(TOP_K):
        pltpu.make_async_copy(y_hbm.at[pl.ds(0, ROW_TILE * TILE_ROWS)], ybuf.at[slot, k], sem.at[slot]).wait()


def _gathered_expert_rows(dest_ref, y_hbm, ybuf, sem):
    i = pl.program_id(0)
    last = pl.num_programs(0) - 1
    slot = lax.rem(i, 2)

    @pl.when(i == 0)
    def _first():
        _start_expert_rows(dest_ref, y_hbm, ybuf, sem, 0, 0)

    _start_expert_rows(dest_ref, y_hbm, ybuf, sem, jnp.minimum(i + 1, last), 1 - slot)
    _wait_expert_rows(y_hbm, ybuf, sem, slot)
    return _load_row_tiles(ybuf.at[slot, 0]), _load_row_tiles(ybuf.at[slot, 1])


def _finish_expert_rows(y_hbm, ybuf, sem):
    i = pl.program_id(0)

    @pl.when(i == pl.num_programs(0) - 1)
    def _():
        _wait_expert_rows(y_hbm, ybuf, sem, 1 - lax.rem(i, 2))


def _combine_kernel(dest_ref, x_ref, y_hbm, wc_ref, mod_ref, oc_ref, ol_ref, ybuf, sem):
    wc = wc_ref[...]
    y0, y1 = _gathered_expert_rows(dest_ref, y_hbm, ybuf, sem)
    x = x_ref[...] + mod_ref[5:6, :] * (wc[:, 0:1] * y0 + wc[:, 1:2] * y1)
    _finish_expert_rows(y_hbm, ybuf, sem)
    is_lat = pl.program_id(0) >= T_CTX // ROW_TILE

    @pl.when(jnp.logical_not(is_lat))
    def _():
        oc_ref[...] = x

    @pl.when(is_lat)
    def _():
        ol_ref[...] = x


def _expert_rows_scratch():
    return [pltpu.VMEM((2, TOP_K, ROW_TILE * TILE_ROWS, 128), F32), pltpu.SemaphoreType.DMA((2,))]


def _final_combine(l, x, moe_out, dest_rows, wcol, mods):
    tm = ROW_TILE
    nt = T_ALL // tm
    n_ctx = T_CTX // tm
    row = lambda i, de: (i, 0)
    grid_spec = pltpu.PrefetchScalarGridSpec(
        num_scalar_prefetch=1,
        grid=(nt,),
        in_specs=[pl.BlockSpec((tm, D_MODEL), row), pl.BlockSpec(memory_space=pl.ANY),
                  pl.BlockSpec((tm, TOP_K), row),
                  pl.BlockSpec((None, None, N_MOD, D_MODEL), lambda i, de: (l, _mod_row(i), 0, 0))],
        out_specs=[pl.BlockSpec((tm, D_MODEL), lambda i, de: (jnp.minimum(i, n_ctx - 1), 0)),
                   pl.BlockSpec((tm, D_MODEL), lambda i, de: (jnp.maximum(i - n_ctx, 0), 0))],
        scratch_shapes=_expert_rows_scratch())
    return pl.pallas_call(
        _combine_kernel,
        grid_spec=grid_spec,
        out_shape=[jax.ShapeDtypeStruct((T_CTX, D_MODEL), F32), jax.ShapeDtypeStruct((T_LAT, D_MODEL), F32)],
        compiler_params=_params(("arbitrary",)),
        name="final_combine",
    )(dest_rows, x, moe_out, wcol, mods)


def kernel(x_prompt, x_sample, cache_k, cache_v, state_ret_fwd, state_ret_bwd, c, c_ctx, w_mod, b_mod, norm1, norm2, w_in, q_norm, k_norm, ret_decay_fwd, ret_decay_bwd, ret_gn, mlp_norm, w_spatial, b_spatial, w_att_o, w_ret_o, w_mlp_o, w_out, w_router, b_router, w_gate, w_up, w_down):
    cond = jnp.zeros((8, D_MODEL), F32).at[0].set(c_ctx).at[1:1 + DEC_BATCH].set(c)
    mods = _modulation(cond, w_mod, b_mod)

    w_in_bf = w_in.astype(BF16)
    wa, wr, wm, wo = (w.astype(BF16) for w in (w_att_o, w_ret_o, w_mlp_o, w_out))
    ws_bf = w_spatial.astype(BF16)
    w_router_t = w_router.T
    dec = jnp.broadcast_to(jnp.stack([ret_decay_fwd, ret_decay_bwd], axis=1)[..., None, None],
                           (DEPTH, 2, N_RET_HEADS, 1, RET_DK))
    rows = DEC_SEQ // GRID_W
    rope_q = _rope_tables(rows, HEAD_DIM, N_Q_HEADS)
    rope_k = _rope_tables(rows, HEAD_DIM, N_KV_HEADS)
    rope_r = _rope_tables(rows, RET_DK, 1)

    x = (x_prompt.reshape(T_CTX, D_MODEL), x_sample.reshape(T_LAT, D_MODEL))
    moe_out, dest_rows, wcol = None, None, None
    ks, vs, sfs, sbs = [], [], [], []
    for l in range(DEPTH):
        x, z = _input_projection(l, x, mods, norm1, w_in_bf, mlp_norm, ws_bf, b_spatial, moe_out, dest_rows, wcol)

        att_c, k_l, v_l = _attention(l, z, q_norm, k_norm, latent=False)
        att_l, = _attention(l, z, q_norm, k_norm, latent=True, rope=rope_q + rope_k,
                            cache_k=cache_k, cache_v=cache_v)
        ret_c, sf_l, sb_l = _retention(l, z, dec, ret_gn, latent=False)
        ret_l, _, _ = _retention(l, z, dec, ret_gn, latent=True, rope=rope_r,
                                 s0f=state_ret_fwd, s0b=state_ret_bwd)
        x, h2, eid, wts, rank, cnt = _merge(l, x, att_c, att_l, ret_c, ret_l, z["cm"], z, mods, norm2,
                                            wa, wr, wm, wo, w_router_t, b_router)
        plan = _dispatch_plan(eid, rank, cnt[:, 0].astype(jnp.int32))
        moe_out = _moe(l, h2, plan, w_gate, w_up, w_down)
        dest_rows = plan[2] * TILE_ROWS
        wcol = wts.T
        ks.append(k_l.reshape(BATCH, SEQ, ATT_KV))
        vs.append(v_l.reshape(BATCH, SEQ, ATT_KV))
        sfs.append(sf_l)
        sbs.append(sb_l)
    y_ctx, y_lat = _final_combine(DEPTH - 1, x, moe_out, dest_rows, wcol, mods)

    y_prompt = y_ctx.reshape(BATCH, SEQ, D_MODEL)
    y_sample = y_lat.reshape(DEC_BATCH, DEC_SEQ, D_MODEL)
    new_k = jnp.stack(ks, axis=1).reshape(BATCH, DEPTH, SEQ, N_KV_HEADS, HEAD_DIM)
    new_v = jnp.stack(vs, axis=1).reshape(BATCH, DEPTH, SEQ, N_KV_HEADS, HEAD_DIM)
    new_sf = jnp.stack(sfs, axis=1)
    new_sb = jnp.stack(sbs, axis=1)
    return (y_prompt, y_sample, new_k, new_v, new_sf, new_sb)
```

```python
import functools

import numpy as np
import jax
import jax.numpy as jnp
from jax import lax
from jax.experimental import pallas as pl
from jax.experimental.pallas import tpu as pltpu

F32 = jnp.float32
BF16 = jnp.bfloat16

D_MODEL = 1024
BATCH = 32
SEQ = 256
DEPTH = 4
DEC_BATCH = 2
DEC_SEQ = 1024
PAST_LEN = 256
GRID_W = 64
BLOCK = 128
ROPE_BASE = 10000.0
EPS = 1e-6
N_Q_HEADS = 8
N_KV_HEADS = 2
HEAD_DIM = 64
ATT_Q = N_Q_HEADS * HEAD_DIM
ATT_KV = N_KV_HEADS * HEAD_DIM
N_RET_HEADS = 4
RET_DK = 128
RET_DV = 128
RET_W = N_RET_HEADS * RET_DK
N_MLP_GROUPS = 4
MLP_W = 512
N_MOD = 6
N_EXPERTS = 16
N_EXPERT_GROUPS = 4
EXPERTS_PER_GROUP = N_EXPERTS // N_EXPERT_GROUPS
TOP_K = 2
EXPERT_FF = 512
MOE_BLOCK = 256

T_CTX = BATCH * SEQ
T_LAT = DEC_BATCH * DEC_SEQ
T_ALL = T_CTX + T_LAT
N_ASSIGN = T_ALL * TOP_K
N_MOE_BLOCKS = -(-N_ASSIGN // MOE_BLOCK) + N_EXPERTS
N_SLOTS = N_MOE_BLOCKS * MOE_BLOCK

ROW_TILE = 512
V7X_VMEM_LIMIT_BYTES = 56 * 1024 * 1024
V7X_MOE_VMEM_LIMIT_BYTES = 60 * 1024 * 1024

Z_SPLITS = (("qa", ATT_Q), ("kv", 2 * ATT_KV), ("qr", RET_W), ("kr", RET_W), ("vr", RET_W), ("gr", RET_W),
            ("mu", MLP_W), ("mv", MLP_W), ("ga", D_MODEL), ("gb", D_MODEL), ("gc", D_MODEL))
DOT_CHUNK = 512
Z_OUTPUTS = tuple((n, w) for n, w in Z_SPLITS if n not in ("mu", "mv")) + (("cm", MLP_W),)


def _params(sem, vmem=None):
    return pltpu.CompilerParams(dimension_semantics=sem, vmem_limit_bytes=vmem)


def _rms(x, g):
    return x * lax.rsqrt(jnp.mean(x * x, axis=-1, keepdims=True) + EPS) * g


def _sigmoid(x):
    return jax.nn.sigmoid(x)


TILE_ROWS = D_MODEL // 128


def _store_row_tiles(ref, val):
    rows = val.shape[0]
    for c in range(TILE_ROWS):
        ref[pl.ds(c, rows, stride=TILE_ROWS), :] = val[:, c * 128:(c + 1) * 128]


def _load_row_tiles(ref):
    rows = ref.shape[0] // TILE_ROWS
    return jnp.concatenate([ref[pl.ds(c, rows, stride=TILE_ROWS), :] for c in range(TILE_ROWS)], axis=1)


def _mod_row(i):
    first_lat = T_CTX // ROW_TILE
    per_batch = DEC_SEQ // ROW_TILE
    return jnp.where(i < first_lat, 0, 1 + (i - first_lat) // per_batch)


def _mod_kernel(cond_ref, w_ref, b_ref, o_ref):
    c = cond_ref[...]
    s = c * _sigmoid(c)
    o_ref[...] = jnp.dot(s, w_ref[...], preferred_element_type=F32,
                         precision=lax.Precision.HIGHEST) + b_ref[...]


def _modulation(cond, w_mod, b_mod):
    tn = D_MODEL
    out = pl.pallas_call(
        _mod_kernel,
        grid=(DEPTH, N_MOD),
        in_specs=[pl.BlockSpec((8, D_MODEL), lambda l, j: (0, 0)),
                  pl.BlockSpec((None, D_MODEL, tn), lambda l, j: (l, 0, j)),
                  pl.BlockSpec((None, 1, tn), lambda l, j: (l, 0, j))],
        out_specs=pl.BlockSpec((None, 8, tn), lambda l, j: (l, 0, j)),
        out_shape=jax.ShapeDtypeStruct((DEPTH, 8, N_MOD * D_MODEL), F32),
        compiler_params=_params(("arbitrary", "arbitrary")),
        name="modulation",
    )(cond, w_mod, b_mod.reshape(DEPTH, 1, N_MOD * D_MODEL))
    return out.reshape(DEPTH, 8, N_MOD, D_MODEL)


def _in_kernel(*refs, combine):
    it = iter(refs)
    if combine:
        dest_ref, x_ref, y_hbm, wc_ref, pmod_ref = (next(it) for _ in range(5))
    else:
        xc_ref, xl_ref = next(it), next(it)
    mod_ref, g_ref, w_ref, gn_ref, ws_ref, bs_ref = (next(it) for _ in range(6))
    xo_ref = next(it)
    outs = {name: next(it) for name, _ in Z_OUTPUTS}
    if combine:
        ybuf, sem = next(it), next(it)

    if combine:
        wc = wc_ref[...]
        y0, y1 = _gathered_expert_rows(dest_ref, y_hbm, ybuf, sem)
        x = x_ref[...] + pmod_ref[5:6, :] * (wc[:, 0:1] * y0 + wc[:, 1:2] * y1)
    else:
        x = jnp.where(pl.program_id(0) >= T_CTX // ROW_TILE, xl_ref[...], xc_ref[...])
    xo_ref[...] = x
    h = (_rms(x, g_ref[...]) * (1.0 + mod_ref[1:2, :]) + mod_ref[0:1, :]).astype(BF16)
    col = 0
    kept = {}
    for name, width in Z_SPLITS:
        for c in range(0, width, DOT_CHUNK):
            cw = min(DOT_CHUNK, width - c)
            r = jnp.dot(h, w_ref[:, col + c:col + c + cw], preferred_element_type=F32)
            if name in ("mu", "mv"):
                kept[name] = r
            elif name in ("ga", "gb", "gc"):
                outs[name][:, c:c + cw] = _sigmoid(r).astype(BF16)
            elif name == "gr":
                outs[name][:, c:c + cw] = (r * _sigmoid(r)).astype(BF16)
            else:
                outs[name][:, c:c + cw] = r.astype(BF16)
        col += width

    u = jax.nn.gelu(kept["mu"], approximate=True)
    vn = _rms(jax.nn.gelu(kept["mv"], approximate=True), gn_ref[...]).astype(BF16)
    gc = MLP_W // N_MLP_GROUPS
    for n in range(ROW_TILE // BLOCK):
        rows = slice(n * BLOCK, (n + 1) * BLOCK)
        for g in range(N_MLP_GROUPS):
            cols = slice(g * gc, (g + 1) * gc)
            f = jnp.dot(ws_ref[g], vn[rows, cols], preferred_element_type=F32) + bs_ref[g]
            outs["cm"][rows, cols] = (u[rows, cols] * f).astype(BF16)
    if combine:
        _finish_expert_rows(y_hbm, ybuf, sem)


def _input_projection(l, x, mods, norm1, w_in_bf, mlp_norm, w_spatial_bf, b_spatial,
                      moe_out=None, dest_rows=None, wcol=None):
    combine = moe_out is not None
    tm = ROW_TILE
    nt = T_ALL // tm
    n_ctx = T_CTX // tm
    row = lambda i, *_: (i, 0)
    mod_spec = lambda ll: pl.BlockSpec((None, None, N_MOD, D_MODEL), lambda i, *_: (ll, _mod_row(i), 0, 0))
    if combine:
        in_specs = [pl.BlockSpec((tm, D_MODEL), row),
                    pl.BlockSpec(memory_space=pl.ANY),
                    pl.BlockSpec((tm, TOP_K), row),
                    mod_spec(l - 1)]
        args = [dest_rows, x, moe_out, wcol, mods]
    else:
        in_specs = [pl.BlockSpec((tm, D_MODEL), lambda i: (jnp.minimum(i, n_ctx - 1), 0)),
                    pl.BlockSpec((tm, D_MODEL), lambda i: (jnp.maximum(i - n_ctx, 0), 0))]
        args = list(x)
    in_specs += [mod_spec(l),
                 pl.BlockSpec((None, 1, D_MODEL), lambda i, *_: (l, 0, 0)),
                 pl.BlockSpec((None, D_MODEL, w_in_bf.shape[-1]), lambda i, *_: (l, 0, 0),
                              pipeline_mode=pl.Buffered(1)),
                 pl.BlockSpec((None, 1, MLP_W), lambda i, *_: (l, 0, 0)),
                 pl.BlockSpec((None, N_MLP_GROUPS, BLOCK, BLOCK), lambda i, *_: (l, 0, 0, 0)),
                 pl.BlockSpec((None, N_MLP_GROUPS, BLOCK, 1), lambda i, *_: (l, 0, 0, 0))]
    args += [mods, norm1.reshape(DEPTH, 1, D_MODEL), w_in_bf, mlp_norm.reshape(DEPTH, 1, MLP_W), w_spatial_bf,
             b_spatial.reshape(DEPTH, N_MLP_GROUPS, BLOCK, 1)]
    out_specs = [pl.BlockSpec((tm, D_MODEL), row)]
    out_shape = [jax.ShapeDtypeStruct((T_ALL, D_MODEL), F32)]
    for _, width in Z_OUTPUTS:
        out_specs.append(pl.BlockSpec((tm, width), row))
        out_shape.append(jax.ShapeDtypeStruct((T_ALL, width), BF16))
    grid_spec = pltpu.PrefetchScalarGridSpec(
        num_scalar_prefetch=1 if combine else 0,
        grid=(nt,),
        in_specs=in_specs, out_specs=out_specs,
        scratch_shapes=_expert_rows_scratch() if combine else [])
    res = pl.pallas_call(
        functools.partial(_in_kernel, combine=combine),
        grid_spec=grid_spec, out_shape=out_shape,
        compiler_params=_params(("arbitrary",), V7X_VMEM_LIMIT_BYTES),
        name="input_projection",
    )(*args)
    return res[0], {name: r for (name, _), r in zip(Z_OUTPUTS, res[1:])}


def _rope_tables(rows, dim, reps):
    r = jnp.repeat(jnp.arange(rows, dtype=F32), GRID_W)
    col = jnp.tile(jnp.arange(GRID_W, dtype=F32), rows)
    half = dim // 2
    inv = ROPE_BASE ** (-jnp.arange(0, half, 2, dtype=F32) / half)
    ar = r[:, None] * inv
    ac = col[:, None] * inv
    ang = jnp.concatenate([ar, ar, ac, ac], axis=-1)
    cos, sin = jnp.cos(ang), jnp.sin(ang)
    first = (jnp.arange(dim) % (dim // 2)) < (dim // 4)
    sin_up = jnp.where(first, -sin, 0.0)
    sin_dn = jnp.where(first, 0.0, sin)
    t = lambda a: jnp.tile(a, (1, reps))
    return t(cos), t(sin_up), t(sin_dn)


def _rope(x, cos, sin_up, sin_dn, quarter):
    w = x.shape[-1]
    return x * cos + pltpu.roll(x, w - quarter, 1) * sin_up + pltpu.roll(x, quarter, 1) * sin_dn


def _attn_kernel(*refs, sk_new, use_rope, has_cache, emit_kv):
    it = iter(refs)
    q_ref, kv_ref, gq_ref, gk_ref, bdq_ref, bdk_ref = (next(it) for _ in range(6))
    if use_rope:
        cq_ref, suq_ref, sdq_ref, ck_ref, suk_ref, sdk_ref = (next(it) for _ in range(6))
    if has_cache:
        kc_ref, vc_ref = next(it), next(it)
    o_ref = next(it)
    if emit_kv:
        nk_ref, nv_ref = next(it), next(it)
    kd_scr, vd_scr = next(it), next(it)

    lo = lax.broadcasted_iota(jnp.int32, (1, 2 * HEAD_DIM), 1) < HEAD_DIM

    def dup_halves(a):
        r = pltpu.roll(a, HEAD_DIM, 1)
        return jnp.where(lo, a, r), jnp.where(lo, r, a)

    @pl.when(pl.program_id(1) == 0)
    def _prep():
        kv = kv_ref[...].astype(F32)
        k, v = kv[:, :ATT_KV], kv[:, ATT_KV:]
        ms = jnp.dot((k * k).astype(BF16), bdk_ref[...], preferred_element_type=F32)
        kn = k * lax.rsqrt(ms + EPS) * gk_ref[...]
        if emit_kv:
            nk_ref[...] = kn
            nv_ref[...] = v
        if use_rope:
            kn = _rope(kn, ck_ref[...], suk_ref[...], sdk_ref[...], HEAD_DIM // 4)
        k0, k1 = dup_halves(kn)
        v0, v1 = dup_halves(v)
        kd_scr[0, 0:sk_new, :] = k0.astype(BF16)
        kd_scr[1, 0:sk_new, :] = k1.astype(BF16)
        vd_scr[0, 0:sk_new, :] = v0.astype(BF16)
        vd_scr[1, 0:sk_new, :] = v1.astype(BF16)
        if has_cache:
            c0, c1 = dup_halves(kc_ref[...])
            d0, d1 = dup_halves(vc_ref[...])
            kd_scr[0, sk_new:, :] = c0.astype(BF16)
            kd_scr[1, sk_new:, :] = c1.astype(BF16)
            vd_scr[0, sk_new:, :] = d0.astype(BF16)
            vd_scr[1, sk_new:, :] = d1.astype(BF16)

    q = q_ref[...].astype(F32)
    ms = jnp.dot((q * q).astype(BF16), bdq_ref[...], preferred_element_type=F32)
    qn = q * lax.rsqrt(ms + EPS) * gq_ref[...]
    if use_rope:
        qn = _rope(qn, cq_ref[...], suq_ref[...], sdq_ref[...], HEAD_DIM // 4)
    qn = qn * (HEAD_DIM ** -0.5)
    heads_per_kv = N_Q_HEADS // N_KV_HEADS
    for j in range(N_Q_HEADS // 2):
        grp = (2 * j) // heads_per_kv
        kd, vd = kd_scr[grp], vd_scr[grp]
        qp = qn[:, 2 * HEAD_DIM * j:2 * HEAD_DIM * (j + 1)]
        halves = []
        for qm in (jnp.where(lo, qp, 0.0), jnp.where(lo, 0.0, qp)):
            s = lax.dot_general(qm.astype(BF16), kd, (((1,), (1,)), ((), ())), preferred_element_type=F32)
            e = jnp.exp(s - jnp.max(s, axis=-1, keepdims=True))
            den = jnp.sum(e, axis=-1, keepdims=True)
            halves.append(jnp.dot(e.astype(BF16), vd, preferred_element_type=F32) / den)
        o_ref[:, 2 * HEAD_DIM * j:2 * HEAD_DIM * (j + 1)] = jnp.where(lo, halves[0], halves[1]).astype(BF16)


def _block_diag_mean(width, group):
    idx = np.arange(width) // group
    return jnp.asarray((idx[:, None] == idx[None, :]).astype(np.float32) / group, dtype=BF16)


def _attention(l, z, q_norm, k_norm, *, latent, rope=None, cache_k=None, cache_v=None):
    if latent:
        nb, s, tq, row0 = DEC_BATCH, DEC_SEQ, 256, T_CTX
    else:
        nb, s, tq, row0 = BATCH, SEQ, SEQ, 0
    nq = s // tq
    sk = s + (PAST_LEN if latent else 0)
    qrow = lambda b, qi: (row0 // tq + b * nq + qi, 0)
    krow = lambda b, qi: (row0 // s + b, 0)
    const = lambda b, qi: (0, 0)
    lrow = lambda b, qi: (l, 0, 0)
    in_specs = [pl.BlockSpec((tq, ATT_Q), qrow),
                pl.BlockSpec((s, 2 * ATT_KV), krow),
                pl.BlockSpec((None, 1, ATT_Q), lrow),
                pl.BlockSpec((None, 1, ATT_KV), lrow),
                pl.BlockSpec((ATT_Q, ATT_Q), const),
                pl.BlockSpec((ATT_KV, ATT_KV), const)]
    args = [z["qa"], z["kv"],
            jnp.tile(q_norm, (1, N_Q_HEADS)).reshape(DEPTH, 1, ATT_Q),
            jnp.tile(k_norm, (1, N_KV_HEADS)).reshape(DEPTH, 1, ATT_KV),
            _block_diag_mean(ATT_Q, HEAD_DIM), _block_diag_mean(ATT_KV, HEAD_DIM)]
    if latent:
        cq, suq, sdq, ck, suk, sdk = rope
        in_specs += [pl.BlockSpec((tq, ATT_Q), lambda b, qi: (qi, 0))] * 3
        in_specs += [pl.BlockSpec((s, ATT_KV), const)] * 3
        in_specs += [pl.BlockSpec((None, None, PAST_LEN, ATT_KV), lambda b, qi: (b, l, 0, 0))] * 2
        args += [cq, suq, sdq, ck, suk, sdk,
                 cache_k.reshape(DEC_BATCH, DEPTH, PAST_LEN, ATT_KV),
                 cache_v.reshape(DEC_BATCH, DEPTH, PAST_LEN, ATT_KV)]
    rows = nb * s
    out_specs = [pl.BlockSpec((tq, ATT_Q), lambda b, qi: (b * nq + qi, 0))]
    out_shape = [jax.ShapeDtypeStruct((rows, ATT_Q), BF16)]
    if not latent:
        out_specs += [pl.BlockSpec((s, ATT_KV), lambda b, qi: (b, 0))] * 2
        out_shape += [jax.ShapeDtypeStruct((rows, ATT_KV), F32)] * 2
    return pl.pallas_call(
        functools.partial(_attn_kernel, sk_new=s, use_rope=latent, has_cache=latent, emit_kv=not latent),
        grid=(nb, nq),
        in_specs=in_specs, out_specs=out_specs, out_shape=out_shape,
        scratch_shapes=[pltpu.VMEM((N_KV_HEADS, sk, 2 * HEAD_DIM), BF16),
                        pltpu.VMEM((N_KV_HEADS, sk, 2 * HEAD_DIM), BF16)],
        compiler_params=_params(("arbitrary", "arbitrary"), V7X_VMEM_LIMIT_BYTES),
        name="attention_latent" if latent else "attention_context",
    )(*args)


def _ret_kernel(*refs, s, heads, use_rope, has_state):
    it = iter(refs)
    q_ref, k_ref, v_ref, g_ref, dec_ref, gn_ref = (next(it) for _ in range(6))
    rope = (next(it)[...], next(it)[...], next(it)[...]) if use_rope else None
    s0f_ref, s0b_ref = (next(it), next(it)) if has_state else (None, None)
    o_ref, sf_ref, sb_ref, oacc = next(it), next(it), next(it), next(it)
    for hh in range(heads):
        cols = slice(hh * RET_DK, (hh + 1) * RET_DK)
        _ret_head(q_ref.at[:, cols], k_ref.at[:, cols], v_ref.at[:, cols], g_ref.at[:, cols],
                  dec_ref.at[:, hh], gn_ref.at[hh], rope,
                  s0f_ref.at[hh] if has_state else None, s0b_ref.at[hh] if has_state else None,
                  o_ref.at[:, cols], sf_ref.at[hh], sb_ref.at[hh], oacc.at[hh], s)


def _ret_head(q_ref, k_ref, v_ref, g_ref, dec_ref, gn_ref, rope, s0f_ref, s0b_ref, o_ref, sf_ref, sb_ref, oacc, s):
    use_rope, has_state = rope is not None, s0f_ref is not None
    n_chunks = s // BLOCK
    lgf = -jnp.exp(dec_ref[0])
    lgb = -jnp.exp(dec_ref[1])
    lgf1, lgb1 = lgf[:, 0:1], lgb[:, 0:1]
    diff = (lax.broadcasted_iota(jnp.int32, (BLOCK, BLOCK), 0)
            - lax.broadcasted_iota(jnp.int32, (BLOCK, BLOCK), 1)).astype(F32)
    dsum = (jnp.where(diff >= 0, jnp.exp(diff * lgf), 0.0)
            + jnp.where(diff <= 0, jnp.exp(-diff * lgb), 0.0))
    ic = lax.broadcasted_iota(jnp.int32, (BLOCK, 1), 0).astype(F32)
    qdf, kdf, cdf = jnp.exp((ic + 1.0) * lgf1), jnp.exp((BLOCK - 1.0 - ic) * lgf1), jnp.exp(BLOCK * lgf1)
    qdb, kdb, cdb = jnp.exp((BLOCK - ic) * lgb1), jnp.exp(ic * lgb1), jnp.exp(BLOCK * lgb1)

    q = q_ref[...].astype(F32)
    k = k_ref[...].astype(F32) * (RET_DK ** -0.5)
    if use_rope:
        q = _rope(q, *rope, RET_DK // 4)
        k = _rope(k, *rope, RET_DK // 4)
    v = v_ref[...]

    def rows(a, n):
        return a[n * BLOCK:(n + 1) * BLOCK]

    def state_step(state, kn, kdec, vn, cdec):
        kd_t = jnp.transpose(kn * kdec).astype(BF16)
        return cdec * state + jnp.dot(kd_t, vn, preferred_element_type=F32)

    state = s0f_ref[...] if has_state else jnp.zeros((RET_DK, RET_DV), F32)
    for n in range(n_chunks):
        qn, kn, vn = rows(q, n), rows(k, n), rows(v, n)
        inner = lax.dot_general(qn.astype(BF16), kn.astype(BF16), (((1,), (1,)), ((), ())),
                                preferred_element_type=F32) * dsum
        o = (jnp.dot(inner.astype(BF16), vn, preferred_element_type=F32)
             + jnp.dot((qn * qdf).astype(BF16), state.astype(BF16), preferred_element_type=F32))
        state = state_step(state, kn, kdf, vn, cdf)
        oacc[n * BLOCK:(n + 1) * BLOCK, :] = o
    sf_ref[...] = state

    state = s0b_ref[...] if has_state else jnp.zeros((RET_DK, RET_DV), F32)
    for n in reversed(range(n_chunks)):
        qn, kn, vn = rows(q, n), rows(k, n), rows(v, n)
        o = rows(oacc, n) + jnp.dot((qn * qdb).astype(BF16), state.astype(BF16), preferred_element_type=F32)
        state = state_step(state, kn, kdb, vn, cdb)
        mu = jnp.mean(o, axis=-1, keepdims=True)
        var = jnp.mean(jnp.square(o - mu), axis=-1, keepdims=True)
        y = (o - mu) * lax.rsqrt(var + EPS) * gn_ref[...]
        gate = g_ref[n * BLOCK:(n + 1) * BLOCK, :].astype(F32)
        o_ref[n * BLOCK:(n + 1) * BLOCK, :] = (y * gate).astype(BF16)
    sb_ref[...] = state


def _retention(l, z, dec, ret_gn, *, latent, rope=None, s0f=None, s0b=None):
    if latent:
        nb, s, row0, heads = DEC_BATCH, DEC_SEQ, T_CTX, 1
    else:
        nb, s, row0, heads = BATCH, SEQ, 0, N_RET_HEADS
    width = heads * RET_DK
    hrow = lambda b, h: (row0 // s + b, h)
    in_specs = [pl.BlockSpec((s, width), hrow)] * 4
    in_specs += [pl.BlockSpec((None, 2, heads, 1, RET_DK), lambda b, h: (l, 0, h, 0, 0)),
                 pl.BlockSpec((None, heads, 1, RET_DV), lambda b, h: (l, h, 0, 0))]
    args = [z["qr"], z["kr"], z["vr"], z["gr"], dec, ret_gn.reshape(DEPTH, N_RET_HEADS, 1, RET_DV)]
    if latent:
        in_specs += [pl.BlockSpec((s, RET_DK), lambda b, h: (0, 0))] * 3
        in_specs += [pl.BlockSpec((None, None, heads, RET_DK, RET_DV), lambda b, h: (b, l, h, 0, 0))] * 2
        args += [*rope, s0f, s0b]
    st_spec = pl.BlockSpec((None, heads, RET_DK, RET_DV), lambda b, h: (b, h, 0, 0))
    st_shape = jax.ShapeDtypeStruct((nb, N_RET_HEADS, RET_DK, RET_DV), F32)
    return pl.pallas_call(
        functools.partial(_ret_kernel, s=s, heads=heads, use_rope=latent, has_state=latent),
        grid=(nb, N_RET_HEADS // heads),
        in_specs=in_specs,
        out_specs=[pl.BlockSpec((s, width), lambda b, h: (b, h)), st_spec, st_spec],
        out_shape=[jax.ShapeDtypeStruct((nb * s, RET_W), BF16), st_shape, st_shape],
        scratch_shapes=[pltpu.VMEM((heads, s, RET_DV), F32)],
        compiler_params=_params(("arbitrary", "arbitrary")),
        name="retention_latent" if latent else "retention_context",
    )(*args)


def _first_max(vals):
    best, idx = vals[0], jnp.zeros(vals[0].shape, jnp.int32)
    for j in range(1, len(vals)):
        upd = vals[j] > best
        best = jnp.where(upd, vals[j], best)
        idx = jnp.where(upd, j, idx)
    return best, idx


def _pick(idx, vals):
    out = vals[-1]
    for j in range(len(vals) - 2, -1, -1):
        out = jnp.where(idx == j, vals[j], out)
    return out


def _merge_kernel(x_ref, attc_ref, attl_ref, retc_ref, retl_ref, cm_ref, ga_ref, gb_ref, gc_ref, mod_ref, g2_ref,
                  wa_ref, wr_ref, wm_ref, wo_ref, wrt_ref, br_ref, tri_ref,
                  x1_ref, h2_ref, eid_ref, wts_ref, rank_ref, cnt_ref, cnt_scr):
    @pl.when(pl.program_id(0) == 0)
    def _zero_counts():
        cnt_scr[...] = jnp.zeros_like(cnt_scr)

    is_lat = pl.program_id(0) >= T_CTX // ROW_TILE
    att = jnp.where(is_lat, attl_ref[...], attc_ref[...])
    ret = jnp.where(is_lat, retl_ref[...], retc_ref[...])
    gate = lambda r: r[...].astype(F32)
    dot = lambda a, b: jnp.dot(a, b, preferred_element_type=F32)
    mix = (gate(ga_ref) * dot(att, wa_ref[...])
           + gate(gb_ref) * dot(ret, wr_ref[...])
           + gate(gc_ref) * dot(cm_ref[...], wm_ref[...]))
    x1 = x_ref[...] + mod_ref[2:3, :] * dot(mix.astype(BF16), wo_ref[...])
    x1_ref[...] = x1
    h2 = _rms(x1, g2_ref[...]) * (1.0 + mod_ref[4:5, :]) + mod_ref[3:4, :]
    _store_row_tiles(h2_ref, h2)

    logits = lax.dot_general(wrt_ref[...], h2, (((1,), (1,)), ((), ())), preferred_element_type=F32,
                             precision=lax.Precision.HIGHEST)
    score = _sigmoid(logits)
    sel = score + br_ref[...]
    sel_rows = [sel[e:e + 1, :] for e in range(N_EXPERTS)]
    score_rows = [score[e:e + 1, :] for e in range(N_EXPERTS)]
    group_scores = []
    for g in range(N_EXPERT_GROUPS):
        v = sel_rows[g * EXPERTS_PER_GROUP:(g + 1) * EXPERTS_PER_GROUP]
        pair_sums = [v[a] + v[b] for a in range(EXPERTS_PER_GROUP) for b in range(a + 1, EXPERTS_PER_GROUP)]
        group_scores.append(functools.reduce(jnp.maximum, pair_sums))
    _, gidx = _first_max(group_scores)
    in_sel = [_pick(gidx, [sel_rows[g * EXPERTS_PER_GROUP + j] for g in range(N_EXPERT_GROUPS)])
              for j in range(EXPERTS_PER_GROUP)]
    in_score = [_pick(gidx, [score_rows[g * EXPERTS_PER_GROUP + j] for g in range(N_EXPERT_GROUPS)])
                for j in range(EXPERTS_PER_GROUP)]
    _, loc0 = _first_max(in_sel)
    _, loc1 = _first_max([jnp.where(loc0 == j, -jnp.inf, in_sel[j]) for j in range(EXPERTS_PER_GROUP)])
    w0, w1 = _pick(loc0, in_score), _pick(loc1, in_score)
    den = w0 + w1
    e0 = gidx * EXPERTS_PER_GROUP + loc0
    e1 = gidx * EXPERTS_PER_GROUP + loc1
    eid_ref[0:1, :] = e0
    eid_ref[1:2, :] = e1
    wts_ref[0:1, :] = w0 / den
    wts_ref[1:2, :] = w1 / den

    eio = lax.broadcasted_iota(jnp.int32, score.shape, 0)
    oh0 = (eio == e0).astype(F32)
    oh1 = (eio == e1).astype(F32)
    both = oh0 + oh1
    before = jnp.dot(both.astype(BF16), tri_ref[...], preferred_element_type=F32) + cnt_scr[...]
    rank_ref[0:1, :] = jnp.sum(oh0 * before, axis=0, keepdims=True).astype(jnp.int32)
    rank_ref[1:2, :] = jnp.sum(oh1 * before, axis=0, keepdims=True).astype(jnp.int32)
    cnt_scr[...] = cnt_scr[...] + jnp.sum(both, axis=1, keepdims=True)
    cnt_ref[...] = cnt_scr[...]


def _merge(l, x, att_c, att_l, ret_c, ret_l, cm, z, mods, norm2, wa, wr, wm, wo, w_router_t, b_router):
    tm = ROW_TILE
    row = lambda i: (i, 0)
    n_ctx = T_CTX // tm
    ctx_row = lambda i: (jnp.minimum(i, n_ctx - 1), 0)
    lat_row = lambda i: (jnp.maximum(i - n_ctx, 0), 0)
    lw = lambda k, n: pl.BlockSpec((None, k, n), lambda i: (l, 0, 0))
    in_specs = [pl.BlockSpec((tm, D_MODEL), row),
                pl.BlockSpec((tm, ATT_Q), ctx_row), pl.BlockSpec((tm, ATT_Q), lat_row),
                pl.BlockSpec((tm, RET_W), ctx_row), pl.BlockSpec((tm, RET_W), lat_row),
                pl.BlockSpec((tm, MLP_W), row),
                pl.BlockSpec((tm, D_MODEL), row), pl.BlockSpec((tm, D_MODEL), row), pl.BlockSpec((tm, D_MODEL), row),
                pl.BlockSpec((None, None, N_MOD, D_MODEL), lambda i: (l, _mod_row(i), 0, 0)),
                pl.BlockSpec((None, 1, D_MODEL), lambda i: (l, 0, 0)),
                lw(ATT_Q, D_MODEL), lw(RET_W, D_MODEL), lw(MLP_W, D_MODEL), lw(D_MODEL, D_MODEL),
                pl.BlockSpec((N_EXPERTS, D_MODEL), lambda i: (0, 0)),
                pl.BlockSpec((N_EXPERTS, 1), lambda i: (0, 0)),
                pl.BlockSpec((tm, tm), lambda i: (0, 0))]
    lane = lambda i: (0, i)
    tri = jnp.asarray(np.triu(np.ones((tm, tm), np.float32), k=1), dtype=BF16)
    return pl.pallas_call(
        _merge_kernel,
        grid=(T_ALL // tm,),
        in_specs=in_specs,
        out_specs=[pl.BlockSpec((tm, D_MODEL), row), pl.BlockSpec((tm * TILE_ROWS, 128), row),
                   pl.BlockSpec((TOP_K, tm), lane), pl.BlockSpec((TOP_K, tm), lane),
                   pl.BlockSpec((TOP_K, tm), lane), pl.BlockSpec((N_EXPERTS, 1), lambda i: (0, 0))],
        out_shape=[jax.ShapeDtypeStruct((T_ALL, D_MODEL), F32), jax.ShapeDtypeStruct((T_ALL * TILE_ROWS, 128), F32),
                   jax.ShapeDtypeStruct((TOP_K, T_ALL), jnp.int32), jax.ShapeDtypeStruct((TOP_K, T_ALL), F32),
                   jax.ShapeDtypeStruct((TOP_K, T_ALL), jnp.int32), jax.ShapeDtypeStruct((N_EXPERTS, 1), F32)],
        scratch_shapes=[pltpu.VMEM((N_EXPERTS, 1), F32)],
        compiler_params=_params(("arbitrary",), V7X_VMEM_LIMIT_BYTES),
        name="merge_router",
    )(x, att_c, att_l, ret_c, ret_l, cm, z["ga"], z["gb"], z["gc"], mods, norm2.reshape(DEPTH, 1, D_MODEL),
      wa, wr, wm, wo, w_router_t, b_router.reshape(N_EXPERTS, 1), tri)


def _dispatch_plan(eid, rank, counts):
    expert = jnp.arange(N_EXPERTS, dtype=jnp.int32)
    padded = (counts + MOE_BLOCK - 1) // MOE_BLOCK * MOE_BLOCK
    pend = jnp.cumsum(padded)
    pstart = pend - padded
    dest = rank + jnp.sum(jnp.where(eid[..., None] == expert, pstart, 0), axis=-1)
    block_start = jnp.arange(N_MOE_BLOCKS, dtype=jnp.int32) * MOE_BLOCK
    block_e = jnp.minimum(jnp.sum((pend[None, :] <= block_start[:, None]).astype(jnp.int32), axis=1),
                          N_EXPERTS - 1)
    pad_ranges = jnp.stack([jnp.append(pstart + counts, pend[-1]),
                            jnp.append(pend, N_SLOTS)]).astype(jnp.int32)
    return block_e.astype(jnp.int32), pad_ranges, dest.reshape(N_ASSIGN)


ROWS_PER_TRIP = 8


ROW_COPY_PRIORITY = 1


def _moe_kernel(be_ref, pad_ref, dest_ref, h_ref, wg_ref, wu_ref, wd_ref, o_ref, xbuf, wg_s, wu_s, wd_s, src_s):
    i = pl.program_id(0)
    last = pl.num_programs(0) - 1

    def gather(blk, buf):
        for r in range(MOE_BLOCK):
            first = pl.multiple_of(src_s[blk * MOE_BLOCK + r], TILE_ROWS)
            xbuf[buf, r * TILE_ROWS:(r + 1) * TILE_ROWS, :] = h_ref[pl.ds(first, TILE_ROWS), :]

    @pl.when(i == 0)
    def _init():
        def clear(j, c):
            src_s[j] = 0
            return c
        for e in range(N_EXPERTS + 1):
            lax.fori_loop(pad_ref[0, e], pad_ref[1, e], clear, 0)

        def put(g, c):
            for u in range(ROWS_PER_TRIP):
                t = g * ROWS_PER_TRIP + u
                for k in range(TOP_K):
                    src_s[dest_ref[k * T_ALL + t]] = t * TILE_ROWS
            return c
        lax.fori_loop(0, T_ALL // ROWS_PER_TRIP, put, 0)
        gather(0, 0)

    prev = be_ref[jnp.maximum(i - 1, 0)]
    used = i * MOE_BLOCK < pad_ref[0, N_EXPERTS]

    @pl.when(used & ((i == 0) | (be_ref[i] != prev)))
    def _cast():
        wg_s[...] = wg_ref[...].astype(BF16)
        wu_s[...] = wu_ref[...].astype(BF16)
        wd_s[...] = wd_ref[...].astype(BF16)

    @pl.when(used)
    def _block():
        buf = lax.rem(i, 2)
        gather(jnp.minimum(i + 1, last), 1 - buf)
        x = _load_row_tiles(xbuf.at[buf]).astype(BF16)
        a = jnp.dot(x, wg_s[...], preferred_element_type=F32)
        b = jnp.dot(x, wu_s[...], preferred_element_type=F32)
        mid = ((a * _sigmoid(a)) * b).astype(BF16)
        _store_row_tiles(o_ref, jnp.dot(mid, wd_s[...], preferred_element_type=F32))

    @pl.when(jnp.logical_not(used))
    def _unused():
        o_ref[...] = jnp.zeros_like(o_ref)


def _moe(l, h2, plan, w_gate, w_up, w_down):
    block_e, pad_ranges, dest = plan
    once = pl.Buffered(1)
    wspec = lambda k, n: pl.BlockSpec((None, None, k, n), lambda i, be, pr, de: (l, be[i], 0, 0),
                                      pipeline_mode=once)
    grid_spec = pltpu.PrefetchScalarGridSpec(
        num_scalar_prefetch=3,
        grid=(N_MOE_BLOCKS,),
        in_specs=[pl.BlockSpec((T_ALL * TILE_ROWS, 128), lambda i, be, pr, de: (0, 0), pipeline_mode=once),
                  wspec(D_MODEL, EXPERT_FF), wspec(D_MODEL, EXPERT_FF), wspec(EXPERT_FF, D_MODEL)],
        out_specs=pl.BlockSpec((MOE_BLOCK * TILE_ROWS, 128), lambda i, be, pr, de: (i, 0)),
        scratch_shapes=[pltpu.VMEM((2, MOE_BLOCK * TILE_ROWS, 128), F32),
                        pltpu.VMEM((D_MODEL, EXPERT_FF), BF16), pltpu.VMEM((D_MODEL, EXPERT_FF), BF16),
                        pltpu.VMEM((EXPERT_FF, D_MODEL), BF16),
                        pltpu.SMEM((N_SLOTS,), jnp.int32)])
    return pl.pallas_call(
        _moe_kernel,
        grid_spec=grid_spec,
        out_shape=jax.ShapeDtypeStruct((N_SLOTS * TILE_ROWS, 128), F32),
        compiler_params=_params(("arbitrary",), V7X_MOE_VMEM_LIMIT_BYTES),
        name="moe_experts",
    )(block_e, pad_ranges, dest, h2, w_gate, w_up, w_down)


def _start_expert_rows(dest_ref, y_hbm, ybuf, sem, tile_idx, slot):
    base = tile_idx * ROW_TILE
    for k in range(TOP_K):
        for t in range(ROW_TILE):
            first = pl.multiple_of(dest_ref[k * T_ALL + base + t], TILE_ROWS)
            pltpu.make_async_copy(y_hbm.at[pl.ds(first, TILE_ROWS)],
                                  ybuf.at[slot, k, pl.ds(t * TILE_ROWS, TILE_ROWS)],
                                  sem.at[slot]).start(priority=ROW_COPY_PRIORITY)


def _wait_expert_rows(y_hbm, ybuf, sem, slot):
    for k in range(TOP_K):
        pltpu.make_async_copy(y_hbm.at[pl.ds(0, ROW_TILE * TILE_ROWS)], ybuf.at[slot, k], sem.at[slot]).wait()


def _gathered_expert_rows(dest_ref, y_hbm, ybuf, sem):
    i = pl.program_id(0)
    last = pl.num_programs(0) - 1
    slot = lax.rem(i, 2)

    @pl.when(i == 0)
    def _first():
        _start_expert_rows(dest_ref, y_hbm, ybuf, sem, 0, 0)

    _start_expert_rows(dest_ref, y_hbm, ybuf, sem, jnp.minimum(i + 1, last), 1 - slot)
    _wait_expert_rows(y_hbm, ybuf, sem, slot)
    return _load_row_tiles(ybuf.at[slot, 0]), _load_row_tiles(ybuf.at[slot, 1])


def _finish_expert_rows(y_hbm, ybuf, sem):
    i = pl.program_id(0)

    @pl.when(i == pl.num_programs(0) - 1)
    def _():
        _wait_expert_rows(y_hbm, ybuf, sem, 1 - lax.rem(i, 2))


def _combine_kernel(dest_ref, x_ref, y_hbm, wc_ref, mod_ref, oc_ref, ol_ref, ybuf, sem):
    wc = wc_ref[...]
    y0, y1 = _gathered_expert_rows(dest_ref, y_hbm, ybuf, sem)
    x = x_ref[...] + mod_ref[5:6, :] * (wc[:, 0:1] * y0 + wc[:, 1:2] * y1)
    _finish_expert_rows(y_hbm, ybuf, sem)
    is_lat = pl.program_id(0) >= T_CTX // ROW_TILE

    @pl.when(jnp.logical_not(is_lat))
    def _():
        oc_ref[...] = x

    @pl.when(is_lat)
    def _():
        ol_ref[...] = x


def _expert_rows_scratch():
    return [pltpu.VMEM((2, TOP_K, ROW_TILE * TILE_ROWS, 128), F32), pltpu.SemaphoreType.DMA((2,))]


def _final_combine(l, x, moe_out, dest_rows, wcol, mods):
    tm = ROW_TILE
    nt = T_ALL // tm
    n_ctx = T_CTX // tm
    row = lambda i, de: (i, 0)
    grid_spec = pltpu.PrefetchScalarGridSpec(
        num_scalar_prefetch=1,
        grid=(nt,),
        in_specs=[pl.BlockSpec((tm, D_MODEL), row), pl.BlockSpec(memory_space=pl.ANY),
                  pl.BlockSpec((tm, TOP_K), row),
                  pl.BlockSpec((None, None, N_MOD, D_MODEL), lambda i, de: (l, _mod_row(i), 0, 0))],
        out_specs=[pl.BlockSpec((tm, D_MODEL), lambda i, de: (jnp.minimum(i, n_ctx - 1), 0)),
                   pl.BlockSpec((tm, D_MODEL), lambda i, de: (jnp.maximum(i - n_ctx, 0), 0))],
        scratch_shapes=_expert_rows_scratch())
    return pl.pallas_call(
        _combine_kernel,
        grid_spec=grid_spec,
        out_shape=[jax.ShapeDtypeStruct((T_CTX, D_MODEL), F32), jax.ShapeDtypeStruct((T_LAT, D_MODEL), F32)],
        compiler_params=_params(("arbitrary",)),
        name="final_combine",
    )(dest_rows, x, moe_out, wcol, mods)


def kernel(x_prompt, x_sample, cache_k, cache_v, state_ret_fwd, state_ret_bwd, c, c_ctx, w_mod, b_mod, norm1, norm2, w_in, q_norm, k_norm, ret_decay_fwd, ret_decay_bwd, ret_gn, mlp_norm, w_spatial, b_spatial, w_att_o, w_ret_o, w_mlp_o, w_out, w_router, b_router, w_gate, w_up, w_down):
    cond = jnp.zeros((8, D_MODEL), F32).at[0].set(c_ctx).at[1:1 + DEC_BATCH].set(c)
    mods = _modulation(cond, w_mod, b_mod)

    w_in_bf = w_in.astype(BF16)
    wa, wr, wm, wo = (w.astype(BF16) for w in (w_att_o, w_ret_o, w_mlp_o, w_out))
    ws_bf = w_spatial.astype(BF16)
    w_router_t = w_router.T
    dec = jnp.broadcast_to(jnp.stack([ret_decay_fwd, ret_decay_bwd], axis=1)[..., None, None],
                           (DEPTH, 2, N_RET_HEADS, 1, RET_DK))
    rows = DEC_SEQ // GRID_W
    rope_q = _rope_tables(rows, HEAD_DIM, N_Q_HEADS)
    rope_k = _rope_tables(rows, HEAD_DIM, N_KV_HEADS)
    rope_r = _rope_tables(rows, RET_DK, 1)

    x = (x_prompt.reshape(T_CTX, D_MODEL), x_sample.reshape(T_LAT, D_MODEL))
    moe_out, dest_rows, wcol = None, None, None
    ks, vs, sfs, sbs = [], [], [], []
    for l in range(DEPTH):
        x, z = _input_projection(l, x, mods, norm1, w_in_bf, mlp_norm, ws_bf, b_spatial, moe_out, dest_rows, wcol)

        att_c, k_l, v_l = _attention(l, z, q_norm, k_norm, latent=False)
        att_l, = _attention(l, z, q_norm, k_norm, latent=True, rope=rope_q + rope_k,
                            cache_k=cache_k, cache_v=cache_v)
        ret_c, sf_l, sb_l = _retention(l, z, dec, ret_gn, latent=False)
        ret_l, _, _ = _retention(l, z, dec, ret_gn, latent=True, rope=rope_r,
                                 s0f=state_ret_fwd, s0b=state_ret_bwd)
        x, h2, eid, wts, rank, cnt = _merge(l, x, att_c, att_l, ret_c, ret_l, z["cm"], z, mods, norm2,
                                            wa, wr, wm, wo, w_router_t, b_router)
        plan = _dispatch_plan(eid, rank, cnt[:, 0].astype(jnp.int32))
        moe_out = _moe(l, h2, plan, w_gate, w_up, w_down)
        dest_rows = plan[2] * TILE_ROWS
        wcol = wts.T
        ks.append(k_l.reshape(BATCH, SEQ, ATT_KV))
        vs.append(v_l.reshape(BATCH, SEQ, ATT_KV))
        sfs.append(sf_l)
        sbs.append(sb_l)
    y_ctx, y_lat = _final_combine(DEPTH - 1, x, moe_out, dest_rows, wcol, mods)

    y_prompt = y_ctx.reshape(BATCH, SEQ, D_MODEL)
    y_sample = y_lat.reshape(DEC_BATCH, DEC_SEQ, D_MODEL)
    new_k = jnp.stack(ks, axis=1).reshape(BATCH, DEPTH, SEQ, N_KV_HEADS, HEAD_DIM)
    new_v = jnp.stack(vs, axis=1).reshape(BATCH, DEPTH, SEQ, N_KV_HEADS, HEAD_DIM)
    new_sf = jnp.stack(sfs, axis=1)
    new_sb = jnp.stack(sbs, axis=1)
    return (y_prompt, y_sample, new_k, new_v, new_sf, new_sb)
```

```python
import functools

import numpy as np
import jax
import jax.numpy as jnp
from jax import lax
from jax.experimental import pallas as pl
from jax.experimental.pallas import tpu as pltpu

F32 = jnp.float32
BF16 = jnp.bfloat16

D_MODEL = 1024
BATCH = 32
SEQ = 256
DEPTH = 4
DEC_BATCH = 2
DEC_SEQ = 1024
PAST_LEN = 256
GRID_W = 64
BLOCK = 128
ROPE_BASE = 10000.0
EPS = 1e-6
N_Q_HEADS = 8
N_KV_HEADS = 2
HEAD_DIM = 64
ATT_Q = N_Q_HEADS * HEAD_DIM
ATT_KV = N_KV_HEADS * HEAD_DIM
N_RET_HEADS = 4
RET_DK = 128
RET_DV = 128
RET_W = N_RET_HEADS * RET_DK
N_MLP_GROUPS = 4
MLP_W = 512
N_MOD = 6
N_EXPERTS = 16
N_EXPERT_GROUPS = 4
EXPERTS_PER_GROUP = N_EXPERTS // N_EXPERT_GROUPS
TOP_K = 2
EXPERT_FF = 512
MOE_BLOCK = 256

T_CTX = BATCH * SEQ
T_LAT = DEC_BATCH * DEC_SEQ
T_ALL = T_CTX + T_LAT
N_ASSIGN = T_ALL * TOP_K
N_MOE_BLOCKS = -(-N_ASSIGN // MOE_BLOCK) + N_EXPERTS
N_SLOTS = N_MOE_BLOCKS * MOE_BLOCK

ROW_TILE = 512
V7X_VMEM_LIMIT_BYTES = 56 * 1024 * 1024
V7X_MOE_VMEM_LIMIT_BYTES = 60 * 1024 * 1024

Z_SPLITS = (("qa", ATT_Q), ("kv", 2 * ATT_KV), ("qr", RET_W), ("kr", RET_W), ("vr", RET_W), ("gr", RET_W),
            ("mu", MLP_W), ("mv", MLP_W), ("ga", D_MODEL), ("gb", D_MODEL), ("gc", D_MODEL))
DOT_CHUNK = 512
Z_OUTPUTS = tuple((n, w) for n, w in Z_SPLITS if n not in ("mu", "mv")) + (("cm", MLP_W),)


def _params(sem, vmem=None):
    return pltpu.CompilerParams(dimension_semantics=sem, vmem_limit_bytes=vmem)


def _rms(x, g):
    return x * lax.rsqrt(jnp.mean(x * x, axis=-1, keepdims=True) + EPS) * g


def _sigmoid(x):
    return jax.nn.sigmoid(x)


TILE_ROWS = D_MODEL // 128


def _store_row_tiles(ref, val):
    rows = val.shape[0]
    for c in range(TILE_ROWS):
        ref[pl.ds(c, rows, stride=TILE_ROWS), :] = val[:, c * 128:(c + 1) * 128]


def _load_row_tiles(ref):
    rows = ref.shape[0] // TILE_ROWS
    return jnp.concatenate([ref[pl.ds(c, rows, stride=TILE_ROWS), :] for c in range(TILE_ROWS)], axis=1)


def _mod_row(i):
    first_lat = T_CTX // ROW_TILE
    per_batch = DEC_SEQ // ROW_TILE
    return jnp.where(i < first_lat, 0, 1 + (i - first_lat) // per_batch)


def _mod_kernel(cond_ref, w_ref, b_ref, o_ref):
    c = cond_ref[...]
    s = c * _sigmoid(c)
    o_ref[...] = jnp.dot(s, w_ref[...], preferred_element_type=F32,
                         precision=lax.Precision.HIGHEST) + b_ref[...]


def _modulation(cond, w_mod, b_mod):
    tn = D_MODEL
    out = pl.pallas_call(
        _mod_kernel,
        grid=(DEPTH, N_MOD),
        in_specs=[pl.BlockSpec((8, D_MODEL), lambda l, j: (0, 0)),
                  pl.BlockSpec((None, D_MODEL, tn), lambda l, j: (l, 0, j)),
                  pl.BlockSpec((None, 1, tn), lambda l, j: (l, 0, j))],
        out_specs=pl.BlockSpec((None, 8, tn), lambda l, j: (l, 0, j)),
        out_shape=jax.ShapeDtypeStruct((DEPTH, 8, N_MOD * D_MODEL), F32),
        compiler_params=_params(("arbitrary", "arbitrary")),
        name="modulation",
    )(cond, w_mod, b_mod.reshape(DEPTH, 1, N_MOD * D_MODEL))
    return out.reshape(DEPTH, 8, N_MOD, D_MODEL)


def _in_kernel(*refs, combine):
    it = iter(refs)
    if combine:
        dest_ref, x_ref, y_hbm, wc_ref, pmod_ref = (next(it) for _ in range(5))
    else:
        xc_ref, xl_ref = next(it), next(it)
    mod_ref, g_ref, w_ref, gn_ref, ws_ref, bs_ref = (next(it) for _ in range(6))
    xo_ref = next(it)
    outs = {name: next(it) for name, _ in Z_OUTPUTS}
    if combine:
        ybuf, sem = next(it), next(it)

    if combine:
        wc = wc_ref[...]
        y0, y1 = _gathered_expert_rows(dest_ref, y_hbm, ybuf, sem)
        x = x_ref[...] + pmod_ref[5:6, :] * (wc[:, 0:1] * y0 + wc[:, 1:2] * y1)
    else:
        x = jnp.where(pl.program_id(0) >= T_CTX // ROW_TILE, xl_ref[...], xc_ref[...])
    xo_ref[...] = x
    h = (_rms(x, g_ref[...]) * (1.0 + mod_ref[1:2, :]) + mod_ref[0:1, :]).astype(BF16)
    col = 0
    kept = {}
    for name, width in Z_SPLITS:
        for c in range(0, width, DOT_CHUNK):
            cw = min(DOT_CHUNK, width - c)
            r = jnp.dot(h, w_ref[:, col + c:col + c + cw], preferred_element_type=F32)
            if name in ("mu", "mv"):
                kept[name] = r
            elif name in ("ga", "gb", "gc"):
                outs[name][:, c:c + cw] = _sigmoid(r).astype(BF16)
            elif name == "gr":
                outs[name][:, c:c + cw] = (r * _sigmoid(r)).astype(BF16)
            else:
                outs[name][:, c:c + cw] = r.astype(BF16)
        col += width

    u = jax.nn.gelu(kept["mu"], approximate=True)
    vn = _rms(jax.nn.gelu(kept["mv"], approximate=True), gn_ref[...]).astype(BF16)
    gc = MLP_W // N_MLP_GROUPS
    for n in range(ROW_TILE // BLOCK):
        rows = slice(n * BLOCK, (n + 1) * BLOCK)
        for g in range(N_MLP_GROUPS):
            cols = slice(g * gc, (g + 1) * gc)
            f = jnp.dot(ws_ref[g], vn[rows, cols], preferred_element_type=F32) + bs_ref[g]
            outs["cm"][rows, cols] = (u[rows, cols] * f).astype(BF16)
    if combine:
        _finish_expert_rows(y_hbm, ybuf, sem)


def _input_projection(l, x, mods, norm1, w_in_bf, mlp_norm, w_spatial_bf, b_spatial,
                      moe_out=None, dest_rows=None, wcol=None):
    combine = moe_out is not None
    tm = ROW_TILE
    nt = T_ALL // tm
    n_ctx = T_CTX // tm
    row = lambda i, *_: (i, 0)
    mod_spec = lambda ll: pl.BlockSpec((None, None, N_MOD, D_MODEL), lambda i, *_: (ll, _mod_row(i), 0, 0))
    if combine:
        in_specs = [pl.BlockSpec((tm, D_MODEL), row),
                    pl.BlockSpec(memory_space=pl.ANY),
                    pl.BlockSpec((tm, TOP_K), row),
                    mod_spec(l - 1)]
        args = [dest_rows, x, moe_out, wcol, mods]
    else:
        in_specs = [pl.BlockSpec((tm, D_MODEL), lambda i: (jnp.minimum(i, n_ctx - 1), 0)),
                    pl.BlockSpec((tm, D_MODEL), lambda i: (jnp.maximum(i - n_ctx, 0), 0))]
        args = list(x)
    in_specs += [mod_spec(l),
                 pl.BlockSpec((None, 1, D_MODEL), lambda i, *_: (l, 0, 0)),
                 pl.BlockSpec((None, D_MODEL, w_in_bf.shape[-1]), lambda i, *_: (l, 0, 0),
                              pipeline_mode=pl.Buffered(1)),
                 pl.BlockSpec((None, 1, MLP_W), lambda i, *_: (l, 0, 0)),
                 pl.BlockSpec((None, N_MLP_GROUPS, BLOCK, BLOCK), lambda i, *_: (l, 0, 0, 0)),
                 pl.BlockSpec((None, N_MLP_GROUPS, BLOCK, 1), lambda i, *_: (l, 0, 0, 0))]
    args += [mods, norm1.reshape(DEPTH, 1, D_MODEL), w_in_bf, mlp_norm.reshape(DEPTH, 1, MLP_W), w_spatial_bf,
             b_spatial.reshape(DEPTH, N_MLP_GROUPS, BLOCK, 1)]
    out_specs = [pl.BlockSpec((tm, D_MODEL), row)]
    out_shape = [jax.ShapeDtypeStruct((T_ALL, D_MODEL), F32)]
    for _, width in Z_OUTPUTS:
        out_specs.append(pl.BlockSpec((tm, width), row))
        out_shape.append(jax.ShapeDtypeStruct((T_ALL, width), BF16))
    grid_spec = pltpu.PrefetchScalarGridSpec(
        num_scalar_prefetch=1 if combine else 0,
        grid=(nt,),
        in_specs=in_specs, out_specs=out_specs,
        scratch_shapes=_expert_rows_scratch() if combine else [])
    res = pl.pallas_call(
        functools.partial(_in_kernel, combine=combine),
        grid_spec=grid_spec, out_shape=out_shape,
        compiler_params=_params(("arbitrary",), V7X_VMEM_LIMIT_BYTES),
        name="input_projection",
    )(*args)
    return res[0], {name: r for (name, _), r in zip(Z_OUTPUTS, res[1:])}


def _rope_tables(rows, dim, reps):
    r = jnp.repeat(jnp.arange(rows, dtype=F32), GRID_W)
    col = jnp.tile(jnp.arange(GRID_W, dtype=F32), rows)
    half = dim // 2
    inv = ROPE_BASE ** (-jnp.arange(0, half, 2, dtype=F32) / half)
    ar = r[:, None] * inv
    ac = col[:, None] * inv
    ang = jnp.concatenate([ar, ar, ac, ac], axis=-1)
    cos, sin = jnp.cos(ang), jnp.sin(ang)
    first = (jnp.arange(dim) % (dim // 2)) < (dim // 4)
    sin_up = jnp.where(first, -sin, 0.0)
    sin_dn = jnp.where(first, 0.0, sin)
    t = lambda a: jnp.tile(a, (1, reps))
    return t(cos), t(sin_up), t(sin_dn)


def _rope(x, cos, sin_up, sin_dn, quarter):
    w = x.shape[-1]
    return x * cos + pltpu.roll(x, w - quarter, 1) * sin_up + pltpu.roll(x, quarter, 1) * sin_dn


def _attn_kernel(*refs, sk_new, use_rope, has_cache, emit_kv):
    it = iter(refs)
    q_ref, kv_ref, gq_ref, gk_ref, bdq_ref, bdk_ref = (next(it) for _ in range(6))
    if use_rope:
        cq_ref, suq_ref, sdq_ref, ck_ref, suk_ref, sdk_ref = (next(it) for _ in range(6))
    if has_cache:
        kc_ref, vc_ref = next(it), next(it)
    o_ref = next(it)
    if emit_kv:
        nk_ref, nv_ref = next(it), next(it)
    kd_scr, vd_scr = next(it), next(it)

    lo = lax.broadcasted_iota(jnp.int32, (1, 2 * HEAD_DIM), 1) < HEAD_DIM

    def dup_halves(a):
        r = pltpu.roll(a, HEAD_DIM, 1)
        return jnp.where(lo, a, r), jnp.where(lo, r, a)

    @pl.when(pl.program_id(1) == 0)
    def _prep():
        kv = kv_ref[...].astype(F32)
        k, v = kv[:, :ATT_KV], kv[:, ATT_KV:]
        ms = jnp.dot((k * k).astype(BF16), bdk_ref[...], preferred_element_type=F32)
        kn = k * lax.rsqrt(ms + EPS) * gk_ref[...]
        if emit_kv:
            nk_ref[...] = kn
            nv_ref[...] = v
        if use_rope:
            kn = _rope(kn, ck_ref[...], suk_ref[...], sdk_ref[...], HEAD_DIM // 4)
        k0, k1 = dup_halves(kn)
        v0, v1 = dup_halves(v)
        kd_scr[0, 0:sk_new, :] = k0.astype(BF16)
        kd_scr[1, 0:sk_new, :] = k1.astype(BF16)
        vd_scr[0, 0:sk_new, :] = v0.astype(BF16)
        vd_scr[1, 0:sk_new, :] = v1.astype(BF16)
        if has_cache:
            c0, c1 = dup_halves(kc_ref[...])
            d0, d1 = dup_halves(vc_ref[...])
            kd_scr[0, sk_new:, :] = c0.astype(BF16)
            kd_scr[1, sk_new:, :] = c1.astype(BF16)
            vd_scr[0, sk_new:, :] = d0.astype(BF16)
            vd_scr[1, sk_new:, :] = d1.astype(BF16)

    q = q_ref[...].astype(F32)
    ms = jnp.dot((q * q).astype(BF16), bdq_ref[...], preferred_element_type=F32)
    qn = q * lax.rsqrt(ms + EPS) * gq_ref[...]
    if use_rope:
        qn = _rope(qn, cq_ref[...], suq_ref[...], sdq_ref[...], HEAD_DIM // 4)
    qn = qn * (HEAD_DIM ** -0.5)
    heads_per_kv = N_Q_HEADS // N_KV_HEADS
    for j in range(N_Q_HEADS // 2):
        grp = (2 * j) // heads_per_kv
        kd, vd = kd_scr[grp], vd_scr[grp]
        qp = qn[:, 2 * HEAD_DIM * j:2 * HEAD_DIM * (j + 1)]
        halves = []
        for qm in (jnp.where(lo, qp, 0.0), jnp.where(lo, 0.0, qp)):
            s = lax.dot_general(qm.astype(BF16), kd, (((1,), (1,)), ((), ())), preferred_element_type=F32)
            e = jnp.exp(s - jnp.max(s, axis=-1, keepdims=True))
            den = jnp.sum(e, axis=-1, keepdims=True)
            halves.append(jnp.dot(e.astype(BF16), vd, preferred_element_type=F32) / den)
        o_ref[:, 2 * HEAD_DIM * j:2 * HEAD_DIM * (j + 1)] = jnp.where(lo, halves[0], halves[1]).astype(BF16)


def _block_diag_mean(width, group):
    idx = np.arange(width) // group
    return jnp.asarray((idx[:, None] == idx[None, :]).astype(np.float32) / group, dtype=BF16)


def _attention(l, z, q_norm, k_norm, *, latent, rope=None, cache_k=None, cache_v=None):
    if latent:
        nb, s, tq, row0 = DEC_BATCH, DEC_SEQ, 256, T_CTX
    else:
        nb, s, tq, row0 = BATCH, SEQ, SEQ, 0
    nq = s // tq
    sk = s + (PAST_LEN if latent else 0)
    qrow = lambda b, qi: (row0 // tq + b * nq + qi, 0)
    krow = lambda b, qi: (row0 // s + b, 0)
    const = lambda b, qi: (0, 0)
    lrow = lambda b, qi: (l, 0, 0)
    in_specs = [pl.BlockSpec((tq, ATT_Q), qrow),
                pl.BlockSpec((s, 2 * ATT_KV), krow),
                pl.BlockSpec((None, 1, ATT_Q), lrow),
                pl.BlockSpec((None, 1, ATT_KV), lrow),
                pl.BlockSpec((ATT_Q, ATT_Q), const),
                pl.BlockSpec((ATT_KV, ATT_KV), const)]
    args = [z["qa"], z["kv"],
            jnp.tile(q_norm, (1, N_Q_HEADS)).reshape(DEPTH, 1, ATT_Q),
            jnp.tile(k_norm, (1, N_KV_HEADS)).reshape(DEPTH, 1, ATT_KV),
            _block_diag_mean(ATT_Q, HEAD_DIM), _block_diag_mean(ATT_KV, HEAD_DIM)]
    if latent:
        cq, suq, sdq, ck, suk, sdk = rope
        in_specs += [pl.BlockSpec((tq, ATT_Q), lambda b, qi: (qi, 0))] * 3
        in_specs += [pl.BlockSpec((s, ATT_KV), const)] * 3
        in_specs += [pl.BlockSpec((None, None, PAST_LEN, ATT_KV), lambda b, qi: (b, l, 0, 0))] * 2
        args += [cq, suq, sdq, ck, suk, sdk,
                 cache_k.reshape(DEC_BATCH, DEPTH, PAST_LEN, ATT_KV),
                 cache_v.reshape(DEC_BATCH, DEPTH, PAST_LEN, ATT_KV)]
    rows = nb * s
    out_specs = [pl.BlockSpec((tq, ATT_Q), lambda b, qi: (b * nq + qi, 0))]
    out_shape = [jax.ShapeDtypeStruct((rows, ATT_Q), BF16)]
    if not latent:
        out_specs += [pl.BlockSpec((s, ATT_KV), lambda b, qi: (b, 0))] * 2
        out_shape += [jax.ShapeDtypeStruct((rows, ATT_KV), F32)] * 2
    return pl.pallas_call(
        functools.partial(_attn_kernel, sk_new=s, use_rope=latent, has_cache=latent, emit_kv=not latent),
        grid=(nb, nq),
        in_specs=in_specs, out_specs=out_specs, out_shape=out_shape,
        scratch_shapes=[pltpu.VMEM((N_KV_HEADS, sk, 2 * HEAD_DIM), BF16),
                        pltpu.VMEM((N_KV_HEADS, sk, 2 * HEAD_DIM), BF16)],
        compiler_params=_params(("arbitrary", "arbitrary"), V7X_VMEM_LIMIT_BYTES),
        name="attention_latent" if latent else "attention_context",
    )(*args)


def _ret_kernel(*refs, s, heads, use_rope, has_state):
    it = iter(refs)
    q_ref, k_ref, v_ref, g_ref, dec_ref, gn_ref = (next(it) for _ in range(6))
    rope = (next(it)[...], next(it)[...], next(it)[...]) if use_rope else None
    s0f_ref, s0b_ref = (next(it), next(it)) if has_state else (None, None)
    o_ref, sf_ref, sb_ref, oacc = next(it), next(it), next(it), next(it)
    for hh in range(heads):
        cols = slice(hh * RET_DK, (hh + 1) * RET_DK)
        _ret_head(q_ref.at[:, cols], k_ref.at[:, cols], v_ref.at[:, cols], g_ref.at[:, cols],
                  dec_ref.at[:, hh], gn_ref.at[hh], rope,
                  s0f_ref.at[hh] if has_state else None, s0b_ref.at[hh] if has_state else None,
                  o_ref.at[:, cols], sf_ref.at[hh], sb_ref.at[hh], oacc.at[hh], s)


def _ret_head(q_ref, k_ref, v_ref, g_ref, dec_ref, gn_ref, rope, s0f_ref, s0b_ref, o_ref, sf_ref, sb_ref, oacc, s):
    use_rope, has_state = rope is not None, s0f_ref is not None
    n_chunks = s // BLOCK
    lgf = -jnp.exp(dec_ref[0])
    lgb = -jnp.exp(dec_ref[1])
    lgf1, lgb1 = lgf[:, 0:1], lgb[:, 0:1]
    diff = (lax.broadcasted_iota(jnp.int32, (BLOCK, BLOCK), 0)
            - lax.broadcasted_iota(jnp.int32, (BLOCK, BLOCK), 1)).astype(F32)
    dsum = (jnp.where(diff >= 0, jnp.exp(diff * lgf), 0.0)
            + jnp.where(diff <= 0, jnp.exp(-diff * lgb), 0.0))
    ic = lax.broadcasted_iota(jnp.int32, (BLOCK, 1), 0).astype(F32)
    qdf, kdf, cdf = jnp.exp((ic + 1.0) * lgf1), jnp.exp((BLOCK - 1.0 - ic) * lgf1), jnp.exp(BLOCK * lgf1)
    qdb, kdb, cdb = jnp.exp((BLOCK - ic) * lgb1), jnp.exp(ic * lgb1), jnp.exp(BLOCK * lgb1)

    q = q_ref[...].astype(F32)
    k = k_ref[...].astype(F32) * (RET_DK ** -0.5)
    if use_rope:
        q = _rope(q, *rope, RET_DK // 4)
        k = _rope(k, *rope, RET_DK // 4)
    v = v_ref[...]

    def rows(a, n):
        return a[n * BLOCK:(n + 1) * BLOCK]

    def state_step(state, kn, kdec, vn, cdec):
        kd_t = jnp.transpose(kn * kdec).astype(BF16)
        return cdec * state + jnp.dot(kd_t, vn, preferred_element_type=F32)

    state = s0f_ref[...] if has_state else jnp.zeros((RET_DK, RET_DV), F32)
    for n in range(n_chunks):
        qn, kn, vn = rows(q, n), rows(k, n), rows(v, n)
        inner = lax.dot_general(qn.astype(BF16), kn.astype(BF16), (((1,), (1,)), ((), ())),
                                preferred_element_type=F32) * dsum
        o = (jnp.dot(inner.astype(BF16), vn, preferred_element_type=F32)
             + jnp.dot((qn * qdf).astype(BF16), state.astype(BF16), preferred_element_type=F32))
        state = state_step(state, kn, kdf, vn, cdf)
        oacc[n * BLOCK:(n + 1) * BLOCK, :] = o
    sf_ref[...] = state

    state = s0b_ref[...] if has_state else jnp.zeros((RET_DK, RET_DV), F32)
    for n in reversed(range(n_chunks)):
        qn, kn, vn = rows(q, n), rows(k, n), rows(v, n)
        o = rows(oacc, n) + jnp.dot((qn * qdb).astype(BF16), state.astype(BF16), preferred_element_type=F32)
        state = state_step(state, kn, kdb, vn, cdb)
        mu = jnp.mean(o, axis=-1, keepdims=True)
        var = jnp.mean(jnp.square(o - mu), axis=-1, keepdims=True)
        y = (o - mu) * lax.rsqrt(var + EPS) * gn_ref[...]
        gate = g_ref[n * BLOCK:(n + 1) * BLOCK, :].astype(F32)
        o_ref[n * BLOCK:(n + 1) * BLOCK, :] = (y * gate).astype(BF16)
    sb_ref[...] = state


def _retention(l, z, dec, ret_gn, *, latent, rope=None, s0f=None, s0b=None):
    if latent:
        nb, s, row0, heads = DEC_BATCH, DEC_SEQ, T_CTX, 1
    else:
        nb, s, row0, heads = BATCH, SEQ, 0, N_RET_HEADS
    width = heads * RET_DK
    hrow = lambda b, h: (row0 // s + b, h)
    in_specs = [pl.BlockSpec((s, width), hrow)] * 4
    in_specs += [pl.BlockSpec((None, 2, heads, 1, RET_DK), lambda b, h: (l, 0, h, 0, 0)),
                 pl.BlockSpec((None, heads, 1, RET_DV), lambda b, h: (l, h, 0, 0))]
    args = [z["qr"], z["kr"], z["vr"], z["gr"], dec, ret_gn.reshape(DEPTH, N_RET_HEADS, 1, RET_DV)]
    if latent:
        in_specs += [pl.BlockSpec((s, RET_DK), lambda b, h: (0, 0))] * 3
        in_specs += [pl.BlockSpec((None, None, heads, RET_DK, RET_DV), lambda b, h: (b, l, h, 0, 0))] * 2
        args += [*rope, s0f, s0b]
    st_spec = pl.BlockSpec((None, heads, RET_DK, RET_DV), lambda b, h: (b, h, 0, 0))
    st_shape = jax.ShapeDtypeStruct((nb, N_RET_HEADS, RET_DK, RET_DV), F32)
    return pl.pallas_call(
        functools.partial(_ret_kernel, s=s, heads=heads, use_rope=latent, has_state=latent),
        grid=(nb, N_RET_HEADS // heads),
        in_specs=in_specs,
        out_specs=[pl.BlockSpec((s, width), lambda b, h: (b, h)), st_spec, st_spec],
        out_shape=[jax.ShapeDtypeStruct((nb * s, RET_W), BF16), st_shape, st_shape],
        scratch_shapes=[pltpu.VMEM((heads, s, RET_DV), F32)],
        compiler_params=_params(("arbitrary", "arbitrary")),
        name="retention_latent" if latent else "retention_context",
    )(*args)


def _first_max(vals):
    best, idx = vals[0], jnp.zeros(vals[0].shape, jnp.int32)
    for j in range(1, len(vals)):
        upd = vals[j] > best
        best = jnp.where(upd, vals[j], best)
        idx = jnp.where(upd, j, idx)
    return best, idx


def _pick(idx, vals):
    out = vals[-1]
    for j in range(len(vals) - 2, -1, -1):
        out = jnp.where(idx == j, vals[j], out)
    return out


def _merge_kernel(x_ref, attc_ref, attl_ref, retc_ref, retl_ref, cm_ref, ga_ref, gb_ref, gc_ref, mod_ref, g2_ref,
                  wa_ref, wr_ref, wm_ref, wo_ref, wrt_ref, br_ref, tri_ref,
                  x1_ref, h2_ref, eid_ref, wts_ref, rank_ref, cnt_ref, cnt_scr):
    @pl.when(pl.program_id(0) == 0)
    def _zero_counts():
        cnt_scr[...] = jnp.zeros_like(cnt_scr)

    is_lat = pl.program_id(0) >= T_CTX // ROW_TILE
    att = jnp.where(is_lat, attl_ref[...], attc_ref[...])
    ret = jnp.where(is_lat, retl_ref[...], retc_ref[...])
    gate = lambda r: r[...].astype(F32)
    dot = lambda a, b: jnp.dot(a, b, preferred_element_type=F32)
    mix = (gate(ga_ref) * dot(att, wa_ref[...])
           + gate(gb_ref) * dot(ret, wr_ref[...])
           + gate(gc_ref) * dot(cm_ref[...], wm_ref[...]))
    x1 = x_ref[...] + mod_ref[2:3, :] * dot(mix.astype(BF16), wo_ref[...])
    x1_ref[...] = x1
    h2 = _rms(x1, g2_ref[...]) * (1.0 + mod_ref[4:5, :]) + mod_ref[3:4, :]
    _store_row_tiles(h2_ref, h2)

    logits = lax.dot_general(wrt_ref[...], h2, (((1,), (1,)), ((), ())), preferred_element_type=F32,
                             precision=lax.Precision.HIGHEST)
    score = _sigmoid(logits)
    sel = score + br_ref[...]
    sel_rows = [sel[e:e + 1, :] for e in range(N_EXPERTS)]
    score_rows = [score[e:e + 1, :] for e in range(N_EXPERTS)]
    group_scores = []
    for g in range(N_EXPERT_GROUPS):
        v = sel_rows[g * EXPERTS_PER_GROUP:(g + 1) * EXPERTS_PER_GROUP]
        pair_sums = [v[a] + v[b] for a in range(EXPERTS_PER_GROUP) for b in range(a + 1, EXPERTS_PER_GROUP)]
        group_scores.append(functools.reduce(jnp.maximum, pair_sums))
    _, gidx = _first_max(group_scores)
    in_sel = [_pick(gidx, [sel_rows[g * EXPERTS_PER_GROUP + j] for g in range(N_EXPERT_GROUPS)])
              for j in range(EXPERTS_PER_GROUP)]
    in_score = [_pick(gidx, [score_rows[g * EXPERTS_PER_GROUP + j] for g in range(N_EXPERT_GROUPS)])
                for j in range(EXPERTS_PER_GROUP)]
    _, loc0 = _first_max(in_sel)
    _, loc1 = _first_max([jnp.where(loc0 == j, -jnp.inf, in_sel[j]) for j in range(EXPERTS_PER_GROUP)])
    w0, w1 = _pick(loc0, in_score), _pick(loc1, in_score)
    den = w0 + w1
    e0 = gidx * EXPERTS_PER_GROUP + loc0
    e1 = gidx * EXPERTS_PER_GROUP + loc1
    eid_ref[0:1, :] = e0
    eid_ref[1:2, :] = e1
    wts_ref[0:1, :] = w0 / den
    wts_ref[1:2, :] = w1 / den

    eio = lax.broadcasted_iota(jnp.int32, score.shape, 0)
    oh0 = (eio == e0).astype(F32)
    oh1 = (eio == e1).astype(F32)
    both = oh0 + oh1
    before = jnp.dot(both.astype(BF16), tri_ref[...], preferred_element_type=F32) + cnt_scr[...]
    rank_ref[0:1, :] = jnp.sum(oh0 * before, axis=0, keepdims=True).astype(jnp.int32)
    rank_ref[1:2, :] = jnp.sum(oh1 * before, axis=0, keepdims=True).astype(jnp.int32)
    cnt_scr[...] = cnt_scr[...] + jnp.sum(both, axis=1, keepdims=True)
    cnt_ref[...] = cnt_scr[...]


def _merge(l, x, att_c, att_l, ret_c, ret_l, cm, z, mods, norm2, wa, wr, wm, wo, w_router_t, b_router):
    tm = ROW_TILE
    row = lambda i: (i, 0)
    n_ctx = T_CTX // tm
    ctx_row = lambda i: (jnp.minimum(i, n_ctx - 1), 0)
    lat_row = lambda i: (jnp.maximum(i - n_ctx, 0), 0)
    lw = lambda k, n: pl.BlockSpec((None, k, n), lambda i: (l, 0, 0))
    in_specs = [pl.BlockSpec((tm, D_MODEL), row),
                pl.BlockSpec((tm, ATT_Q), ctx_row), pl.BlockSpec((tm, ATT_Q), lat_row),
                pl.BlockSpec((tm, RET_W), ctx_row), pl.BlockSpec((tm, RET_W), lat_row),
                pl.BlockSpec((tm, MLP_W), row),
                pl.BlockSpec((tm, D_MODEL), row), pl.BlockSpec((tm, D_MODEL), row), pl.BlockSpec((tm, D_MODEL), row),
                pl.BlockSpec((None, None, N_MOD, D_MODEL), lambda i: (l, _mod_row(i), 0, 0)),
                pl.BlockSpec((None, 1, D_MODEL), lambda i: (l, 0, 0)),
                lw(ATT_Q, D_MODEL), lw(RET_W, D_MODEL), lw(MLP_W, D_MODEL), lw(D_MODEL, D_MODEL),
                pl.BlockSpec((N_EXPERTS, D_MODEL), lambda i: (0, 0)),
                pl.BlockSpec((N_EXPERTS, 1), lambda i: (0, 0)),
                pl.BlockSpec((tm, tm), lambda i: (0, 0))]
    lane = lambda i: (0, i)
    tri = jnp.asarray(np.triu(np.ones((tm, tm), np.float32), k=1), dtype=BF16)
    return pl.pallas_call(
        _merge_kernel,
        grid=(T_ALL // tm,),
        in_specs=in_specs,
        out_specs=[pl.BlockSpec((tm, D_MODEL), row), pl.BlockSpec((tm * TILE_ROWS, 128), row),
                   pl.BlockSpec((TOP_K, tm), lane), pl.BlockSpec((TOP_K, tm), lane),
                   pl.BlockSpec((TOP_K, tm), lane), pl.BlockSpec((N_EXPERTS, 1), lambda i: (0, 0))],
        out_shape=[jax.ShapeDtypeStruct((T_ALL, D_MODEL), F32), jax.ShapeDtypeStruct((T_ALL * TILE_ROWS, 128), F32),
                   jax.ShapeDtypeStruct((TOP_K, T_ALL), jnp.int32), jax.ShapeDtypeStruct((TOP_K, T_ALL), F32),
                   jax.ShapeDtypeStruct((TOP_K, T_ALL), jnp.int32), jax.ShapeDtypeStruct((N_EXPERTS, 1), F32)],
        scratch_shapes=[pltpu.VMEM((N_EXPERTS, 1), F32)],
        compiler_params=_params(("arbitrary",), V7X_VMEM_LIMIT_BYTES),
        name="merge_router",
    )(x, att_c, att_l, ret_c, ret_l, cm, z["ga"], z["gb"], z["gc"], mods, norm2.reshape(DEPTH, 1, D_MODEL),
      wa, wr, wm, wo, w_router_t, b_router.reshape(N_EXPERTS, 1), tri)


def _dispatch_plan(eid, rank, counts):
    expert = jnp.arange(N_EXPERTS, dtype=jnp.int32)
    padded = (counts + MOE_BLOCK - 1) // MOE_BLOCK * MOE_BLOCK
    pend = jnp.cumsum(padded)
    pstart = pend - padded
    dest = rank + jnp.sum(jnp.where(eid[..., None] == expert, pstart, 0), axis=-1)
    block_start = jnp.arange(N_MOE_BLOCKS, dtype=jnp.int32) * MOE_BLOCK
    block_e = jnp.minimum(jnp.sum((pend[None, :] <= block_start[:, None]).astype(jnp.int32), axis=1),
                          N_EXPERTS - 1)
    pad_ranges = jnp.stack([jnp.append(pstart + counts, pend[-1]),
                            jnp.append(pend, N_SLOTS)]).astype(jnp.int32)
    return block_e.astype(jnp.int32), pad_ranges, dest.reshape(N_ASSIGN)


ROWS_PER_TRIP = 8


N_DMA_PRIORITIES = 2


def _moe_kernel(be_ref, pad_ref, dest_ref, h_hbm, wg_ref, wu_ref, wd_ref, o_ref,
                h_ref, xbuf, wg_s, wu_s, wd_s, src_s, h_sem):
    i = pl.program_id(0)
    last = pl.num_programs(0) - 1
    h_copy = pltpu.make_async_copy(h_hbm, h_ref, h_sem)

    def gather(blk, buf):
        for r in range(MOE_BLOCK):
            first = pl.multiple_of(src_s[blk * MOE_BLOCK + r], TILE_ROWS)
            xbuf[buf, r * TILE_ROWS:(r + 1) * TILE_ROWS, :] = h_ref[pl.ds(first, TILE_ROWS), :]

    @pl.when(i == 0)
    def _init():
        h_copy.start()

        def clear(j, c):
            src_s[j] = 0
            return c
        for e in range(N_EXPERTS + 1):
            lax.fori_loop(pad_ref[0, e], pad_ref[1, e], clear, 0)

        def put(g, c):
            for u in range(ROWS_PER_TRIP):
                t = g * ROWS_PER_TRIP + u
                for k in range(TOP_K):
                    src_s[dest_ref[k * T_ALL + t]] = t * TILE_ROWS
            return c
        lax.fori_loop(0, T_ALL // ROWS_PER_TRIP, put, 0)
        h_copy.wait()
        gather(0, 0)

    prev = be_ref[jnp.maximum(i - 1, 0)]
    used = i * MOE_BLOCK < pad_ref[0, N_EXPERTS]

    @pl.when(used & ((i == 0) | (be_ref[i] != prev)))
    def _cast():
        wg_s[...] = wg_ref[...].astype(BF16)
        wu_s[...] = wu_ref[...].astype(BF16)
        wd_s[...] = wd_ref[...].astype(BF16)

    @pl.when(used)
    def _block():
        buf = lax.rem(i, 2)
        gather(jnp.minimum(i + 1, last), 1 - buf)
        x = _load_row_tiles(xbuf.at[buf]).astype(BF16)
        a = jnp.dot(x, wg_s[...], preferred_element_type=F32)
        b = jnp.dot(x, wu_s[...], preferred_element_type=F32)
        mid = ((a * _sigmoid(a)) * b).astype(BF16)
        _store_row_tiles(o_ref, jnp.dot(mid, wd_s[...], preferred_element_type=F32))

    @pl.when(jnp.logical_not(used))
    def _unused():
        o_ref[...] = jnp.zeros_like(o_ref)


def _moe(l, h2, plan, w_gate, w_up, w_down):
    block_e, pad_ranges, dest = plan
    once = pl.Buffered(1)
    wspec = lambda k, n: pl.BlockSpec((None, None, k, n), lambda i, be, pr, de: (l, be[i], 0, 0),
                                      pipeline_mode=once)
    grid_spec = pltpu.PrefetchScalarGridSpec(
        num_scalar_prefetch=3,
        grid=(N_MOE_BLOCKS,),
        in_specs=[pl.BlockSpec(memory_space=pl.ANY),
                  wspec(D_MODEL, EXPERT_FF), wspec(D_MODEL, EXPERT_FF), wspec(EXPERT_FF, D_MODEL)],
        out_specs=pl.BlockSpec((MOE_BLOCK * TILE_ROWS, 128), lambda i, be, pr, de: (i, 0)),
        scratch_shapes=[pltpu.VMEM((T_ALL * TILE_ROWS, 128), F32),
                        pltpu.VMEM((2, MOE_BLOCK * TILE_ROWS, 128), F32),
                        pltpu.VMEM((D_MODEL, EXPERT_FF), BF16), pltpu.VMEM((D_MODEL, EXPERT_FF), BF16),
                        pltpu.VMEM((EXPERT_FF, D_MODEL), BF16),
                        pltpu.SMEM((N_SLOTS,), jnp.int32),
                        pltpu.SemaphoreType.DMA(())])
    return pl.pallas_call(
        _moe_kernel,
        grid_spec=grid_spec,
        out_shape=jax.ShapeDtypeStruct((N_SLOTS * TILE_ROWS, 128), F32),
        compiler_params=_params(("arbitrary",), V7X_MOE_VMEM_LIMIT_BYTES),
        name="moe_experts",
    )(block_e, pad_ranges, dest, h2, w_gate, w_up, w_down)


def _start_expert_rows(dest_ref, y_hbm, ybuf, sem, tile_idx, slot):
    base = tile_idx * ROW_TILE
    for k in range(TOP_K):
        for t in range(ROW_TILE):
            first = pl.multiple_of(dest_ref[k * T_ALL + base + t], TILE_ROWS)
            pltpu.make_async_copy(y_hbm.at[pl.ds(first, TILE_ROWS)],
                                  ybuf.at[slot, k, pl.ds(t * TILE_ROWS, TILE_ROWS)],
                                  sem.at[slot]).start(priority=t % N_DMA_PRIORITIES)


def _wait_expert_rows(y_hbm, ybuf, sem, slot):
    for k in range(TOP_K):
        pltpu.make_async_copy(y_hbm.at[pl.ds(0, ROW_TILE * TILE_ROWS)], ybuf.at[slot, k], sem.at[slot]).wait()


def _gathered_expert_rows(dest_ref, y_hbm, ybuf, sem):
    i = pl.program_id(0)
    last = pl.num_programs(0) - 1
    slot = lax.rem(i, 2)

    @pl.when(i == 0)
    def _first():
        _start_expert_rows(dest_ref, y_hbm, ybuf, sem, 0, 0)

    _start_expert_rows(dest_ref, y_hbm, ybuf, sem, jnp.minimum(i + 1, last), 1 - slot)
    _wait_expert_rows(y_hbm, ybuf, sem, slot)
    return _load_row_tiles(ybuf.at[slot, 0]), _load_row_tiles(ybuf.at[slot, 1])


def _finish_expert_rows(y_hbm, ybuf, sem):
    i = pl.program_id(0)

    @pl.when(i == pl.num_programs(0) - 1)
    def _():
        _wait_expert_rows(y_hbm, ybuf, sem, 1 - lax.rem(i, 2))


def _combine_kernel(dest_ref, x_ref, y_hbm, wc_ref, mod_ref, oc_ref, ol_ref, ybuf, sem):
    wc = wc_ref[...]
    y0, y1 = _gathered_expert_rows(dest_ref, y_hbm, ybuf, sem)
    x = x_ref[...] + mod_ref[5:6, :] * (wc[:, 0:1] * y0 + wc[:, 1:2] * y1)
    _finish_expert_rows(y_hbm, ybuf, sem)
    is_lat = pl.program_id(0) >= T_CTX // ROW_TILE

    @pl.when(jnp.logical_not(is_lat))
    def _():
        oc_ref[...] = x

    @pl.when(is_lat)
    def _():
        ol_ref[...] = x


def _expert_rows_scratch():
    return [pltpu.VMEM((2, TOP_K, ROW_TILE * TILE_ROWS, 128), F32), pltpu.SemaphoreType.DMA((2,))]


def _final_combine(l, x, moe_out, dest_rows, wcol, mods):
    tm = ROW_TILE
    nt = T_ALL // tm
    n_ctx = T_CTX // tm
    row = lambda i, de: (i, 0)
    grid_spec = pltpu.PrefetchScalarGridSpec(
        num_scalar_prefetch=1,
        grid=(nt,),
        in_specs=[pl.BlockSpec((tm, D_MODEL), row), pl.BlockSpec(memory_space=pl.ANY),
                  pl.BlockSpec((tm, TOP_K), row),
                  pl.BlockSpec((None, None, N_MOD, D_MODEL), lambda i, de: (l, _mod_row(i), 0, 0))],
        out_specs=[pl.BlockSpec((tm, D_MODEL), lambda i, de: (jnp.minimum(i, n_ctx - 1), 0)),
                   pl.BlockSpec((tm, D_MODEL), lambda i, de: (jnp.maximum(i - n_ctx, 0), 0))],
        scratch_shapes=_expert_rows_scratch())
    return pl.pallas_call(
        _combine_kernel,
        grid_spec=grid_spec,
        out_shape=[jax.ShapeDtypeStruct((T_CTX, D_MODEL), F32), jax.ShapeDtypeStruct((T_LAT, D_MODEL), F32)],
        compiler_params=_params(("arbitrary",)),
        name="final_combine",
    )(dest_rows, x, moe_out, wcol, mods)


def kernel(x_prompt, x_sample, cache_k, cache_v, state_ret_fwd, state_ret_bwd, c, c_ctx, w_mod, b_mod, norm1, norm2, w_in, q_norm, k_norm, ret_decay_fwd, ret_decay_bwd, ret_gn, mlp_norm, w_spatial, b_spatial, w_att_o, w_ret_o, w_mlp_o, w_out, w_router, b_router, w_gate, w_up, w_down):
    cond = jnp.zeros((8, D_MODEL), F32).at[0].set(c_ctx).at[1:1 + DEC_BATCH].set(c)
    mods = _modulation(cond, w_mod, b_mod)

    w_in_bf = w_in.astype(BF16)
    wa, wr, wm, wo = (w.astype(BF16) for w in (w_att_o, w_ret_o, w_mlp_o, w_out))
    ws_bf = w_spatial.astype(BF16)
    w_router_t = w_router.T
    dec = jnp.broadcast_to(jnp.stack([ret_decay_fwd, ret_decay_bwd], axis=1)[..., None, None],
                           (DEPTH, 2, N_RET_HEADS, 1, RET_DK))
    rows = DEC_SEQ // GRID_W
    rope_q = _rope_tables(rows, HEAD_DIM, N_Q_HEADS)
    rope_k = _rope_tables(rows, HEAD_DIM, N_KV_HEADS)
    rope_r = _rope_tables(rows, RET_DK, 1)

    x = (x_prompt.reshape(T_CTX, D_MODEL), x_sample.reshape(T_LAT, D_MODEL))
    moe_out, dest_rows, wcol = None, None, None
    ks, vs, sfs, sbs = [], [], [], []
    for l in range(DEPTH):
        x, z = _input_projection(l, x, mods, norm1, w_in_bf, mlp_norm, ws_bf, b_spatial, moe_out, dest_rows, wcol)

        att_c, k_l, v_l = _attention(l, z, q_norm, k_norm, latent=False)
        att_l, = _attention(l, z, q_norm, k_norm, latent=True, rope=rope_q + rope_k,
                            cache_k=cache_k, cache_v=cache_v)
        ret_c, sf_l, sb_l = _retention(l, z, dec, ret_gn, latent=False)
        ret_l, _, _ = _retention(l, z, dec, ret_gn, latent=True, rope=rope_r,
                                 s0f=state_ret_fwd, s0b=state_ret_bwd)
        x, h2, eid, wts, rank, cnt = _merge(l, x, att_c, att_l, ret_c, ret_l, z["cm"], z, mods, norm2,
                                            wa, wr, wm, wo, w_router_t, b_router)
        plan = _dispatch_plan(eid, rank, cnt[:, 0].astype(jnp.int32))
        moe_out = _moe(l, h2, plan, w_gate, w_up, w_down)
        dest_rows = plan[2] * TILE_ROWS
        wcol = wts.T
        ks.append(k_l.reshape(BATCH, SEQ, ATT_KV))
        vs.append(v_l.reshape(BATCH, SEQ, ATT_KV))
        sfs.append(sf_l)
        sbs.append(sb_l)
    y_ctx, y_lat = _final_combine(DEPTH - 1, x, moe_out, dest_rows, wcol, mods)

    y_prompt = y_ctx.reshape(BATCH, SEQ, D_MODEL)
    y_sample = y_lat.reshape(DEC_BATCH, DEC_SEQ, D_MODEL)
    new_k = jnp.stack(ks, axis=1).reshape(BATCH, DEPTH, SEQ, N_KV_HEADS, HEAD_DIM)
    new_v = jnp.stack(vs, axis=1).reshape(BATCH, DEPTH, SEQ, N_KV_HEADS, HEAD_DIM)
    new_sf = jnp.stack(sfs, axis=1)
    new_sb = jnp.stack(sbs, axis=1)
    return (y_prompt, y_sample, new_k, new_v, new_sf, new_sb)
```

```python
import functools

import numpy as np
import jax
import jax.numpy as jnp
from jax import lax
from jax.experimental import pallas as pl
from jax.experimental.pallas import tpu as pltpu

F32 = jnp.float32
BF16 = jnp.bfloat16

D_MODEL = 1024
BATCH = 32
SEQ = 256
DEPTH = 4
DEC_BATCH = 2
DEC_SEQ = 1024
PAST_LEN = 256
GRID_W = 64
BLOCK = 128
ROPE_BASE = 10000.0
EPS = 1e-6
N_Q_HEADS = 8
N_KV_HEADS = 2
HEAD_DIM = 64
ATT_Q = N_Q_HEADS * HEAD_DIM
ATT_KV = N_KV_HEADS * HEAD_DIM
N_RET_HEADS = 4
RET_DK = 128
RET_DV = 128
RET_W = N_RET_HEADS * RET_DK
N_MLP_GROUPS = 4
MLP_W = 512
N_MOD = 6
N_EXPERTS = 16
N_EXPERT_GROUPS = 4
EXPERTS_PER_GROUP = N_EXPERTS // N_EXPERT_GROUPS
TOP_K = 2
EXPERT_FF = 512
MOE_BLOCK = 256

T_CTX = BATCH * SEQ
T_LAT = DEC_BATCH * DEC_SEQ
T_ALL = T_CTX + T_LAT
N_ASSIGN = T_ALL * TOP_K
N_MOE_BLOCKS = -(-N_ASSIGN // MOE_BLOCK) + N_EXPERTS
N_SLOTS = N_MOE_BLOCKS * MOE_BLOCK

ROW_TILE = 512
V7X_VMEM_LIMIT_BYTES = 56 * 1024 * 1024
V7X_MOE_VMEM_LIMIT_BYTES = 60 * 1024 * 1024

Z_SPLITS = (("qa", ATT_Q), ("kv", 2 * ATT_KV), ("qr", RET_W), ("kr", RET_W), ("vr", RET_W), ("gr", RET_W),
            ("mu", MLP_W), ("mv", MLP_W), ("ga", D_MODEL), ("gb", D_MODEL), ("gc", D_MODEL))
DOT_CHUNK = 512
Z_OUTPUTS = tuple((n, w) for n, w in Z_SPLITS if n not in ("mu", "mv")) + (("cm", MLP_W),)


def _params(sem, vmem=None):
    return pltpu.CompilerParams(dimension_semantics=sem, vmem_limit_bytes=vmem)


def _rms(x, g):
    return x * lax.rsqrt(jnp.mean(x * x, axis=-1, keepdims=True) + EPS) * g


def _sigmoid(x):
    return jax.nn.sigmoid(x)


TILE_ROWS = D_MODEL // 128


def _store_row_tiles(ref, val):
    rows = val.shape[0]
    for c in range(TILE_ROWS):
        ref[pl.ds(c, rows, stride=TILE_ROWS), :] = val[:, c * 128:(c + 1) * 128]


def _load_row_tiles(ref):
    rows = ref.shape[0] // TILE_ROWS
    return jnp.concatenate([ref[pl.ds(c, rows, stride=TILE_ROWS), :] for c in range(TILE_ROWS)], axis=1)


def _mod_row(i):
    first_lat = T_CTX // ROW_TILE
    per_batch = DEC_SEQ // ROW_TILE
    return jnp.where(i < first_lat, 0, 1 + (i - first_lat) // per_batch)


def _mod_kernel(cond_ref, w_ref, b_ref, o_ref):
    c = cond_ref[...]
    s = c * _sigmoid(c)
    o_ref[...] = jnp.dot(s.astype(BF16), w_ref[...].astype(BF16), preferred_element_type=F32) + b_ref[...]


def _modulation(cond, w_mod, b_mod):
    tn = D_MODEL
    out = pl.pallas_call(
        _mod_kernel,
        grid=(DEPTH, N_MOD),
        in_specs=[pl.BlockSpec((8, D_MODEL), lambda l, j: (0, 0)),
                  pl.BlockSpec((None, D_MODEL, tn), lambda l, j: (l, 0, j)),
                  pl.BlockSpec((None, 1, tn), lambda l, j: (l, 0, j))],
        out_specs=pl.BlockSpec((None, 8, tn), lambda l, j: (l, 0, j)),
        out_shape=jax.ShapeDtypeStruct((DEPTH, 8, N_MOD * D_MODEL), F32),
        compiler_params=_params(("arbitrary", "arbitrary")),
        name="modulation",
    )(cond, w_mod, b_mod.reshape(DEPTH, 1, N_MOD * D_MODEL))
    return out.reshape(DEPTH, 8, N_MOD, D_MODEL)


def _in_kernel(*refs, combine):
    it = iter(refs)
    if combine:
        dest_ref, x_ref, y_hbm, wc_ref, pmod_ref = (next(it) for _ in range(5))
    else:
        xc_ref, xl_ref = next(it), next(it)
    mod_ref, g_ref, w_ref, gn_ref, ws_ref, bs_ref = (next(it) for _ in range(6))
    xo_ref = next(it)
    outs = {name: next(it) for name, _ in Z_OUTPUTS}
    if combine:
        ybuf, sem = next(it), next(it)

    if combine:
        wc = wc_ref[...]
        y0, y1 = _gathered_expert_rows(dest_ref, y_hbm, ybuf, sem)
        x = x_ref[...] + pmod_ref[5:6, :] * (wc[:, 0:1] * y0 + wc[:, 1:2] * y1)
    else:
        x = jnp.where(pl.program_id(0) >= T_CTX // ROW_TILE, xl_ref[...], xc_ref[...])
    xo_ref[...] = x
    h = (_rms(x, g_ref[...]) * (1.0 + mod_ref[1:2, :]) + mod_ref[0:1, :]).astype(BF16)
    col = 0
    kept = {}
    for name, width in Z_SPLITS:
        for c in range(0, width, DOT_CHUNK):
            cw = min(DOT_CHUNK, width - c)
            r = jnp.dot(h, w_ref[:, col + c:col + c + cw], preferred_element_type=F32)
            if name in ("mu", "mv"):
                kept[name] = r
            elif name in ("ga", "gb", "gc"):
                outs[name][:, c:c + cw] = _sigmoid(r).astype(BF16)
            elif name == "gr":
                outs[name][:, c:c + cw] = (r * _sigmoid(r)).astype(BF16)
            else:
                outs[name][:, c:c + cw] = r.astype(BF16)
        col += width

    u = jax.nn.gelu(kept["mu"], approximate=True)
    vn = _rms(jax.nn.gelu(kept["mv"], approximate=True), gn_ref[...]).astype(BF16)
    gc = MLP_W // N_MLP_GROUPS
    for n in range(ROW_TILE // BLOCK):
        rows = slice(n * BLOCK, (n + 1) * BLOCK)
        for g in range(N_MLP_GROUPS):
            cols = slice(g * gc, (g + 1) * gc)
            f = jnp.dot(ws_ref[g], vn[rows, cols], preferred_element_type=F32) + bs_ref[g]
            outs["cm"][rows, cols] = (u[rows, cols] * f).astype(BF16)
    if combine:
        _finish_expert_rows(y_hbm, ybuf, sem)


def _input_projection(l, x, mods, norm1, w_in_bf, mlp_norm, w_spatial_bf, b_spatial,
                      moe_out=None, dest_rows=None, wcol=None):
    combine = moe_out is not None
    tm = ROW_TILE
    nt = T_ALL // tm
    n_ctx = T_CTX // tm
    row = lambda i, *_: (i, 0)
    mod_spec = lambda ll: pl.BlockSpec((None, None, N_MOD, D_MODEL), lambda i, *_: (ll, _mod_row(i), 0, 0))
    if combine:
        in_specs = [pl.BlockSpec((tm, D_MODEL), row),
                    pl.BlockSpec(memory_space=pl.ANY),
                    pl.BlockSpec((tm, TOP_K), row),
                    mod_spec(l - 1)]
        args = [dest_rows, x, moe_out, wcol, mods]
    else:
        in_specs = [pl.BlockSpec((tm, D_MODEL), lambda i: (jnp.minimum(i, n_ctx - 1), 0)),
                    pl.BlockSpec((tm, D_MODEL), lambda i: (jnp.maximum(i - n_ctx, 0), 0))]
        args = list(x)
    in_specs += [mod_spec(l),
                 pl.BlockSpec((None, 1, D_MODEL), lambda i, *_: (l, 0, 0)),
                 pl.BlockSpec((None, D_MODEL, w_in_bf.shape[-1]), lambda i, *_: (l, 0, 0),
                              pipeline_mode=pl.Buffered(1)),
                 pl.BlockSpec((None, 1, MLP_W), lambda i, *_: (l, 0, 0)),
                 pl.BlockSpec((None, N_MLP_GROUPS, BLOCK, BLOCK), lambda i, *_: (l, 0, 0, 0)),
                 pl.BlockSpec((None, N_MLP_GROUPS, BLOCK, 1), lambda i, *_: (l, 0, 0, 0))]
    args += [mods, norm1.reshape(DEPTH, 1, D_MODEL), w_in_bf, mlp_norm.reshape(DEPTH, 1, MLP_W), w_spatial_bf,
             b_spatial.reshape(DEPTH, N_MLP_GROUPS, BLOCK, 1)]
    out_specs = [pl.BlockSpec((tm, D_MODEL), row)]
    out_shape = [jax.ShapeDtypeStruct((T_ALL, D_MODEL), F32)]
    for _, width in Z_OUTPUTS:
        out_specs.append(pl.BlockSpec((tm, width), row))
        out_shape.append(jax.ShapeDtypeStruct((T_ALL, width), BF16))
    grid_spec = pltpu.PrefetchScalarGridSpec(
        num_scalar_prefetch=1 if combine else 0,
        grid=(nt,),
        in_specs=in_specs, out_specs=out_specs,
        scratch_shapes=_expert_rows_scratch() if combine else [])
    res = pl.pallas_call(
        functools.partial(_in_kernel, combine=combine),
        grid_spec=grid_spec, out_shape=out_shape,
        compiler_params=_params(("arbitrary",), V7X_VMEM_LIMIT_BYTES),
        name="input_projection",
    )(*args)
    return res[0], {name: r for (name, _), r in zip(Z_OUTPUTS, res[1:])}


def _rope_tables(rows, dim, reps):
    r = jnp.repeat(jnp.arange(rows, dtype=F32), GRID_W)
    col = jnp.tile(jnp.arange(GRID_W, dtype=F32), rows)
    half = dim // 2
    inv = ROPE_BASE ** (-jnp.arange(0, half, 2, dtype=F32) / half)
    ar = r[:, None] * inv
    ac = col[:, None] * inv
    ang = jnp.concatenate([ar, ar, ac, ac], axis=-1)
    cos, sin = jnp.cos(ang), jnp.sin(ang)
    first = (jnp.arange(dim) % (dim // 2)) < (dim // 4)
    sin_up = jnp.where(first, -sin, 0.0)
    sin_dn = jnp.where(first, 0.0, sin)
    t = lambda a: jnp.tile(a, (1, reps))
    return t(cos), t(sin_up), t(sin_dn)


def _rope(x, cos, sin_up, sin_dn, quarter):
    w = x.shape[-1]
    return x * cos + pltpu.roll(x, w - quarter, 1) * sin_up + pltpu.roll(x, quarter, 1) * sin_dn


def _attn_kernel(*refs, sk_new, use_rope, has_cache, emit_kv):
    it = iter(refs)
    q_ref, kv_ref, gq_ref, gk_ref, bdq_ref, bdk_ref = (next(it) for _ in range(6))
    if use_rope:
        cq_ref, suq_ref, sdq_ref, ck_ref, suk_ref, sdk_ref = (next(it) for _ in range(6))
    if has_cache:
        kc_ref, vc_ref = next(it), next(it)
    o_ref = next(it)
    if emit_kv:
        nk_ref, nv_ref = next(it), next(it)
    kd_scr, vd_scr = next(it), next(it)

    lo = lax.broadcasted_iota(jnp.int32, (1, 2 * HEAD_DIM), 1) < HEAD_DIM

    def dup_halves(a):
        r = pltpu.roll(a, HEAD_DIM, 1)
        return jnp.where(lo, a, r), jnp.where(lo, r, a)

    @pl.when(pl.program_id(1) == 0)
    def _prep():
        kv = kv_ref[...].astype(F32)
        k, v = kv[:, :ATT_KV], kv[:, ATT_KV:]
        ms = jnp.dot((k * k).astype(BF16), bdk_ref[...], preferred_element_type=F32)
        kn = k * lax.rsqrt(ms + EPS) * gk_ref[...]
        if emit_kv:
            nk_ref[...] = kn
            nv_ref[...] = v
        if use_rope:
            kn = _rope(kn, ck_ref[...], suk_ref[...], sdk_ref[...], HEAD_DIM // 4)
        k0, k1 = dup_halves(kn)
        v0, v1 = dup_halves(v)
        kd_scr[0, 0:sk_new, :] = k0.astype(BF16)
        kd_scr[1, 0:sk_new, :] = k1.astype(BF16)
        vd_scr[0, 0:sk_new, :] = v0.astype(BF16)
        vd_scr[1, 0:sk_new, :] = v1.astype(BF16)
        if has_cache:
            c0, c1 = dup_halves(kc_ref[...])
            d0, d1 = dup_halves(vc_ref[...])
            kd_scr[0, sk_new:, :] = c0.astype(BF16)
            kd_scr[1, sk_new:, :] = c1.astype(BF16)
            vd_scr[0, sk_new:, :] = d0.astype(BF16)
            vd_scr[1, sk_new:, :] = d1.astype(BF16)

    q = q_ref[...].astype(F32)
    ms = jnp.dot((q * q).astype(BF16), bdq_ref[...], preferred_element_type=F32)
    qn = q * lax.rsqrt(ms + EPS) * gq_ref[...]
    if use_rope:
        qn = _rope(qn, cq_ref[...], suq_ref[...], sdq_ref[...], HEAD_DIM // 4)
    qn = qn * (HEAD_DIM ** -0.5)
    heads_per_kv = N_Q_HEADS // N_KV_HEADS
    for j in range(N_Q_HEADS // 2):
        grp = (2 * j) // heads_per_kv
        kd, vd = kd_scr[grp], vd_scr[grp]
        qp = qn[:, 2 * HEAD_DIM * j:2 * HEAD_DIM * (j + 1)]
        halves = []
        for qm in (jnp.where(lo, qp, 0.0), jnp.where(lo, 0.0, qp)):
            s = lax.dot_general(qm.astype(BF16), kd, (((1,), (1,)), ((), ())), preferred_element_type=F32)
            e = jnp.exp(s - jnp.max(s, axis=-1, keepdims=True))
            den = jnp.sum(e, axis=-1, keepdims=True)
            halves.append(jnp.dot(e.astype(BF16), vd, preferred_element_type=F32) / den)
        o_ref[:, 2 * HEAD_DIM * j:2 * HEAD_DIM * (j + 1)] = jnp.where(lo, halves[0], halves[1]).astype(BF16)


def _block_diag_mean(width, group):
    idx = np.arange(width) // group
    return jnp.asarray((idx[:, None] == idx[None, :]).astype(np.float32) / group, dtype=BF16)


def _attention(l, z, q_norm, k_norm, *, latent, rope=None, cache_k=None, cache_v=None):
    if latent:
        nb, s, tq, row0 = DEC_BATCH, DEC_SEQ, 256, T_CTX
    else:
        nb, s, tq, row0 = BATCH, SEQ, SEQ, 0
    nq = s // tq
    sk = s + (PAST_LEN if latent else 0)
    qrow = lambda b, qi: (row0 // tq + b * nq + qi, 0)
    krow = lambda b, qi: (row0 // s + b, 0)
    const = lambda b, qi: (0, 0)
    lrow = lambda b, qi: (l, 0, 0)
    in_specs = [pl.BlockSpec((tq, ATT_Q), qrow),
                pl.BlockSpec((s, 2 * ATT_KV), krow),
                pl.BlockSpec((None, 1, ATT_Q), lrow),
                pl.BlockSpec((None, 1, ATT_KV), lrow),
                pl.BlockSpec((ATT_Q, ATT_Q), const),
                pl.BlockSpec((ATT_KV, ATT_KV), const)]
    args = [z["qa"], z["kv"],
            jnp.tile(q_norm, (1, N_Q_HEADS)).reshape(DEPTH, 1, ATT_Q),
            jnp.tile(k_norm, (1, N_KV_HEADS)).reshape(DEPTH, 1, ATT_KV),
            _block_diag_mean(ATT_Q, HEAD_DIM), _block_diag_mean(ATT_KV, HEAD_DIM)]
    if latent:
        cq, suq, sdq, ck, suk, sdk = rope
        in_specs += [pl.BlockSpec((tq, ATT_Q), lambda b, qi: (qi, 0))] * 3
        in_specs += [pl.BlockSpec((s, ATT_KV), const)] * 3
        in_specs += [pl.BlockSpec((None, None, PAST_LEN, ATT_KV), lambda b, qi: (b, l, 0, 0))] * 2
        args += [cq, suq, sdq, ck, suk, sdk,
                 cache_k.reshape(DEC_BATCH, DEPTH, PAST_LEN, ATT_KV),
                 cache_v.reshape(DEC_BATCH, DEPTH, PAST_LEN, ATT_KV)]
    rows = nb * s
    out_specs = [pl.BlockSpec((tq, ATT_Q), lambda b, qi: (b * nq + qi, 0))]
    out_shape = [jax.ShapeDtypeStruct((rows, ATT_Q), BF16)]
    if not latent:
        out_specs += [pl.BlockSpec((s, ATT_KV), lambda b, qi: (b, 0))] * 2
        out_shape += [jax.ShapeDtypeStruct((rows, ATT_KV), F32)] * 2
    return pl.pallas_call(
        functools.partial(_attn_kernel, sk_new=s, use_rope=latent, has_cache=latent, emit_kv=not latent),
        grid=(nb, nq),
        in_specs=in_specs, out_specs=out_specs, out_shape=out_shape,
        scratch_shapes=[pltpu.VMEM((N_KV_HEADS, sk, 2 * HEAD_DIM), BF16),
                        pltpu.VMEM((N_KV_HEADS, sk, 2 * HEAD_DIM), BF16)],
        compiler_params=_params(("arbitrary", "arbitrary"), V7X_VMEM_LIMIT_BYTES),
        name="attention_latent" if latent else "attention_context",
    )(*args)


def _ret_kernel(*refs, s, heads, use_rope, has_state):
    it = iter(refs)
    q_ref, k_ref, v_ref, g_ref, dec_ref, gn_ref = (next(it) for _ in range(6))
    rope = (next(it)[...], next(it)[...], next(it)[...]) if use_rope else None
    s0f_ref, s0b_ref = (next(it), next(it)) if has_state else (None, None)
    o_ref, sf_ref, sb_ref, oacc = next(it), next(it), next(it), next(it)
    for hh in range(heads):
        cols = slice(hh * RET_DK, (hh + 1) * RET_DK)
        _ret_head(q_ref.at[:, cols], k_ref.at[:, cols], v_ref.at[:, cols], g_ref.at[:, cols],
                  dec_ref.at[:, hh], gn_ref.at[hh], rope,
                  s0f_ref.at[hh] if has_state else None, s0b_ref.at[hh] if has_state else None,
                  o_ref.at[:, cols], sf_ref.at[hh], sb_ref.at[hh], oacc.at[hh], s)


def _ret_head(q_ref, k_ref, v_ref, g_ref, dec_ref, gn_ref, rope, s0f_ref, s0b_ref, o_ref, sf_ref, sb_ref, oacc, s):
    use_rope, has_state = rope is not None, s0f_ref is not None
    n_chunks = s // BLOCK
    lgf = -jnp.exp(dec_ref[0])
    lgb = -jnp.exp(dec_ref[1])
    lgf1, lgb1 = lgf[:, 0:1], lgb[:, 0:1]
    diff = (lax.broadcasted_iota(jnp.int32, (BLOCK, BLOCK), 0)
            - lax.broadcasted_iota(jnp.int32, (BLOCK, BLOCK), 1)).astype(F32)
    dsum = (jnp.where(diff >= 0, jnp.exp(diff * lgf), 0.0)
            + jnp.where(diff <= 0, jnp.exp(-diff * lgb), 0.0))
    ic = lax.broadcasted_iota(jnp.int32, (BLOCK, 1), 0).astype(F32)
    qdf, kdf, cdf = jnp.exp((ic + 1.0) * lgf1), jnp.exp((BLOCK - 1.0 - ic) * lgf1), jnp.exp(BLOCK * lgf1)
    qdb, kdb, cdb = jnp.exp((BLOCK - ic) * lgb1), jnp.exp(ic * lgb1), jnp.exp(BLOCK * lgb1)

    q = q_ref[...].astype(F32)
    k = k_ref[...].astype(F32) * (RET_DK ** -0.5)
    if use_rope:
        q = _rope(q, *rope, RET_DK // 4)
        k = _rope(k, *rope, RET_DK // 4)
    v = v_ref[...]

    def rows(a, n):
        return a[n * BLOCK:(n + 1) * BLOCK]

    def state_step(state, kn, kdec, vn, cdec):
        kd_t = jnp.transpose(kn * kdec).astype(BF16)
        return cdec * state + jnp.dot(kd_t, vn, preferred_element_type=F32)

    state = s0f_ref[...] if has_state else jnp.zeros((RET_DK, RET_DV), F32)
    for n in range(n_chunks):
        qn, kn, vn = rows(q, n), rows(k, n), rows(v, n)
        inner = lax.dot_general(qn.astype(BF16), kn.astype(BF16), (((1,), (1,)), ((), ())),
                                preferred_element_type=F32) * dsum
        o = (jnp.dot(inner.astype(BF16), vn, preferred_element_type=F32)
             + jnp.dot((qn * qdf).astype(BF16), state.astype(BF16), preferred_element_type=F32))
        state = state_step(state, kn, kdf, vn, cdf)
        oacc[n * BLOCK:(n + 1) * BLOCK, :] = o
    sf_ref[...] = state

    state = s0b_ref[...] if has_state else jnp.zeros((RET_DK, RET_DV), F32)
    for n in reversed(range(n_chunks)):
        qn, kn, vn = rows(q, n), rows(k, n), rows(v, n)
        o = rows(oacc, n) + jnp.dot((qn * qdb).astype(BF16), state.astype(BF16), preferred_element_type=F32)
        state = state_step(state, kn, kdb, vn, cdb)
        mu = jnp.mean(o, axis=-1, keepdims=True)
        var = jnp.mean(jnp.square(o - mu), axis=-1, keepdims=True)
        y = (o - mu) * lax.rsqrt(var + EPS) * gn_ref[...]
        gate = g_ref[n * BLOCK:(n + 1) * BLOCK, :].astype(F32)
        o_ref[n * BLOCK:(n + 1) * BLOCK, :] = (y * gate).astype(BF16)
    sb_ref[...] = state


def _retention(l, z, dec, ret_gn, *, latent, rope=None, s0f=None, s0b=None):
    if latent:
        nb, s, row0, heads = DEC_BATCH, DEC_SEQ, T_CTX, 1
    else:
        nb, s, row0, heads = BATCH, SEQ, 0, N_RET_HEADS
    width = heads * RET_DK
    hrow = lambda b, h: (row0 // s + b, h)
    in_specs = [pl.BlockSpec((s, width), hrow)] * 4
    in_specs += [pl.BlockSpec((None, 2, heads, 1, RET_DK), lambda b, h: (l, 0, h, 0, 0)),
                 pl.BlockSpec((None, heads, 1, RET_DV), lambda b, h: (l, h, 0, 0))]
    args = [z["qr"], z["kr"], z["vr"], z["gr"], dec, ret_gn.reshape(DEPTH, N_RET_HEADS, 1, RET_DV)]
    if latent:
        in_specs += [pl.BlockSpec((s, RET_DK), lambda b, h: (0, 0))] * 3
        in_specs += [pl.BlockSpec((None, None, heads, RET_DK, RET_DV), lambda b, h: (b, l, h, 0, 0))] * 2
        args += [*rope, s0f, s0b]
    st_spec = pl.BlockSpec((None, heads, RET_DK, RET_DV), lambda b, h: (b, h, 0, 0))
    st_shape = jax.ShapeDtypeStruct((nb, N_RET_HEADS, RET_DK, RET_DV), F32)
    return pl.pallas_call(
        functools.partial(_ret_kernel, s=s, heads=heads, use_rope=latent, has_state=latent),
        grid=(nb, N_RET_HEADS // heads),
        in_specs=in_specs,
        out_specs=[pl.BlockSpec((s, width), lambda b, h: (b, h)), st_spec, st_spec],
        out_shape=[jax.ShapeDtypeStruct((nb * s, RET_W), BF16), st_shape, st_shape],
        scratch_shapes=[pltpu.VMEM((heads, s, RET_DV), F32)],
        compiler_params=_params(("arbitrary", "arbitrary")),
        name="retention_latent" if latent else "retention_context",
    )(*args)


def _first_max(vals):
    best, idx = vals[0], jnp.zeros(vals[0].shape, jnp.int32)
    for j in range(1, len(vals)):
        upd = vals[j] > best
        best = jnp.where(upd, vals[j], best)
        idx = jnp.where(upd, j, idx)
    return best, idx


def _pick(idx, vals):
    out = vals[-1]
    for j in range(len(vals) - 2, -1, -1):
        out = jnp.where(idx == j, vals[j], out)
    return out


def _merge_kernel(x_ref, attc_ref, attl_ref, retc_ref, retl_ref, cm_ref, ga_ref, gb_ref, gc_ref, mod_ref, g2_ref,
                  wa_ref, wr_ref, wm_ref, wo_ref, wrt_ref, br_ref, tri_ref,
                  x1_ref, h2_ref, eid_ref, wts_ref, rank_ref, cnt_ref, cnt_scr):
    @pl.when(pl.program_id(0) == 0)
    def _zero_counts():
        cnt_scr[...] = jnp.zeros_like(cnt_scr)

    is_lat = pl.program_id(0) >= T_CTX // ROW_TILE
    att = jnp.where(is_lat, attl_ref[...], attc_ref[...])
    ret = jnp.where(is_lat, retl_ref[...], retc_ref[...])
    gate = lambda r: r[...].astype(F32)
    dot = lambda a, b: jnp.dot(a, b, preferred_element_type=F32)
    mix = (gate(ga_ref) * dot(att, wa_ref[...])
           + gate(gb_ref) * dot(ret, wr_ref[...])
           + gate(gc_ref) * dot(cm_ref[...], wm_ref[...]))
    x1 = x_ref[...] + mod_ref[2:3, :] * dot(mix.astype(BF16), wo_ref[...])
    x1_ref[...] = x1
    h2 = _rms(x1, g2_ref[...]) * (1.0 + mod_ref[4:5, :]) + mod_ref[3:4, :]
    _store_row_tiles(h2_ref, h2)

    logits = lax.dot_general(wrt_ref[...].astype(BF16), h2.astype(BF16), (((1,), (1,)), ((), ())),
                             preferred_element_type=F32)
    score = _sigmoid(logits)
    sel = score + br_ref[...]
    sel_rows = [sel[e:e + 1, :] for e in range(N_EXPERTS)]
    score_rows = [score[e:e + 1, :] for e in range(N_EXPERTS)]
    group_scores = []
    for g in range(N_EXPERT_GROUPS):
        v = sel_rows[g * EXPERTS_PER_GROUP:(g + 1) * EXPERTS_PER_GROUP]
        pair_sums = [v[a] + v[b] for a in range(EXPERTS_PER_GROUP) for b in range(a + 1, EXPERTS_PER_GROUP)]
        group_scores.append(functools.reduce(jnp.maximum, pair_sums))
    _, gidx = _first_max(group_scores)
    in_sel = [_pick(gidx, [sel_rows[g * EXPERTS_PER_GROUP + j] for g in range(N_EXPERT_GROUPS)])
              for j in range(EXPERTS_PER_GROUP)]
    in_score = [_pick(gidx, [score_rows[g * EXPERTS_PER_GROUP + j] for g in range(N_EXPERT_GROUPS)])
                for j in range(EXPERTS_PER_GROUP)]
    _, loc0 = _first_max(in_sel)
    _, loc1 = _first_max([jnp.where(loc0 == j, -jnp.inf, in_sel[j]) for j in range(EXPERTS_PER_GROUP)])
    w0, w1 = _pick(loc0, in_score), _pick(loc1, in_score)
    den = w0 + w1
    e0 = gidx * EXPERTS_PER_GROUP + loc0
    e1 = gidx * EXPERTS_PER_GROUP + loc1
    eid_ref[0:1, :] = e0
    eid_ref[1:2, :] = e1
    wts_ref[0:1, :] = w0 / den
    wts_ref[1:2, :] = w1 / den

    eio = lax.broadcasted_iota(jnp.int32, score.shape, 0)
    oh0 = (eio == e0).astype(F32)
    oh1 = (eio == e1).astype(F32)
    both = oh0 + oh1
    before = jnp.dot(both.astype(BF16), tri_ref[...], preferred_element_type=F32) + cnt_scr[...]
    rank_ref[0:1, :] = jnp.sum(oh0 * before, axis=0, keepdims=True).astype(jnp.int32)
    rank_ref[1:2, :] = jnp.sum(oh1 * before, axis=0, keepdims=True).astype(jnp.int32)
    cnt_scr[...] = cnt_scr[...] + jnp.sum(both, axis=1, keepdims=True)
    cnt_ref[...] = cnt_scr[...]


def _merge(l, x, att_c, att_l, ret_c, ret_l, cm, z, mods, norm2, wa, wr, wm, wo, w_router_t, b_router):
    tm = ROW_TILE
    row = lambda i: (i, 0)
    n_ctx = T_CTX // tm
    ctx_row = lambda i: (jnp.minimum(i, n_ctx - 1), 0)
    lat_row = lambda i: (jnp.maximum(i - n_ctx, 0), 0)
    lw = lambda k, n: pl.BlockSpec((None, k, n), lambda i: (l, 0, 0))
    in_specs = [pl.BlockSpec((tm, D_MODEL), row),
                pl.BlockSpec((tm, ATT_Q), ctx_row), pl.BlockSpec((tm, ATT_Q), lat_row),
                pl.BlockSpec((tm, RET_W), ctx_row), pl.BlockSpec((tm, RET_W), lat_row),
                pl.BlockSpec((tm, MLP_W), row),
                pl.BlockSpec((tm, D_MODEL), row), pl.BlockSpec((tm, D_MODEL), row), pl.BlockSpec((tm, D_MODEL), row),
                pl.BlockSpec((None, None, N_MOD, D_MODEL), lambda i: (l, _mod_row(i), 0, 0)),
                pl.BlockSpec((None, 1, D_MODEL), lambda i: (l, 0, 0)),
                lw(ATT_Q, D_MODEL), lw(RET_W, D_MODEL), lw(MLP_W, D_MODEL), lw(D_MODEL, D_MODEL),
                pl.BlockSpec((N_EXPERTS, D_MODEL), lambda i: (0, 0)),
                pl.BlockSpec((N_EXPERTS, 1), lambda i: (0, 0)),
                pl.BlockSpec((tm, tm), lambda i: (0, 0))]
    lane = lambda i: (0, i)
    tri = jnp.asarray(np.triu(np.ones((tm, tm), np.float32), k=1), dtype=BF16)
    return pl.pallas_call(
        _merge_kernel,
        grid=(T_ALL // tm,),
        in_specs=in_specs,
        out_specs=[pl.BlockSpec((tm, D_MODEL), row), pl.BlockSpec((tm * TILE_ROWS, 128), row),
                   pl.BlockSpec((TOP_K, tm), lane), pl.BlockSpec((TOP_K, tm), lane),
                   pl.BlockSpec((TOP_K, tm), lane), pl.BlockSpec((N_EXPERTS, 1), lambda i: (0, 0))],
        out_shape=[jax.ShapeDtypeStruct((T_ALL, D_MODEL), F32), jax.ShapeDtypeStruct((T_ALL * TILE_ROWS, 128), F32),
                   jax.ShapeDtypeStruct((TOP_K, T_ALL), jnp.int32), jax.ShapeDtypeStruct((TOP_K, T_ALL), F32),
                   jax.ShapeDtypeStruct((TOP_K, T_ALL), jnp.int32), jax.ShapeDtypeStruct((N_EXPERTS, 1), F32)],
        scratch_shapes=[pltpu.VMEM((N_EXPERTS, 1), F32)],
        compiler_params=_params(("arbitrary",), V7X_VMEM_LIMIT_BYTES),
        name="merge_router",
    )(x, att_c, att_l, ret_c, ret_l, cm, z["ga"], z["gb"], z["gc"], mods, norm2.reshape(DEPTH, 1, D_MODEL),
      wa, wr, wm, wo, w_router_t, b_router.reshape(N_EXPERTS, 1), tri)


def _dispatch_plan(eid, rank, counts):
    expert = jnp.arange(N_EXPERTS, dtype=jnp.int32)
    padded = (counts + MOE_BLOCK - 1) // MOE_BLOCK * MOE_BLOCK
    pend = jnp.cumsum(padded)
    pstart = pend - padded
    dest = rank + jnp.sum(jnp.where(eid[..., None] == expert, pstart, 0), axis=-1)
    block_start = jnp.arange(N_MOE_BLOCKS, dtype=jnp.int32) * MOE_BLOCK
    block_e = jnp.minimum(jnp.sum((pend[None, :] <= block_start[:, None]).astype(jnp.int32), axis=1),
                          N_EXPERTS - 1)
    pad_ranges = jnp.stack([jnp.append(pstart + counts, pend[-1]),
                            jnp.append(pend, N_SLOTS)]).astype(jnp.int32)
    return block_e.astype(jnp.int32), pad_ranges, dest.reshape(N_ASSIGN)


ROWS_PER_TRIP = 8


N_DMA_PRIORITIES = 2


def _moe_kernel(be_ref, pad_ref, dest_ref, h_ref, wg_ref, wu_ref, wd_ref, o_ref, xbuf, wg_s, wu_s, wd_s, src_s):
    i = pl.program_id(0)
    last = pl.num_programs(0) - 1

    def gather(blk, buf):
        for r in range(MOE_BLOCK):
            first = pl.multiple_of(src_s[blk * MOE_BLOCK + r], TILE_ROWS)
            xbuf[buf, r * TILE_ROWS:(r + 1) * TILE_ROWS, :] = h_ref[pl.ds(first, TILE_ROWS), :]

    @pl.when(i == 0)
    def _init():
        def clear(j, c):
            src_s[j] = 0
            return c
        for e in range(N_EXPERTS + 1):
            lax.fori_loop(pad_ref[0, e], pad_ref[1, e], clear, 0)

        def put(g, c):
            for u in range(ROWS_PER_TRIP):
                t = g * ROWS_PER_TRIP + u
                for k in range(TOP_K):
                    src_s[dest_ref[k * T_ALL + t]] = t * TILE_ROWS
            return c
        lax.fori_loop(0, T_ALL // ROWS_PER_TRIP, put, 0)
        gather(0, 0)

    prev = be_ref[jnp.maximum(i - 1, 0)]
    used = i * MOE_BLOCK < pad_ref[0, N_EXPERTS]

    @pl.when(used & ((i == 0) | (be_ref[i] != prev)))
    def _cast():
        wg_s[...] = wg_ref[...].astype(BF16)
        wu_s[...] = wu_ref[...].astype(BF16)
        wd_s[...] = wd_ref[...].astype(BF16)

    @pl.when(used)
    def _block():
        buf = lax.rem(i, 2)
        gather(jnp.minimum(i + 1, last), 1 - buf)
        x = _load_row_tiles(xbuf.at[buf]).astype(BF16)
        a = jnp.dot(x, wg_s[...], preferred_element_type=F32)
        b = jnp.dot(x, wu_s[...], preferred_element_type=F32)
        mid = ((a * _sigmoid(a)) * b).astype(BF16)
        _store_row_tiles(o_ref, jnp.dot(mid, wd_s[...], preferred_element_type=F32))

    @pl.when(jnp.logical_not(used))
    def _unused():
        o_ref[...] = jnp.zeros_like(o_ref)


def _moe(l, h2, plan, w_gate, w_up, w_down):
    block_e, pad_ranges, dest = plan
    once = pl.Buffered(1)
    wspec = lambda k, n: pl.BlockSpec((None, None, k, n), lambda i, be, pr, de: (l, be[i], 0, 0),
                                      pipeline_mode=once)
    grid_spec = pltpu.PrefetchScalarGridSpec(
        num_scalar_prefetch=3,
        grid=(N_MOE_BLOCKS,),
        in_specs=[pl.BlockSpec((T_ALL * TILE_ROWS, 128), lambda i, be, pr, de: (0, 0), pipeline_mode=once),
                  wspec(D_MODEL, EXPERT_FF), wspec(D_MODEL, EXPERT_FF), wspec(EXPERT_FF, D_MODEL)],
        out_specs=pl.BlockSpec((MOE_BLOCK * TILE_ROWS, 128), lambda i, be, pr, de: (i, 0)),
        scratch_shapes=[pltpu.VMEM((2, MOE_BLOCK * TILE_ROWS, 128), F32),
                        pltpu.VMEM((D_MODEL, EXPERT_FF), BF16), pltpu.VMEM((D_MODEL, EXPERT_FF), BF16),
                        pltpu.VMEM((EXPERT_FF, D_MODEL), BF16),
                        pltpu.SMEM((N_SLOTS,), jnp.int32)])
    return pl.pallas_call(
        _moe_kernel,
        grid_spec=grid_spec,
        out_shape=jax.ShapeDtypeStruct((N_SLOTS * TILE_ROWS, 128), F32),
        compiler_params=_params(("arbitrary",), V7X_MOE_VMEM_LIMIT_BYTES),
        name="moe_experts",
    )(block_e, pad_ranges, dest, h2, w_gate, w_up, w_down)


def _start_expert_rows(dest_ref, y_hbm, ybuf, sem, tile_idx, slot):
    base = tile_idx * ROW_TILE
    for k in range(TOP_K):
        for t in range(ROW_TILE):
            first = pl.multiple_of(dest_ref[k * T_ALL + base + t], TILE_ROWS)
            pltpu.make_async_copy(y_hbm.at[pl.ds(first, TILE_ROWS)],
                                  ybuf.at[slot, k, pl.ds(t * TILE_ROWS, TILE_ROWS)],
                                  sem.at[slot]).start(priority=t % N_DMA_PRIORITIES)


def _wait_expert_rows(y_hbm, ybuf, sem, slot):
    for k in range(TOP_K):
        pltpu.make_async_copy(y_hbm.at[pl.ds(0, ROW_TILE * TILE_ROWS)], ybuf.at[slot, k], sem.at[slot]).wait()


def _gathered_expert_rows(dest_ref, y_hbm, ybuf, sem):
    i = pl.program_id(0)
    last = pl.num_programs(0) - 1
    slot = lax.rem(i, 2)

    @pl.when(i == 0)
    def _first():
        _start_expert_rows(dest_ref, y_hbm, ybuf, sem, 0, 0)

    _start_expert_rows(dest_ref, y_hbm, ybuf, sem, jnp.minimum(i + 1, last), 1 - slot)
    _wait_expert_rows(y_hbm, ybuf, sem, slot)
    return _load_row_tiles(ybuf.at[slot, 0]), _load_row_tiles(ybuf.at[slot, 1])


def _finish_expert_rows(y_hbm, ybuf, sem):
    i = pl.program_id(0)

    @pl.when(i == pl.num_programs(0) - 1)
    def _():
        _wait_expert_rows(y_hbm, ybuf, sem, 1 - lax.rem(i, 2))


def _combine_kernel(dest_ref, x_ref, y_hbm, wc_ref, mod_ref, oc_ref, ol_ref, ybuf, sem):
    wc = wc_ref[...]
    y0, y1 = _gathered_expert_rows(dest_ref, y_hbm, ybuf, sem)
    x = x_ref[...] + mod_ref[5:6, :] * (wc[:, 0:1] * y0 + wc[:, 1:2] * y1)
    _finish_expert_rows(y_hbm, ybuf, sem)
    is_lat = pl.program_id(0) >= T_CTX // ROW_TILE

    @pl.when(jnp.logical_not(is_lat))
    def _():
        oc_ref[...] = x

    @pl.when(is_lat)
    def _():
        ol_ref[...] = x


def _expert_rows_scratch():
    return [pltpu.VMEM((2, TOP_K, ROW_TILE * TILE_ROWS, 128), F32), pltpu.SemaphoreType.DMA((2,))]


def _final_combine(l, x, moe_out, dest_rows, wcol, mods):
    tm = ROW_TILE
    nt = T_ALL // tm
    n_ctx = T_CTX // tm
    row = lambda i, de: (i, 0)
    grid_spec = pltpu.PrefetchScalarGridSpec(
        num_scalar_prefetch=1,
        grid=(nt,),
        in_specs=[pl.BlockSpec((tm, D_MODEL), row), pl.BlockSpec(memory_space=pl.ANY),
                  pl.BlockSpec((tm, TOP_K), row),
                  pl.BlockSpec((None, None, N_MOD, D_MODEL), lambda i, de: (l, _mod_row(i), 0, 0))],
        out_specs=[pl.BlockSpec((tm, D_MODEL), lambda i, de: (jnp.minimum(i, n_ctx - 1), 0)),
                   pl.BlockSpec((tm, D_MODEL), lambda i, de: (jnp.maximum(i - n_ctx, 0), 0))],
        scratch_shapes=_expert_rows_scratch())
    return pl.pallas_call(
        _combine_kernel,
        grid_spec=grid_spec,
        out_shape=[jax.ShapeDtypeStruct((T_CTX, D_MODEL), F32), jax.ShapeDtypeStruct((T_LAT, D_MODEL), F32)],
        compiler_params=_params(("arbitrary",)),
        name="final_combine",
    )(dest_rows, x, moe_out, wcol, mods)


def kernel(x_prompt, x_sample, cache_k, cache_v, state_ret_fwd, state_ret_bwd, c, c_ctx, w_mod, b_mod, norm1, norm2, w_in, q_norm, k_norm, ret_decay_fwd, ret_decay_bwd, ret_gn, mlp_norm, w_spatial, b_spatial, w_att_o, w_ret_o, w_mlp_o, w_out, w_router, b_router, w_gate, w_up, w_down):
    cond = jnp.zeros((8, D_MODEL), F32).at[0].set(c_ctx).at[1:1 + DEC_BATCH].set(c)
    mods = _modulation(cond, w_mod, b_mod)

    w_in_bf = w_in.astype(BF16)
    wa, wr, wm, wo = (w.astype(BF16) for w in (w_att_o, w_ret_o, w_mlp_o, w_out))
    ws_bf = w_spatial.astype(BF16)
    w_router_t = w_router.T
    dec = jnp.broadcast_to(jnp.stack([ret_decay_fwd, ret_decay_bwd], axis=1)[..., None, None],
                           (DEPTH, 2, N_RET_HEADS, 1, RET_DK))
    rows = DEC_SEQ // GRID_W
    rope_q = _rope_tables(rows, HEAD_DIM, N_Q_HEADS)
    rope_k = _rope_tables(rows, HEAD_DIM, N_KV_HEADS)
    rope_r = _rope_tables(rows, RET_DK, 1)

    x = (x_prompt.reshape(T_CTX, D_MODEL), x_sample.reshape(T_LAT, D_MODEL))
    moe_out, dest_rows, wcol = None, None, None
    ks, vs, sfs, sbs = [], [], [], []
    for l in range(DEPTH):
        x, z = _input_projection(l, x, mods, norm1, w_in_bf, mlp_norm, ws_bf, b_spatial, moe_out, dest_rows, wcol)

        att_c, k_l, v_l = _attention(l, z, q_norm, k_norm, latent=False)
        att_l, = _attention(l, z, q_norm, k_norm, latent=True, rope=rope_q + rope_k,
                            cache_k=cache_k, cache_v=cache_v)
        ret_c, sf_l, sb_l = _retention(l, z, dec, ret_gn, latent=False)
        ret_l, _, _ = _retention(l, z, dec, ret_gn, latent=True, rope=rope_r,
                                 s0f=state_ret_fwd, s0b=state_ret_bwd)
        x, h2, eid, wts, rank, cnt = _merge(l, x, att_c, att_l, ret_c, ret_l, z["cm"], z, mods, norm2,
                                            wa, wr, wm, wo, w_router_t, b_router)
        plan = _dispatch_plan(eid, rank, cnt[:, 0].astype(jnp.int32))
        moe_out = _moe(l, h2, plan, w_gate, w_up, w_down)
        dest_rows = plan[2] * TILE_ROWS
        wcol = wts.T
        ks.append(k_l.reshape(BATCH, SEQ, ATT_KV))
        vs.append(v_l.reshape(BATCH, SEQ, ATT_KV))
        sfs.append(sf_l)
        sbs.append(sb_l)
    y_ctx, y_lat = _final_combine(DEPTH - 1, x, moe_out, dest_rows, wcol, mods)

    y_prompt = y_ctx.reshape(BATCH, SEQ, D_MODEL)
    y_sample = y_lat.reshape(DEC_BATCH, DEC_SEQ, D_MODEL)
    new_k = jnp.stack(ks, axis=1).reshape(BATCH, DEPTH, SEQ, N_KV_HEADS, HEAD_DIM)
    new_v = jnp.stack(vs, axis=1).reshape(BATCH, DEPTH, SEQ, N_KV_HEADS, HEAD_DIM)
    new_sf = jnp.stack(sfs, axis=1)
    new_sb = jnp.stack(sbs, axis=1)
    return (y_prompt, y_sample, new_k, new_v, new_sf, new_sb)
```

```python
import functools

import numpy as np
import jax
import jax.numpy as jnp
from jax import lax
from jax.experimental import pallas as pl
from jax.experimental.pallas import tpu as pltpu

F32 = jnp.float32
BF16 = jnp.bfloat16

D_MODEL = 1024
BATCH = 32
SEQ = 256
DEPTH = 4
DEC_BATCH = 2
DEC_SEQ = 1024
PAST_LEN = 256
GRID_W = 64
BLOCK = 128
ROPE_BASE = 10000.0
EPS = 1e-6
N_Q_HEADS = 8
N_KV_HEADS = 2
HEAD_DIM = 64
ATT_Q = N_Q_HEADS * HEAD_DIM
ATT_KV = N_KV_HEADS * HEAD_DIM
N_RET_HEADS = 4
RET_DK = 128
RET_DV = 128
RET_W = N_RET_HEADS * RET_DK
N_MLP_GROUPS = 4
MLP_W = 512
N_MOD = 6
N_EXPERTS = 16
N_EXPERT_GROUPS = 4
EXPERTS_PER_GROUP = N_EXPERTS // N_EXPERT_GROUPS
TOP_K = 2
EXPERT_FF = 512
MOE_BLOCK = 256

T_CTX = BATCH * SEQ
T_LAT = DEC_BATCH * DEC_SEQ
T_ALL = T_CTX + T_LAT
N_ASSIGN = T_ALL * TOP_K
N_MOE_BLOCKS = -(-N_ASSIGN // MOE_BLOCK) + N_EXPERTS
N_SLOTS = N_MOE_BLOCKS * MOE_BLOCK

ROW_TILE = 512
V7X_VMEM_LIMIT_BYTES = 56 * 1024 * 1024
V7X_MOE_VMEM_LIMIT_BYTES = 62 * 1024 * 1024

Z_SPLITS = (("qa", ATT_Q), ("kv", 2 * ATT_KV), ("qr", RET_W), ("kr", RET_W), ("vr", RET_W), ("gr", RET_W),
            ("mu", MLP_W), ("mv", MLP_W), ("ga", D_MODEL), ("gb", D_MODEL), ("gc", D_MODEL))
DOT_CHUNK = 512
Z_OUTPUTS = tuple((n, w) for n, w in Z_SPLITS if n not in ("mu", "mv")) + (("cm", MLP_W),)


def _params(sem, vmem=None):
    return pltpu.CompilerParams(dimension_semantics=sem, vmem_limit_bytes=vmem)


def _rms(x, g):
    return x * lax.rsqrt(jnp.mean(x * x, axis=-1, keepdims=True) + EPS) * g


def _sigmoid(x):
    return jax.nn.sigmoid(x)


TILE_ROWS = D_MODEL // 128


def _store_row_tiles(ref, val):
    rows = val.shape[0]
    for c in range(TILE_ROWS):
        ref[pl.ds(c, rows, stride=TILE_ROWS), :] = val[:, c * 128:(c + 1) * 128]


def _load_row_tiles(ref):
    rows = ref.shape[0] // TILE_ROWS
    return jnp.concatenate([ref[pl.ds(c, rows, stride=TILE_ROWS), :] for c in range(TILE_ROWS)], axis=1)


def _mod_row(i):
    first_lat = T_CTX // ROW_TILE
    per_batch = DEC_SEQ // ROW_TILE
    return jnp.where(i < first_lat, 0, 1 + (i - first_lat) // per_batch)


def _mod_kernel(cond_ref, w_ref, b_ref, o_ref):
    c = cond_ref[...]
    s = c * _sigmoid(c)
    o_ref[...] = jnp.dot(s.astype(BF16), w_ref[...].astype(BF16), preferred_element_type=F32) + b_ref[...]


def _modulation(cond, w_mod, b_mod):
    tn = D_MODEL
    out = pl.pallas_call(
        _mod_kernel,
        grid=(DEPTH, N_MOD),
        in_specs=[pl.BlockSpec((8, D_MODEL), lambda l, j: (0, 0)),
                  pl.BlockSpec((None, D_MODEL, tn), lambda l, j: (l, 0, j)),
                  pl.BlockSpec((None, 1, tn), lambda l, j: (l, 0, j))],
        out_specs=pl.BlockSpec((None, 8, tn), lambda l, j: (l, 0, j)),
        out_shape=jax.ShapeDtypeStruct((DEPTH, 8, N_MOD * D_MODEL), F32),
        compiler_params=_params(("arbitrary", "arbitrary")),
        name="modulation",
    )(cond, w_mod, b_mod.reshape(DEPTH, 1, N_MOD * D_MODEL))
    return out.reshape(DEPTH, 8, N_MOD, D_MODEL)


def _in_kernel(*refs, combine):
    it = iter(refs)
    if combine:
        dest_ref, x_ref, y_hbm, wc_ref, pmod_ref = (next(it) for _ in range(5))
    else:
        xc_ref, xl_ref = next(it), next(it)
    mod_ref, g_ref, w_ref, gn_ref, ws_ref, bs_ref = (next(it) for _ in range(6))
    xo_ref = next(it)
    outs = {name: next(it) for name, _ in Z_OUTPUTS}
    if combine:
        ybuf, sem = next(it), next(it)

    if combine:
        wc = wc_ref[...]
        y0, y1 = _gathered_expert_rows(dest_ref, y_hbm, ybuf, sem)
        x = x_ref[...] + pmod_ref[5:6, :] * (wc[:, 0:1] * y0 + wc[:, 1:2] * y1)
    else:
        x = jnp.where(pl.program_id(0) >= T_CTX // ROW_TILE, xl_ref[...], xc_ref[...])
    xo_ref[...] = x
    h = (_rms(x, g_ref[...]) * (1.0 + mod_ref[1:2, :]) + mod_ref[0:1, :]).astype(BF16)
    col = 0
    kept = {}
    for name, width in Z_SPLITS:
        for c in range(0, width, DOT_CHUNK):
            cw = min(DOT_CHUNK, width - c)
            r = jnp.dot(h, w_ref[:, col + c:col + c + cw], preferred_element_type=F32)
            if name in ("mu", "mv"):
                kept[name] = r
            elif name in ("ga", "gb", "gc"):
                outs[name][:, c:c + cw] = _sigmoid(r).astype(BF16)
            elif name == "gr":
                outs[name][:, c:c + cw] = (r * _sigmoid(r)).astype(BF16)
            else:
                outs[name][:, c:c + cw] = r.astype(BF16)
        col += width

    u = jax.nn.gelu(kept["mu"], approximate=True)
    vn = _rms(jax.nn.gelu(kept["mv"], approximate=True), gn_ref[...]).astype(BF16)
    gc = MLP_W // N_MLP_GROUPS
    for n in range(ROW_TILE // BLOCK):
        rows = slice(n * BLOCK, (n + 1) * BLOCK)
        for g in range(N_MLP_GROUPS):
            cols = slice(g * gc, (g + 1) * gc)
            f = jnp.dot(ws_ref[g], vn[rows, cols], preferred_element_type=F32) + bs_ref[g]
            outs["cm"][rows, cols] = (u[rows, cols] * f).astype(BF16)
    if combine:
        _finish_expert_rows(y_hbm, ybuf, sem)


def _input_projection(l, x, mods, norm1, w_in_bf, mlp_norm, w_spatial_bf, b_spatial,
                      moe_out=None, dest_rows=None, wcol=None):
    combine = moe_out is not None
    tm = ROW_TILE
    nt = T_ALL // tm
    n_ctx = T_CTX // tm
    row = lambda i, *_: (i, 0)
    mod_spec = lambda ll: pl.BlockSpec((None, None, N_MOD, D_MODEL), lambda i, *_: (ll, _mod_row(i), 0, 0))
    if combine:
        in_specs = [pl.BlockSpec((tm, D_MODEL), row),
                    pl.BlockSpec(memory_space=pl.ANY),
                    pl.BlockSpec((tm, TOP_K), row),
                    mod_spec(l - 1)]
        args = [dest_rows, x, moe_out, wcol, mods]
    else:
        in_specs = [pl.BlockSpec((tm, D_MODEL), lambda i: (jnp.minimum(i, n_ctx - 1), 0)),
                    pl.BlockSpec((tm, D_MODEL), lambda i: (jnp.maximum(i - n_ctx, 0), 0))]
        args = list(x)
    in_specs += [mod_spec(l),
                 pl.BlockSpec((None, 1, D_MODEL), lambda i, *_: (l, 0, 0)),
                 pl.BlockSpec((None, D_MODEL, w_in_bf.shape[-1]), lambda i, *_: (l, 0, 0),
                              pipeline_mode=pl.Buffered(1)),
                 pl.BlockSpec((None, 1, MLP_W), lambda i, *_: (l, 0, 0)),
                 pl.BlockSpec((None, N_MLP_GROUPS, BLOCK, BLOCK), lambda i, *_: (l, 0, 0, 0)),
                 pl.BlockSpec((None, N_MLP_GROUPS, BLOCK, 1), lambda i, *_: (l, 0, 0, 0))]
    args += [mods, norm1.reshape(DEPTH, 1, D_MODEL), w_in_bf, mlp_norm.reshape(DEPTH, 1, MLP_W), w_spatial_bf,
             b_spatial.reshape(DEPTH, N_MLP_GROUPS, BLOCK, 1)]
    out_specs = [pl.BlockSpec((tm, D_MODEL), row)]
    out_shape = [jax.ShapeDtypeStruct((T_ALL, D_MODEL), F32)]
    for _, width in Z_OUTPUTS:
        out_specs.append(pl.BlockSpec((tm, width), row))
        out_shape.append(jax.ShapeDtypeStruct((T_ALL, width), BF16))
    grid_spec = pltpu.PrefetchScalarGridSpec(
        num_scalar_prefetch=1 if combine else 0,
        grid=(nt,),
        in_specs=in_specs, out_specs=out_specs,
        scratch_shapes=_expert_rows_scratch() if combine else [])
    res = pl.pallas_call(
        functools.partial(_in_kernel, combine=combine),
        grid_spec=grid_spec, out_shape=out_shape,
        compiler_params=_params(("arbitrary",), V7X_VMEM_LIMIT_BYTES),
        name="input_projection",
    )(*args)
    return res[0], {name: r for (name, _), r in zip(Z_OUTPUTS, res[1:])}


def _rope_tables(rows, dim, reps):
    r = jnp.repeat(jnp.arange(rows, dtype=F32), GRID_W)
    col = jnp.tile(jnp.arange(GRID_W, dtype=F32), rows)
    half = dim // 2
    inv = ROPE_BASE ** (-jnp.arange(0, half, 2, dtype=F32) / half)
    ar = r[:, None] * inv
    ac = col[:, None] * inv
    ang = jnp.concatenate([ar, ar, ac, ac], axis=-1)
    cos, sin = jnp.cos(ang), jnp.sin(ang)
    first = (jnp.arange(dim) % (dim // 2)) < (dim // 4)
    sin_up = jnp.where(first, -sin, 0.0)
    sin_dn = jnp.where(first, 0.0, sin)
    t = lambda a: jnp.tile(a, (1, reps))
    return t(cos), t(sin_up), t(sin_dn)


def _rope(x, cos, sin_up, sin_dn, quarter):
    w = x.shape[-1]
    return x * cos + pltpu.roll(x, w - quarter, 1) * sin_up + pltpu.roll(x, quarter, 1) * sin_dn


def _attn_kernel(*refs, sk_new, use_rope, has_cache, emit_kv):
    it = iter(refs)
    q_ref, kv_ref, gq_ref, gk_ref, bdq_ref, bdk_ref = (next(it) for _ in range(6))
    if use_rope:
        cq_ref, suq_ref, sdq_ref, ck_ref, suk_ref, sdk_ref = (next(it) for _ in range(6))
    if has_cache:
        kc_ref, vc_ref = next(it), next(it)
    o_ref = next(it)
    if emit_kv:
        nk_ref, nv_ref = next(it), next(it)
    kd_scr, vd_scr = next(it), next(it)

    lo = lax.broadcasted_iota(jnp.int32, (1, 2 * HEAD_DIM), 1) < HEAD_DIM

    def dup_halves(a):
        r = pltpu.roll(a, HEAD_DIM, 1)
        return jnp.where(lo, a, r), jnp.where(lo, r, a)

    @pl.when(pl.program_id(1) == 0)
    def _prep():
        kv = kv_ref[...].astype(F32)
        k, v = kv[:, :ATT_KV], kv[:, ATT_KV:]
        ms = jnp.dot((k * k).astype(BF16), bdk_ref[...], preferred_element_type=F32)
        kn = k * lax.rsqrt(ms + EPS) * gk_ref[...]
        if emit_kv:
            nk_ref[...] = kn
            nv_ref[...] = v
        if use_rope:
            kn = _rope(kn, ck_ref[...], suk_ref[...], sdk_ref[...], HEAD_DIM // 4)
        k0, k1 = dup_halves(kn)
        v0, v1 = dup_halves(v)
        kd_scr[0, 0:sk_new, :] = k0.astype(BF16)
        kd_scr[1, 0:sk_new, :] = k1.astype(BF16)
        vd_scr[0, 0:sk_new, :] = v0.astype(BF16)
        vd_scr[1, 0:sk_new, :] = v1.astype(BF16)
        if has_cache:
            c0, c1 = dup_halves(kc_ref[...])
            d0, d1 = dup_halves(vc_ref[...])
            kd_scr[0, sk_new:, :] = c0.astype(BF16)
            kd_scr[1, sk_new:, :] = c1.astype(BF16)
            vd_scr[0, sk_new:, :] = d0.astype(BF16)
            vd_scr[1, sk_new:, :] = d1.astype(BF16)

    q = q_ref[...].astype(F32)
    ms = jnp.dot((q * q).astype(BF16), bdq_ref[...], preferred_element_type=F32)
    qn = q * lax.rsqrt(ms + EPS) * gq_ref[...]
    if use_rope:
        qn = _rope(qn, cq_ref[...], suq_ref[...], sdq_ref[...], HEAD_DIM // 4)
    qn = qn * (HEAD_DIM ** -0.5)
    heads_per_kv = N_Q_HEADS // N_KV_HEADS
    for j in range(N_Q_HEADS // 2):
        grp = (2 * j) // heads_per_kv
        kd, vd = kd_scr[grp], vd_scr[grp]
        qp = qn[:, 2 * HEAD_DIM * j:2 * HEAD_DIM * (j + 1)]
        halves = []
        for qm in (jnp.where(lo, qp, 0.0), jnp.where(lo, 0.0, qp)):
            s = lax.dot_general(qm.astype(BF16), kd, (((1,), (1,)), ((), ())), preferred_element_type=F32)
            e = jnp.exp(s - jnp.max(s, axis=-1, keepdims=True))
            den = jnp.sum(e, axis=-1, keepdims=True)
            halves.append(jnp.dot(e.astype(BF16), vd, preferred_element_type=F32) / den)
        o_ref[:, 2 * HEAD_DIM * j:2 * HEAD_DIM * (j + 1)] = jnp.where(lo, halves[0], halves[1]).astype(BF16)


def _block_diag_mean(width, group):
    idx = np.arange(width) // group
    return jnp.asarray((idx[:, None] == idx[None, :]).astype(np.float32) / group, dtype=BF16)


def _attention(l, z, q_norm, k_norm, *, latent, rope=None, cache_k=None, cache_v=None):
    if latent:
        nb, s, tq, row0 = DEC_BATCH, DEC_SEQ, 256, T_CTX
    else:
        nb, s, tq, row0 = BATCH, SEQ, SEQ, 0
    nq = s // tq
    sk = s + (PAST_LEN if latent else 0)
    qrow = lambda b, qi: (row0 // tq + b * nq + qi, 0)
    krow = lambda b, qi: (row0 // s + b, 0)
    const = lambda b, qi: (0, 0)
    lrow = lambda b, qi: (l, 0, 0)
    in_specs = [pl.BlockSpec((tq, ATT_Q), qrow),
                pl.BlockSpec((s, 2 * ATT_KV), krow),
                pl.BlockSpec((None, 1, ATT_Q), lrow),
                pl.BlockSpec((None, 1, ATT_KV), lrow),
                pl.BlockSpec((ATT_Q, ATT_Q), const),
                pl.BlockSpec((ATT_KV, ATT_KV), const)]
    args = [z["qa"], z["kv"],
            jnp.tile(q_norm, (1, N_Q_HEADS)).reshape(DEPTH, 1, ATT_Q),
            jnp.tile(k_norm, (1, N_KV_HEADS)).reshape(DEPTH, 1, ATT_KV),
            _block_diag_mean(ATT_Q, HEAD_DIM), _block_diag_mean(ATT_KV, HEAD_DIM)]
    if latent:
        cq, suq, sdq, ck, suk, sdk = rope
        in_specs += [pl.BlockSpec((tq, ATT_Q), lambda b, qi: (qi, 0))] * 3
        in_specs += [pl.BlockSpec((s, ATT_KV), const)] * 3
        in_specs += [pl.BlockSpec((None, None, PAST_LEN, ATT_KV), lambda b, qi: (b, l, 0, 0))] * 2
        args += [cq, suq, sdq, ck, suk, sdk,
                 cache_k.reshape(DEC_BATCH, DEPTH, PAST_LEN, ATT_KV),
                 cache_v.reshape(DEC_BATCH, DEPTH, PAST_LEN, ATT_KV)]
    rows = nb * s
    out_specs = [pl.BlockSpec((tq, ATT_Q), lambda b, qi: (b * nq + qi, 0))]
    out_shape = [jax.ShapeDtypeStruct((rows, ATT_Q), BF16)]
    if not latent:
        out_specs += [pl.BlockSpec((s, ATT_KV), lambda b, qi: (b, 0))] * 2
        out_shape += [jax.ShapeDtypeStruct((rows, ATT_KV), F32)] * 2
    return pl.pallas_call(
        functools.partial(_attn_kernel, sk_new=s, use_rope=latent, has_cache=latent, emit_kv=not latent),
        grid=(nb, nq),
        in_specs=in_specs, out_specs=out_specs, out_shape=out_shape,
        scratch_shapes=[pltpu.VMEM((N_KV_HEADS, sk, 2 * HEAD_DIM), BF16),
                        pltpu.VMEM((N_KV_HEADS, sk, 2 * HEAD_DIM), BF16)],
        compiler_params=_params(("arbitrary", "arbitrary"), V7X_VMEM_LIMIT_BYTES),
        name="attention_latent" if latent else "attention_context",
    )(*args)


def _ret_kernel(*refs, s, heads, use_rope, has_state):
    it = iter(refs)
    q_ref, k_ref, v_ref, g_ref, dec_ref, gn_ref = (next(it) for _ in range(6))
    rope = (next(it)[...], next(it)[...], next(it)[...]) if use_rope else None
    s0f_ref, s0b_ref = (next(it), next(it)) if has_state else (None, None)
    o_ref, sf_ref, sb_ref, oacc = next(it), next(it), next(it), next(it)
    for hh in range(heads):
        cols = slice(hh * RET_DK, (hh + 1) * RET_DK)
        _ret_head(q_ref.at[:, cols], k_ref.at[:, cols], v_ref.at[:, cols], g_ref.at[:, cols],
                  dec_ref.at[:, hh], gn_ref.at[hh], rope,
                  s0f_ref.at[hh] if has_state else None, s0b_ref.at[hh] if has_state else None,
                  o_ref.at[:, cols], sf_ref.at[hh], sb_ref.at[hh], oacc.at[hh], s)


def _ret_head(q_ref, k_ref, v_ref, g_ref, dec_ref, gn_ref, rope, s0f_ref, s0b_ref, o_ref, sf_ref, sb_ref, oacc, s):
    use_rope, has_state = rope is not None, s0f_ref is not None
    n_chunks = s // BLOCK
    lgf = -jnp.exp(dec_ref[0])
    lgb = -jnp.exp(dec_ref[1])
    lgf1, lgb1 = lgf[:, 0:1], lgb[:, 0:1]
    diff = (lax.broadcasted_iota(jnp.int32, (BLOCK, BLOCK), 0)
            - lax.broadcasted_iota(jnp.int32, (BLOCK, BLOCK), 1)).astype(F32)
    dsum = (jnp.where(diff >= 0, jnp.exp(diff * lgf), 0.0)
            + jnp.where(diff <= 0, jnp.exp(-diff * lgb), 0.0))
    ic = lax.broadcasted_iota(jnp.int32, (BLOCK, 1), 0).astype(F32)
    qdf, kdf, cdf = jnp.exp((ic + 1.0) * lgf1), jnp.exp((BLOCK - 1.0 - ic) * lgf1), jnp.exp(BLOCK * lgf1)
    qdb, kdb, cdb = jnp.exp((BLOCK - ic) * lgb1), jnp.exp(ic * lgb1), jnp.exp(BLOCK * lgb1)

    q = q_ref[...].astype(F32)
    k = k_ref[...].astype(F32) * (RET_DK ** -0.5)
    if use_rope:
        q = _rope(q, *rope, RET_DK // 4)
        k = _rope(k, *rope, RET_DK // 4)
    v = v_ref[...]

    def rows(a, n):
        return a[n * BLOCK:(n + 1) * BLOCK]

    def state_step(state, kn, kdec, vn, cdec):
        kd_t = jnp.transpose(kn * kdec).astype(BF16)
        return cdec * state + jnp.dot(kd_t, vn, preferred_element_type=F32)

    state = s0f_ref[...] if has_state else jnp.zeros((RET_DK, RET_DV), F32)
    for n in range(n_chunks):
        qn, kn, vn = rows(q, n), rows(k, n), rows(v, n)
        inner = lax.dot_general(qn.astype(BF16), kn.astype(BF16), (((1,), (1,)), ((), ())),
                                preferred_element_type=F32) * dsum
        o = (jnp.dot(inner.astype(BF16), vn, preferred_element_type=F32)
             + jnp.dot((qn * qdf).astype(BF16), state.astype(BF16), preferred_element_type=F32))
        state = state_step(state, kn, kdf, vn, cdf)
        oacc[n * BLOCK:(n + 1) * BLOCK, :] = o
    sf_ref[...] = state

    state = s0b_ref[...] if has_state else jnp.zeros((RET_DK, RET_DV), F32)
    for n in reversed(range(n_chunks)):
        qn, kn, vn = rows(q, n), rows(k, n), rows(v, n)
        o = rows(oacc, n) + jnp.dot((qn * qdb).astype(BF16), state.astype(BF16), preferred_element_type=F32)
        state = state_step(state, kn, kdb, vn, cdb)
        mu = jnp.mean(o, axis=-1, keepdims=True)
        var = jnp.mean(jnp.square(o - mu), axis=-1, keepdims=True)
        y = (o - mu) * lax.rsqrt(var + EPS) * gn_ref[...]
        gate = g_ref[n * BLOCK:(n + 1) * BLOCK, :].astype(F32)
        o_ref[n * BLOCK:(n + 1) * BLOCK, :] = (y * gate).astype(BF16)
    sb_ref[...] = state


def _retention(l, z, dec, ret_gn, *, latent, rope=None, s0f=None, s0b=None):
    if latent:
        nb, s, row0, heads = DEC_BATCH, DEC_SEQ, T_CTX, 1
    else:
        nb, s, row0, heads = BATCH, SEQ, 0, N_RET_HEADS
    width = heads * RET_DK
    hrow = lambda b, h: (row0 // s + b, h)
    in_specs = [pl.BlockSpec((s, width), hrow)] * 4
    in_specs += [pl.BlockSpec((None, 2, heads, 1, RET_DK), lambda b, h: (l, 0, h, 0, 0)),
                 pl.BlockSpec((None, heads, 1, RET_DV), lambda b, h: (l, h, 0, 0))]
    args = [z["qr"], z["kr"], z["vr"], z["gr"], dec, ret_gn.reshape(DEPTH, N_RET_HEADS, 1, RET_DV)]
    if latent:
        in_specs += [pl.BlockSpec((s, RET_DK), lambda b, h: (0, 0))] * 3
        in_specs += [pl.BlockSpec((None, None, heads, RET_DK, RET_DV), lambda b, h: (b, l, h, 0, 0))] * 2
        args += [*rope, s0f, s0b]
    st_spec = pl.BlockSpec((None, heads, RET_DK, RET_DV), lambda b, h: (b, h, 0, 0))
    st_shape = jax.ShapeDtypeStruct((nb, N_RET_HEADS, RET_DK, RET_DV), F32)
    return pl.pallas_call(
        functools.partial(_ret_kernel, s=s, heads=heads, use_rope=latent, has_state=latent),
        grid=(nb, N_RET_HEADS // heads),
        in_specs=in_specs,
        out_specs=[pl.BlockSpec((s, width), lambda b, h: (b, h)), st_spec, st_spec],
        out_shape=[jax.ShapeDtypeStruct((nb * s, RET_W), BF16), st_shape, st_shape],
        scratch_shapes=[pltpu.VMEM((heads, s, RET_DV), F32)],
        compiler_params=_params(("arbitrary", "arbitrary")),
        name="retention_latent" if latent else "retention_context",
    )(*args)


def _first_max(vals):
    best, idx = vals[0], jnp.zeros(vals[0].shape, jnp.int32)
    for j in range(1, len(vals)):
        upd = vals[j] > best
        best = jnp.where(upd, vals[j], best)
        idx = jnp.where(upd, j, idx)
    return best, idx


def _pick(idx, vals):
    out = vals[-1]
    for j in range(len(vals) - 2, -1, -1):
        out = jnp.where(idx == j, vals[j], out)
    return out


def _merge_kernel(x_ref, attc_ref, attl_ref, retc_ref, retl_ref, cm_ref, ga_ref, gb_ref, gc_ref, mod_ref, g2_ref,
                  wa_ref, wr_ref, wm_ref, wo_ref, wrt_ref, br_ref, tri_ref,
                  x1_ref, h2_ref, eid_ref, wts_ref, rank_ref, cnt_ref, cnt_scr):
    @pl.when(pl.program_id(0) == 0)
    def _zero_counts():
        cnt_scr[...] = jnp.zeros_like(cnt_scr)

    is_lat = pl.program_id(0) >= T_CTX // ROW_TILE
    att = jnp.where(is_lat, attl_ref[...], attc_ref[...])
    ret = jnp.where(is_lat, retl_ref[...], retc_ref[...])
    gate = lambda r: r[...].astype(F32)
    dot = lambda a, b: jnp.dot(a, b, preferred_element_type=F32)
    mix = (gate(ga_ref) * dot(att, wa_ref[...])
           + gate(gb_ref) * dot(ret, wr_ref[...])
           + gate(gc_ref) * dot(cm_ref[...], wm_ref[...]))
    x1 = x_ref[...] + mod_ref[2:3, :] * dot(mix.astype(BF16), wo_ref[...])
    x1_ref[...] = x1
    h2 = _rms(x1, g2_ref[...]) * (1.0 + mod_ref[4:5, :]) + mod_ref[3:4, :]
    _store_row_tiles(h2_ref, h2)

    logits = lax.dot_general(wrt_ref[...].astype(BF16), h2.astype(BF16), (((1,), (1,)), ((), ())),
                             preferred_element_type=F32)
    score = _sigmoid(logits)
    sel = score + br_ref[...]
    sel_rows = [sel[e:e + 1, :] for e in range(N_EXPERTS)]
    score_rows = [score[e:e + 1, :] for e in range(N_EXPERTS)]
    group_scores = []
    for g in range(N_EXPERT_GROUPS):
        v = sel_rows[g * EXPERTS_PER_GROUP:(g + 1) * EXPERTS_PER_GROUP]
        pair_sums = [v[a] + v[b] for a in range(EXPERTS_PER_GROUP) for b in range(a + 1, EXPERTS_PER_GROUP)]
        group_scores.append(functools.reduce(jnp.maximum, pair_sums))
    _, gidx = _first_max(group_scores)
    in_sel = [_pick(gidx, [sel_rows[g * EXPERTS_PER_GROUP + j] for g in range(N_EXPERT_GROUPS)])
              for j in range(EXPERTS_PER_GROUP)]
    in_score = [_pick(gidx, [score_rows[g * EXPERTS_PER_GROUP + j] for g in range(N_EXPERT_GROUPS)])
                for j in range(EXPERTS_PER_GROUP)]
    _, loc0 = _first_max(in_sel)
    _, loc1 = _first_max([jnp.where(loc0 == j, -jnp.inf, in_sel[j]) for j in range(EXPERTS_PER_GROUP)])
    w0, w1 = _pick(loc0, in_score), _pick(loc1, in_score)
    den = w0 + w1
    e0 = gidx * EXPERTS_PER_GROUP + loc0
    e1 = gidx * EXPERTS_PER_GROUP + loc1
    eid_ref[0:1, :] = e0
    eid_ref[1:2, :] = e1
    wts_ref[0:1, :] = w0 / den
    wts_ref[1:2, :] = w1 / den

    eio = lax.broadcasted_iota(jnp.int32, score.shape, 0)
    oh0 = (eio == e0).astype(F32)
    oh1 = (eio == e1).astype(F32)
    both = oh0 + oh1
    before = jnp.dot(both.astype(BF16), tri_ref[...], preferred_element_type=F32) + cnt_scr[...]
    rank_ref[0:1, :] = jnp.sum(oh0 * before, axis=0, keepdims=True).astype(jnp.int32)
    rank_ref[1:2, :] = jnp.sum(oh1 * before, axis=0, keepdims=True).astype(jnp.int32)
    cnt_scr[...] = cnt_scr[...] + jnp.sum(both, axis=1, keepdims=True)
    cnt_ref[...] = cnt_scr[...]


def _merge(l, x, att_c, att_l, ret_c, ret_l, cm, z, mods, norm2, wa, wr, wm, wo, w_router_t, b_router):
    tm = ROW_TILE
    row = lambda i: (i, 0)
    n_ctx = T_CTX // tm
    ctx_row = lambda i: (jnp.minimum(i, n_ctx - 1), 0)
    lat_row = lambda i: (jnp.maximum(i - n_ctx, 0), 0)
    lw = lambda k, n: pl.BlockSpec((None, k, n), lambda i: (l, 0, 0))
    in_specs = [pl.BlockSpec((tm, D_MODEL), row),
                pl.BlockSpec((tm, ATT_Q), ctx_row), pl.BlockSpec((tm, ATT_Q), lat_row),
                pl.BlockSpec((tm, RET_W), ctx_row), pl.BlockSpec((tm, RET_W), lat_row),
                pl.BlockSpec((tm, MLP_W), row),
                pl.BlockSpec((tm, D_MODEL), row), pl.BlockSpec((tm, D_MODEL), row), pl.BlockSpec((tm, D_MODEL), row),
                pl.BlockSpec((None, None, N_MOD, D_MODEL), lambda i: (l, _mod_row(i), 0, 0)),
                pl.BlockSpec((None, 1, D_MODEL), lambda i: (l, 0, 0)),
                lw(ATT_Q, D_MODEL), lw(RET_W, D_MODEL), lw(MLP_W, D_MODEL), lw(D_MODEL, D_MODEL),
                pl.BlockSpec((N_EXPERTS, D_MODEL), lambda i: (0, 0)),
                pl.BlockSpec((N_EXPERTS, 1), lambda i: (0, 0)),
                pl.BlockSpec((tm, tm), lambda i: (0, 0))]
    lane = lambda i: (0, i)
    tri = jnp.asarray(np.triu(np.ones((tm, tm), np.float32), k=1), dtype=BF16)
    return pl.pallas_call(
        _merge_kernel,
        grid=(T_ALL // tm,),
        in_specs=in_specs,
        out_specs=[pl.BlockSpec((tm, D_MODEL), row), pl.BlockSpec((tm * TILE_ROWS, 128), row),
                   pl.BlockSpec((TOP_K, tm), lane), pl.BlockSpec((TOP_K, tm), lane),
                   pl.BlockSpec((TOP_K, tm), lane), pl.BlockSpec((N_EXPERTS, 1), lambda i: (0, 0))],
        out_shape=[jax.ShapeDtypeStruct((T_ALL, D_MODEL), F32), jax.ShapeDtypeStruct((T_ALL * TILE_ROWS, 128), F32),
                   jax.ShapeDtypeStruct((TOP_K, T_ALL), jnp.int32), jax.ShapeDtypeStruct((TOP_K, T_ALL), F32),
                   jax.ShapeDtypeStruct((TOP_K, T_ALL), jnp.int32), jax.ShapeDtypeStruct((N_EXPERTS, 1), F32)],
        scratch_shapes=[pltpu.VMEM((N_EXPERTS, 1), F32)],
        compiler_params=_params(("arbitrary",), V7X_VMEM_LIMIT_BYTES),
        name="merge_router",
    )(x, att_c, att_l, ret_c, ret_l, cm, z["ga"], z["gb"], z["gc"], mods, norm2.reshape(DEPTH, 1, D_MODEL),
      wa, wr, wm, wo, w_router_t, b_router.reshape(N_EXPERTS, 1), tri)


def _dispatch_plan(eid, rank, counts):
    expert = jnp.arange(N_EXPERTS, dtype=jnp.int32)
    padded = (counts + MOE_BLOCK - 1) // MOE_BLOCK * MOE_BLOCK
    pend = jnp.cumsum(padded)
    pstart = pend - padded
    dest = rank + jnp.sum(jnp.where(eid[..., None] == expert, pstart, 0), axis=-1)
    block_start = jnp.arange(N_MOE_BLOCKS, dtype=jnp.int32) * MOE_BLOCK
    block_e = jnp.minimum(jnp.sum((pend[None, :] <= block_start[:, None]).astype(jnp.int32), axis=1),
                          N_EXPERTS - 1)
    pad_ranges = jnp.stack([jnp.append(pstart + counts, pend[-1]),
                            jnp.append(pend, N_SLOTS)]).astype(jnp.int32)
    return block_e.astype(jnp.int32), pad_ranges, dest.reshape(N_ASSIGN)


ROWS_PER_TRIP = 8


N_DMA_PRIORITIES = 2


def _moe_kernel(be_ref, pad_ref, dest_ref, h_ref, wg_ref, wu_ref, wd_ref, o_ref, xbuf, wg_s, wu_s, wd_s, src_s):
    i = pl.program_id(0)
    last = pl.num_programs(0) - 1

    def gather(blk, buf):
        for r in range(MOE_BLOCK):
            first = pl.multiple_of(src_s[blk * MOE_BLOCK + r], TILE_ROWS)
            xbuf[buf, r * TILE_ROWS:(r + 1) * TILE_ROWS, :] = h_ref[pl.ds(first, TILE_ROWS), :]

    @pl.when(i == 0)
    def _init():
        def clear(j, c):
            src_s[j] = 0
            return c
        for e in range(N_EXPERTS + 1):
            lax.fori_loop(pad_ref[0, e], pad_ref[1, e], clear, 0)

        def put(g, c):
            for u in range(ROWS_PER_TRIP):
                t = g * ROWS_PER_TRIP + u
                for k in range(TOP_K):
                    src_s[dest_ref[k * T_ALL + t]] = t * TILE_ROWS
            return c
        lax.fori_loop(0, T_ALL // ROWS_PER_TRIP, put, 0)
        gather(0, 0)

    prev = be_ref[jnp.maximum(i - 1, 0)]
    used = i * MOE_BLOCK < pad_ref[0, N_EXPERTS]

    @pl.when(used & ((i == 0) | (be_ref[i] != prev)))
    def _cast():
        wg_s[...] = wg_ref[...].astype(BF16)
        wu_s[...] = wu_ref[...].astype(BF16)
        wd_s[...] = wd_ref[...].astype(BF16)

    @pl.when(used)
    def _block():
        buf = lax.rem(i, 2)
        gather(jnp.minimum(i + 1, last), 1 - buf)
        x = _load_row_tiles(xbuf.at[buf]).astype(BF16)
        a = jnp.dot(x, wg_s[...], preferred_element_type=F32)
        b = jnp.dot(x, wu_s[...], preferred_element_type=F32)
        mid = ((a * _sigmoid(a)) * b).astype(BF16)
        _store_row_tiles(o_ref, jnp.dot(mid, wd_s[...], preferred_element_type=F32))

    @pl.when(jnp.logical_not(used))
    def _unused():
        o_ref[...] = jnp.zeros_like(o_ref)


def _moe(l, h2, plan, w_gate, w_up, w_down):
    block_e, pad_ranges, dest = plan
    once = pl.Buffered(1)
    wspec = lambda k, n: pl.BlockSpec((None, None, k, n), lambda i, be, pr, de: (l, be[i], 0, 0))
    grid_spec = pltpu.PrefetchScalarGridSpec(
        num_scalar_prefetch=3,
        grid=(N_MOE_BLOCKS,),
        in_specs=[pl.BlockSpec((T_ALL * TILE_ROWS, 128), lambda i, be, pr, de: (0, 0), pipeline_mode=once),
                  wspec(D_MODEL, EXPERT_FF), wspec(D_MODEL, EXPERT_FF), wspec(EXPERT_FF, D_MODEL)],
        out_specs=pl.BlockSpec((MOE_BLOCK * TILE_ROWS, 128), lambda i, be, pr, de: (i, 0)),
        scratch_shapes=[pltpu.VMEM((2, MOE_BLOCK * TILE_ROWS, 128), F32),
                        pltpu.VMEM((D_MODEL, EXPERT_FF), BF16), pltpu.VMEM((D_MODEL, EXPERT_FF), BF16),
                        pltpu.VMEM((EXPERT_FF, D_MODEL), BF16),
                        pltpu.SMEM((N_SLOTS,), jnp.int32)])
    return pl.pallas_call(
        _moe_kernel,
        grid_spec=grid_spec,
        out_shape=jax.ShapeDtypeStruct((N_SLOTS * TILE_ROWS, 128), F32),
        compiler_params=_params(("arbitrary",), V7X_MOE_VMEM_LIMIT_BYTES),
        name="moe_experts",
    )(block_e, pad_ranges, dest, h2, w_gate, w_up, w_down)


def _start_expert_rows(dest_ref, y_hbm, ybuf, sem, tile_idx, slot):
    base = tile_idx * ROW_TILE
    for k in range(TOP_K):
        for t in range(ROW_TILE):
            first = pl.multiple_of(dest_ref[k * T_ALL + base + t], TILE_ROWS)
            pltpu.make_async_copy(y_hbm.at[pl.ds(first, TILE_ROWS)],
                                  ybuf.at[slot, k, pl.ds(t * TILE_ROWS, TILE_ROWS)],
                                  sem.at[slot]).start(priority=t % N_DMA_PRIORITIES)


def _wait_expert_rows(y_hbm, ybuf, sem, slot):
    for k in range(TOP_K):
        pltpu.make_async_copy(y_hbm.at[pl.ds(0, ROW_TILE * TILE_ROWS)], ybuf.at[slot, k], sem.at[slot]).wait()


def _gathered_expert_rows(dest_ref, y_hbm, ybuf, sem):
    i = pl.program_id(0)
    last = pl.num_programs(0) - 1
    slot = lax.rem(i, 2)

    @pl.when(i == 0)
    def _first():
        _start_expert_rows(dest_ref, y_hbm, ybuf, sem, 0, 0)

    _start_expert_rows(dest_ref, y_hbm, ybuf, sem, jnp.minimum(i + 1, last), 1 - slot)
    _wait_expert_rows(y_hbm, ybuf, sem, slot)
    return _load_row_tiles(ybuf.at[slot, 0]), _load_row_tiles(ybuf.at[slot, 1])


def _finish_expert_rows(y_hbm, ybuf, sem):
    i = pl.program_id(0)

    @pl.when(i == pl.num_programs(0) - 1)
    def _():
        _wait_expert_rows(y_hbm, ybuf, sem, 1 - lax.rem(i, 2))


def _combine_kernel(dest_ref, x_ref, y_hbm, wc_ref, mod_ref, oc_ref, ol_ref, ybuf, sem):
    wc = wc_ref[...]
    y0, y1 = _gathered_expert_rows(dest_ref, y_hbm, ybuf, sem)
    x = x_ref[...] + mod_ref[5:6, :] * (wc[:, 0:1] * y0 + wc[:, 1:2] * y1)
    _finish_expert_rows(y_hbm, ybuf, sem)
    is_lat = pl.program_id(0) >= T_CTX // ROW_TILE

    @pl.when(jnp.logical_not(is_lat))
    def _():
        oc_ref[...] = x

    @pl.when(is_lat)
    def _():
        ol_ref[...] = x


def _expert_rows_scratch():
    return [pltpu.VMEM((2, TOP_K, ROW_TILE * TILE_ROWS, 128), F32), pltpu.SemaphoreType.DMA((2,))]


def _final_combine(l, x, moe_out, dest_rows, wcol, mods):
    tm = ROW_TILE
    nt = T_ALL // tm
    n_ctx = T_CTX // tm
    row = lambda i, de: (i, 0)
    grid_spec = pltpu.PrefetchScalarGridSpec(
        num_scalar_prefetch=1,
        grid=(nt,),
        in_specs=[pl.BlockSpec((tm, D_MODEL), row), pl.BlockSpec(memory_space=pl.ANY),
                  pl.BlockSpec((tm, TOP_K), row),
                  pl.BlockSpec((None, None, N_MOD, D_MODEL), lambda i, de: (l, _mod_row(i), 0, 0))],
        out_specs=[pl.BlockSpec((tm, D_MODEL), lambda i, de: (jnp.minimum(i, n_ctx - 1), 0)),
                   pl.BlockSpec((tm, D_MODEL), lambda i, de: (jnp.maximum(i - n_ctx, 0), 0))],
        scratch_shapes=_expert_rows_scratch())
    return pl.pallas_call(
        _combine_kernel,
        grid_spec=grid_spec,
        out_shape=[jax.ShapeDtypeStruct((T_CTX, D_MODEL), F32), jax.ShapeDtypeStruct((T_LAT, D_MODEL), F32)],
        compiler_params=_params(("arbitrary",)),
        name="final_combine",
    )(dest_rows, x, moe_out, wcol, mods)


def kernel(x_prompt, x_sample, cache_k, cache_v, state_ret_fwd, state_ret_bwd, c, c_ctx, w_mod, b_mod, norm1, norm2, w_in, q_norm, k_norm, ret_decay_fwd, ret_decay_bwd, ret_gn, mlp_norm, w_spatial, b_spatial, w_att_o, w_ret_o, w_mlp_o, w_out, w_router, b_router, w_gate, w_up, w_down):
    cond = jnp.zeros((8, D_MODEL), F32).at[0].set(c_ctx).at[1:1 + DEC_BATCH].set(c)
    mods = _modulation(cond, w_mod, b_mod)

    w_in_bf = w_in.astype(BF16)
    wa, wr, wm, wo = (w.astype(BF16) for w in (w_att_o, w_ret_o, w_mlp_o, w_out))
    ws_bf = w_spatial.astype(BF16)
    w_router_t = w_router.T
    dec = jnp.broadcast_to(jnp.stack([ret_decay_fwd, ret_decay_bwd], axis=1)[..., None, None],
                           (DEPTH, 2, N_RET_HEADS, 1, RET_DK))
    rows = DEC_SEQ // GRID_W
    rope_q = _rope_tables(rows, HEAD_DIM, N_Q_HEADS)
    rope_k = _rope_tables(rows, HEAD_DIM, N_KV_HEADS)
    rope_r = _rope_tables(rows, RET_DK, 1)

    x = (x_prompt.reshape(T_CTX, D_MODEL), x_sample.reshape(T_LAT, D_MODEL))
    moe_out, dest_rows, wcol = None, None, None
    ks, vs, sfs, sbs = [], [], [], []
    for l in range(DEPTH):
        x, z = _input_projection(l, x, mods, norm1, w_in_bf, mlp_norm, ws_bf, b_spatial, moe_out, dest_rows, wcol)

        att_c, k_l, v_l = _attention(l, z, q_norm, k_norm, latent=False)
        att_l, = _attention(l, z, q_norm, k_norm, latent=True, rope=rope_q + rope_k,
                            cache_k=cache_k, cache_v=cache_v)
        ret_c, sf_l, sb_l = _retention(l, z, dec, ret_gn, latent=False)
        ret_l, _, _ = _retention(l, z, dec, ret_gn, latent=True, rope=rope_r,
                                 s0f=state_ret_fwd, s0b=state_ret_bwd)
        x, h2, eid, wts, rank, cnt = _merge(l, x, att_c, att_l, ret_c, ret_l, z["cm"], z, mods, norm2,
                                            wa, wr, wm, wo, w_router_t, b_router)
        plan = _dispatch_plan(eid, rank, cnt[:, 0].astype(jnp.int32))
        moe_out = _moe(l, h2, plan, w_gate, w_up, w_down)
        dest_rows = plan[2] * TILE_ROWS
        wcol = wts.T
        ks.append(k_l.reshape(BATCH, SEQ, ATT_KV))
        vs.append(v_l.reshape(BATCH, SEQ, ATT_KV))
        sfs.append(sf_l)
        sbs.append(sb_l)
    y_ctx, y_lat = _final_combine(DEPTH - 1, x, moe_out, dest_rows, wcol, mods)

    y_prompt = y_ctx.reshape(BATCH, SEQ, D_MODEL)
    y_sample = y_lat.reshape(DEC_BATCH, DEC_SEQ, D_MODEL)
    new_k = jnp.stack(ks, axis=1).reshape(BATCH, DEPTH, SEQ, N_KV_HEADS, HEAD_DIM)
    new_v = jnp.stack(vs, axis=1).reshape(BATCH, DEPTH, SEQ, N_KV_HEADS, HEAD_DIM)
    new_sf = jnp.stack(sfs, axis=1)
    new_sb = jnp.stack(sbs, axis=1)
    return (y_prompt, y_sample, new_k, new_v, new_sf, new_sb)
```
